```python
import jax, jax.numpy as jnp
from jax import lax
import numpy as np

D_MODEL = 1024
BATCH = 8
SEQ = 2048
DEPTH = 1

N_HEADS = 8
N_KV = 2
GQ = N_HEADS // N_KV
HD = 64
Q_WIDTH = N_HEADS * HD
KV_WIDTH = N_KV * HD
WINDOW = 128
BLK = 128
N_FGROUPS = 4
FG_DIM = 128
F_WIDTH = N_FGROUPS * FG_DIM
N_BUCKETS = 32
MAX_DIST = 128
D_FF = 2816
EPS = 1e-6
NEG = -1e30
IN_WIDTH = Q_WIDTH + 2 * KV_WIDTH + F_WIDTH + 2 * D_MODEL

kernel_name = "hybrid_window_gqa_fnet_gated_macaron"


def _rms(x, g):
    xf = x.astype(jnp.float32)
    y = xf * lax.rsqrt(jnp.mean(xf * xf, axis=-1, keepdims=True) + EPS)
    return (y * g.astype(jnp.float32)).astype(x.dtype)


def _swiglu(h, w_up, w_down):
    gate, up = jnp.split(h @ w_up, 2, axis=-1)
    return (jax.nn.silu(gate) * up) @ w_down


def _t5_bucket_and_rel():
    rel = (np.arange(3 * BLK)[None, :] - BLK) - np.arange(BLK)[:, None]
    half = N_BUCKETS // 2
    max_exact = half // 2
    ret = (rel > 0).astype(np.int32) * half
    n = np.abs(rel)
    n_safe = np.maximum(n, 1).astype(np.float32)
    large = max_exact + (np.log(n_safe / max_exact) / np.log(MAX_DIST / max_exact)
                         * (half - max_exact)).astype(np.int32)
    large = np.minimum(large, half - 1)
    bucket = ret + np.where(n < max_exact, n, large)
    return rel, bucket.astype(np.int32)


def _window_attention(q, k, v, sink, rel_bias):
    B, S = q.shape[0], q.shape[1]
    nb = S // BLK
    qb = q.reshape(B, nb, BLK, N_KV, GQ, HD).astype(jnp.float32)
    pad = ((0, 0), (BLK, BLK), (0, 0))
    kp = jnp.pad(k, pad).reshape(B, nb + 2, BLK, N_KV, HD)
    vp = jnp.pad(v, pad).reshape(B, nb + 2, BLK, N_KV, HD)

    def win(t):
        return jnp.concatenate([t[:, :-2], t[:, 1:-1], t[:, 2:]], axis=2).astype(jnp.float32)

    kw, vw = win(kp), win(vp)
    s = jnp.einsum('bnqkgd,bnjkd->bnkgqj', qb, kw) * (HD ** -0.5)

    rel, bucket = _t5_bucket_and_rel()
    bias = rel_bias.astype(jnp.float32)[bucket]
    bias = jnp.transpose(bias, (2, 0, 1)).reshape(N_KV, GQ, BLK, 3 * BLK)
    key_idx = np.arange(nb)[:, None] * BLK - BLK + np.arange(3 * BLK)[None, :]
    mask = (np.abs(rel) <= WINDOW)[None] & ((key_idx >= 0) & (key_idx < S))[:, None, :]
    mask = jnp.asarray(mask)[None, :, None, None]
    s = jnp.where(mask, s + bias, NEG)

    sk = sink.astype(jnp.float32).reshape(1, 1, N_KV, GQ, 1, 1)
    m = jnp.maximum(s.max(axis=-1, keepdims=True), sk)
    p = jnp.exp(s - m)
    w = p / (p.sum(axis=-1, keepdims=True) + jnp.exp(sk - m))
    o = jnp.einsum('bnkgqj,bnjkd->bnqkgd', w, vw)
    return o.reshape(B, S, Q_WIDTH).astype(q.dtype)


def _fourier_mix(f):
    B, S = f.shape[0], f.shape[1]
    fg = f.astype(jnp.float32).reshape(B, S, N_FGROUPS, FG_DIM)
    y = jnp.fft.fft2(fg, axes=(1, 3), norm="ortho").real
    return y.reshape(B, S, F_WIDTH).astype(f.dtype)


def setup_inputs(seed: int = 0) -> dict:
    key = jax.random.key(seed)
    ks = jax.random.split(key, 20)
    f32 = jnp.float32

    def w(k, shape, fan_in):
        return jax.random.normal(k, shape, f32) * (fan_in ** -0.5)

    def gain(k, shape):
        return 1.0 + 0.02 * jax.random.normal(k, shape, f32)

    return {
        "x": jax.random.normal(ks[0], (BATCH, SEQ, D_MODEL), f32),
        "g_ffn1": gain(ks[1], (DEPTH, D_MODEL)),
        "w_up1": w(ks[2], (DEPTH, D_MODEL, 2 * D_FF), D_MODEL),
        "w_down1": w(ks[3], (DEPTH, D_FF, D_MODEL), D_FF),
        "g_mix": gain(ks[4], (DEPTH, D_MODEL)),
        "w_in": w(ks[5], (DEPTH, D_MODEL, IN_WIDTH), D_MODEL),
        "b_gate": 0.01 * jax.random.normal(ks[6], (DEPTH, 2 * D_MODEL), f32),
        "sink": 0.5 * jax.random.normal(ks[7], (DEPTH, N_HEADS), f32),
        "rel_bias": 0.5 * jax.random.normal(ks[8], (N_BUCKETS, N_HEADS), f32),
        "w_branch_a": w(ks[9], (DEPTH, Q_WIDTH, D_MODEL), Q_WIDTH),
        "w_branch_b": w(ks[10], (DEPTH, F_WIDTH, D_MODEL), F_WIDTH),
        "w_out": w(ks[11], (DEPTH, D_MODEL, D_MODEL), D_MODEL),
        "g_ffn2": gain(ks[12], (DEPTH, D_MODEL)),
        "w_up2": w(ks[13], (DEPTH, D_MODEL, 2 * D_FF), D_MODEL),
        "w_down2": w(ks[14], (DEPTH, D_FF, D_MODEL), D_FF),
        "g_final": gain(ks[15], (D_MODEL,)),
    }


def reference(x, g_ffn1, w_up1, w_down1, g_mix, w_in, b_gate, sink, rel_bias,
              w_branch_a, w_branch_b, w_out, g_ffn2, w_up2, w_down2, g_final):
    o_q = Q_WIDTH
    o_k = o_q + KV_WIDTH
    o_v = o_k + KV_WIDTH
    o_f = o_v + F_WIDTH
    for l in range(DEPTH):
        x = x + 0.5 * _swiglu(_rms(x, g_ffn1[l]), w_up1[l], w_down1[l])
        h = _rms(x, g_mix[l])
        z = h @ w_in[l]
        q, k, v = z[..., :o_q], z[..., o_q:o_k], z[..., o_k:o_v]
        f = z[..., o_v:o_f]
        gates = jax.nn.sigmoid(z[..., o_f:] + b_gate[l])
        gate_a, gate_b = gates[..., :D_MODEL], gates[..., D_MODEL:]
        y_a = _window_attention(q, k, v, sink[l], rel_bias) @ w_branch_a[l]
        y_b = _fourier_mix(f) @ w_branch_b[l]
        x = x + (gate_a * y_a + gate_b * y_b) @ w_out[l]
        x = x + 0.5 * _swiglu(_rms(x, g_ffn2[l]), w_up2[l], w_down2[l])
    return _rms(x, g_final)
```

```python
import functools

import numpy as np
import jax
import jax.numpy as jnp
from jax import lax
from jax.experimental import pallas as pl
from jax.experimental.pallas import tpu as pltpu

D_MODEL = 1024
BATCH = 8
SEQ = 2048
DEPTH = 1
N_HEADS = 8
N_KV = 2
GQ = N_HEADS // N_KV
HD = 64
Q_WIDTH = N_HEADS * HD
KV_WIDTH = N_KV * HD
WINDOW = 128
BLK = 128
N_FGROUPS = 4
FG_DIM = 128
F_WIDTH = N_FGROUPS * FG_DIM
N_BUCKETS = 32
MAX_DIST = 128
D_FF = 2816
EPS = 1e-6
NEG = -1e30
O_K = Q_WIDTH
O_V = O_K + KV_WIDTH
O_F = O_V + KV_WIDTH
O_G = O_F + F_WIDTH
IN_WIDTH = O_G + 2 * D_MODEL

F32 = jnp.float32
BF16 = jnp.bfloat16

V7X_LANES = 128
V7X_MXU_DIM = 256
V7X_VMEM_LIMIT_BYTES = 56 * 1024 * 1024

TOKENS = BATCH * SEQ
ROW_TILE = 512
TILES_PER_SEQ = SEQ // ROW_TILE
BLOCKS_PER_TILE = ROW_TILE // BLK
FF_CHUNK = V7X_MXU_DIM
HALF = HD


def _t5_bucket_table():
    rel = (np.arange(3 * BLK)[None, :] - BLK) - np.arange(BLK)[:, None]
    half = N_BUCKETS // 2
    max_exact = half // 2
    ret = (rel > 0).astype(np.int32) * half
    n = np.abs(rel)
    n_safe = np.maximum(n, 1).astype(np.float32)
    large = max_exact + (np.log(n_safe / max_exact) / np.log(MAX_DIST / max_exact)
                         * (half - max_exact)).astype(np.int32)
    large = np.minimum(large, half - 1)
    return (ret + np.where(n < max_exact, n, large)).astype(np.int32)


@functools.lru_cache(maxsize=None)
def _dft_tables():
    def cos_sin(n):
        idx = np.arange(n)
        ang = 2.0 * np.pi * ((idx[:, None] * idx[None, :]) % n).astype(np.float64) / n
        return np.cos(ang) / np.sqrt(n), np.sin(ang) / np.sqrt(n)
    cc, sc = cos_sin(FG_DIM)
    cs, ss = cos_sin(SEQ)
    chan = np.concatenate([cc, sc], axis=1).astype(np.float32)
    pos = np.concatenate([cs, -ss], axis=1).astype(np.float32)
    return chan, pos


def _rms(x, g):
    return x * lax.rsqrt(jnp.mean(x * x, axis=-1, keepdims=True) + EPS) * g


def _sigmoid(z):
    return 1.0 / (1.0 + jnp.exp(-z))


def _dot(a, b):
    return jnp.dot(a, b, preferred_element_type=F32)


def _swiglu(h, wup_ref, wdn_ref, act_ref):
    for c in range(D_FF // FF_CHUNK):
        lo = c * FF_CHUNK
        gate = _dot(h, wup_ref[:, lo:lo + FF_CHUNK])
        up = _dot(h, wup_ref[:, D_FF + lo:D_FF + lo + FF_CHUNK])
        act_ref[:, lo:lo + FF_CHUNK] = (gate * _sigmoid(gate) * up).astype(BF16)
    return _dot(act_ref[...], wdn_ref[...])


def _ffn_proj_kernel(x_ref, g1_ref, wup_ref, wdn_ref, gmix_ref, win_ref, bg_ref,
                     x1_ref, q_ref, kv_ref, f_ref, gate_ref, act_ref):
    x = x_ref[...]
    h = _rms(x, g1_ref[...]).astype(BF16)
    x1 = x + 0.5 * _swiglu(h, wup_ref, wdn_ref, act_ref)
    x1_ref[...] = x1
    h2 = _rms(x1, gmix_ref[...]).astype(BF16)
    q_ref[...] = (_dot(h2, win_ref[:, 0:O_K]) * (HD ** -0.5)).astype(BF16)
    kv_ref[...] = _dot(h2, win_ref[:, O_K:O_F]).astype(BF16)
    f_ref[...] = _dot(h2, win_ref[:, O_F:O_G]).astype(BF16)
    for c in range(2):
        lo = c * D_MODEL
        z = _dot(h2, win_ref[:, O_G + lo:O_G + lo + D_MODEL]) + bg_ref[:, lo:lo + D_MODEL]
        gate_ref[:, lo:lo + D_MODEL] = _sigmoid(z).astype(BF16)


def _ffn_final_kernel(x_ref, g2_ref, wup_ref, wdn_ref, gfin_ref, out_ref, act_ref):
    x = x_ref[...]
    h = _rms(x, g2_ref[...]).astype(BF16)
    x3 = x + 0.5 * _swiglu(h, wup_ref, wdn_ref, act_ref)
    out_ref[...] = _rms(x3, gfin_ref[...])


def _softmax_numerator(s, mask, bias, sink):
    s = jnp.where(mask, s + bias, NEG)
    m = jnp.maximum(jnp.max(s, axis=-1, keepdims=True), sink)
    p = jnp.exp(s - m)
    den = jnp.sum(p, axis=-1, keepdims=True) + jnp.exp(sink - m)
    return p.astype(BF16), den


def _mixer_kernel(relb_ref, sink_ref, x1_ref, q_ref, kv_ref, f_ref, gate_ref, bucket_ref,
                  dftc_ref, dfts_ref, wa_ref, wb_ref, wout_ref, out_ref,
                  bias_ref, kvpad_ref, xcs_ref, o_ref):
    b = pl.program_id(0)
    t = pl.program_id(1)

    @pl.when((b == 0) & (t == 0))
    def _build_bias():
        bucket = bucket_ref[...]
        for h in range(N_HEADS):
            def pick(bk, acc, h=h):
                return jnp.where(bucket == bk, relb_ref[bk, h], acc)
            bias_ref[h] = lax.fori_loop(0, N_BUCKETS, pick, jnp.zeros((BLK, 3 * BLK), F32))

    @pl.when(t == 0)
    def _per_sequence():
        kvpad_ref[0:BLK, :] = jnp.zeros((BLK, 2 * KV_WIDTH), BF16)
        kvpad_ref[BLK:BLK + SEQ, :] = kv_ref[...]
        kvpad_ref[BLK + SEQ:, :] = jnp.zeros((BLK, 2 * KV_WIDTH), BF16)
        for g in range(N_FGROUPS):
            lo = g * FG_DIM
            r = _dot(f_ref[:, lo:lo + FG_DIM], dftc_ref[...])
            xcs_ref[0:SEQ, lo:lo + FG_DIM] = r[:, :FG_DIM].astype(BF16)
            xcs_ref[SEQ:, lo:lo + FG_DIM] = r[:, FG_DIM:].astype(BF16)

    def attend_block(i, carry):
        n = t * BLOCKS_PER_TILE + i
        row0 = pl.multiple_of(i * BLK, BLK)
        key0 = pl.multiple_of(n * BLK, BLK)
        kvw = kvpad_ref[pl.ds(key0, 3 * BLK), :]
        kw = kvw[:, :KV_WIDTH]
        vw = kvw[:, KV_WIDTH:]
        lane = lax.broadcasted_iota(jnp.int32, (3 * BLK, KV_WIDTH), 1)
        low = lane < HALF
        zeros = jnp.zeros_like(kw)
        k_only = (jnp.where(low, kw, zeros), jnp.where(low, zeros, kw))
        v_lo = jnp.where(low, vw, zeros)
        v_hi = jnp.where(low, zeros, vw)
        v_stack = (jnp.concatenate([v_lo, pltpu.roll(v_lo, HALF, 1)], axis=0),
                   jnp.concatenate([pltpu.roll(v_hi, HALF, 1), v_hi], axis=0))
        qi = lax.broadcasted_iota(jnp.int32, (BLK, 3 * BLK), 0)
        kj = lax.broadcasted_iota(jnp.int32, (BLK, 3 * BLK), 1)
        rel = kj - BLK - qi
        key_idx = key0 - BLK + kj
        mask = (jnp.abs(rel) <= WINDOW) & (key_idx >= 0) & (key_idx < SEQ)
        out_low = lax.broadcasted_iota(jnp.int32, (BLK, 2 * HD), 1) < HALF
        for p in range(N_HEADS // 2):
            kh = (2 * p) // GQ
            qp = q_ref[pl.ds(row0, BLK), p * 2 * HD:(p + 1) * 2 * HD]
            lhs = jnp.concatenate([qp, pltpu.roll(qp, HALF, 1)], axis=0)
            s2 = lax.dot_general(lhs, k_only[kh], (((1,), (1,)), ((), ())),
                                 preferred_element_type=F32)
            top, bot = 2 * p + kh, 2 * p + 1 - kh
            p_top, d_top = _softmax_numerator(s2[:BLK], mask, bias_ref[top], sink_ref[top])
            p_bot, d_bot = _softmax_numerator(s2[BLK:], mask, bias_ref[bot], sink_ref[bot])
            if kh == 0:
                p_even, d_even, p_odd, d_odd = p_top, d_top, p_bot, d_bot
            else:
                p_even, d_even, p_odd, d_odd = p_bot, d_bot, p_top, d_top
            pv = _dot(jnp.concatenate([p_even, p_odd], axis=1), v_stack[kh])
            den = jnp.where(out_low, d_even, d_odd)
            o_ref[pl.ds(row0, BLK), p * 2 * HD:(p + 1) * 2 * HD] = (pv / den).astype(BF16)
        return carry

    lax.fori_loop(0, BLOCKS_PER_TILE, attend_block, 0)

    y_a = _dot(o_ref[...], wa_ref[...])
    fmix = _dot(dfts_ref[...], xcs_ref[...]).astype(BF16)
    y_b = _dot(fmix, wb_ref[...])
    mix = (gate_ref[:, :D_MODEL].astype(F32) * y_a + gate_ref[:, D_MODEL:].astype(F32) * y_b)
    out_ref[...] = x1_ref[...] + _dot(mix.astype(BF16), wout_ref[...])


def _resident(shape):
    return pl.BlockSpec(shape, lambda *_: (0,) * len(shape), pipeline_mode=pl.Buffered(1))


def _rows(width):
    return pl.BlockSpec((ROW_TILE, width), lambda i: (i, 0))


def _ffn_proj(x, g1, wup, wdn, gmix, win, bg):
    return pl.pallas_call(
        _ffn_proj_kernel,
        grid=(TOKENS // ROW_TILE,),
        in_specs=[_rows(D_MODEL), _resident((1, D_MODEL)), _resident((D_MODEL, 2 * D_FF)),
                  _resident((D_FF, D_MODEL)), _resident((1, D_MODEL)),
                  _resident((D_MODEL, IN_WIDTH)), _resident((1, 2 * D_MODEL))],
        out_specs=[_rows(D_MODEL), _rows(Q_WIDTH), _rows(2 * KV_WIDTH), _rows(F_WIDTH),
                   _rows(2 * D_MODEL)],
        out_shape=[jax.ShapeDtypeStruct((TOKENS, D_MODEL), F32),
                   jax.ShapeDtypeStruct((TOKENS, Q_WIDTH), BF16),
                   jax.ShapeDtypeStruct((TOKENS, 2 * KV_WIDTH), BF16),
                   jax.ShapeDtypeStruct((TOKENS, F_WIDTH), BF16),
                   jax.ShapeDtypeStruct((TOKENS, 2 * D_MODEL), BF16)],
        scratch_shapes=[pltpu.VMEM((ROW_TILE, D_FF), BF16)],
        compiler_params=pltpu.CompilerParams(dimension_semantics=("arbitrary",),
                                             vmem_limit_bytes=V7X_VMEM_LIMIT_BYTES),
        name="ffn_proj",
    )(x, g1, wup, wdn, gmix, win, bg)


def _ffn_final(x, g2, wup, wdn, gfin):
    return pl.pallas_call(
        _ffn_final_kernel,
        grid=(TOKENS // ROW_TILE,),
        in_specs=[_rows(D_MODEL), _resident((1, D_MODEL)), _resident((D_MODEL, 2 * D_FF)),
                  _resident((D_FF, D_MODEL)), _resident((1, D_MODEL))],
        out_specs=_rows(D_MODEL),
        out_shape=jax.ShapeDtypeStruct((TOKENS, D_MODEL), F32),
        scratch_shapes=[pltpu.VMEM((ROW_TILE, D_FF), BF16)],
        compiler_params=pltpu.CompilerParams(dimension_semantics=("arbitrary",),
                                             vmem_limit_bytes=V7X_VMEM_LIMIT_BYTES),
        name="ffn_final",
    )(x, g2, wup, wdn, gfin)


def _mixer(rel_bias, sink, x1, q, kv, f, gates, bucket, dftc, dfts, wa, wb, wout):
    def tile_rows(width):
        return pl.BlockSpec((ROW_TILE, width), lambda b, t: (b * TILES_PER_SEQ + t, 0))

    def seq_rows(width):
        return pl.BlockSpec((SEQ, width), lambda b, t: (b, 0))

    smem = pl.BlockSpec(memory_space=pltpu.SMEM)
    return pl.pallas_call(
        _mixer_kernel,
        grid=(BATCH, TILES_PER_SEQ),
        in_specs=[smem, smem, tile_rows(D_MODEL), tile_rows(Q_WIDTH), seq_rows(2 * KV_WIDTH),
                  seq_rows(F_WIDTH), tile_rows(2 * D_MODEL), _resident((BLK, 3 * BLK)),
                  _resident((FG_DIM, 2 * FG_DIM)),
                  pl.BlockSpec((ROW_TILE, 2 * SEQ), lambda b, t: (t, 0)),
                  _resident((Q_WIDTH, D_MODEL)), _resident((F_WIDTH, D_MODEL)),
                  _resident((D_MODEL, D_MODEL))],
        out_specs=tile_rows(D_MODEL),
        out_shape=jax.ShapeDtypeStruct((TOKENS, D_MODEL), F32),
        scratch_shapes=[pltpu.VMEM((N_HEADS, BLK, 3 * BLK), F32),
                        pltpu.VMEM((SEQ + 2 * BLK, 2 * KV_WIDTH), BF16),
                        pltpu.VMEM((2 * SEQ, F_WIDTH), BF16),
                        pltpu.VMEM((ROW_TILE, Q_WIDTH), BF16)],
        compiler_params=pltpu.CompilerParams(dimension_semantics=("arbitrary", "arbitrary"),
                                             vmem_limit_bytes=V7X_VMEM_LIMIT_BYTES),
        name="mixer",
    )(rel_bias, sink, x1, q, kv, f, gates, bucket, dftc, dfts, wa, wb, wout)


def kernel(x, g_ffn1, w_up1, w_down1, g_mix, w_in, b_gate, sink, rel_bias, w_branch_a, w_branch_b, w_out, g_ffn2, w_up2, w_down2, g_final):
    assert x.shape == (BATCH, SEQ, D_MODEL) and w_up1.shape[0] == DEPTH == 1
    chan, pos = _dft_tables()
    bucket = jnp.asarray(_t5_bucket_table())
    dftc = jnp.asarray(chan).astype(BF16)
    dfts = jnp.asarray(pos).astype(BF16)
    row = lambda v: v.reshape(1, -1)
    x0 = x.reshape(TOKENS, D_MODEL)
    x1, q, kv, f, gates = _ffn_proj(
        x0, row(g_ffn1[0]), w_up1[0].astype(BF16), w_down1[0].astype(BF16), row(g_mix[0]),
        w_in[0].astype(BF16), row(b_gate[0]))
    x2 = _mixer(rel_bias, sink[0], x1, q, kv, f, gates, bucket, dftc, dfts,
                w_branch_a[0].astype(BF16), w_branch_b[0].astype(BF16), w_out[0].astype(BF16))
    out = _ffn_final(x2, row(g_ffn2[0]), w_up2[0].astype(BF16), w_down2[0].astype(BF16),
                     row(g_final))
    return out.reshape(BATCH, SEQ, D_MODEL)
```

```python
import functools

import numpy as np
import jax
import jax.numpy as jnp
from jax import lax
from jax.experimental import pallas as pl
from jax.experimental.pallas import tpu as pltpu

D_MODEL = 1024
BATCH = 8
SEQ = 2048
DEPTH = 1
N_HEADS = 8
N_KV = 2
GQ = N_HEADS // N_KV
HD = 64
Q_WIDTH = N_HEADS * HD
KV_WIDTH = N_KV * HD
WINDOW = 128
BLK = 128
N_FGROUPS = 4
FG_DIM = 128
F_WIDTH = N_FGROUPS * FG_DIM
N_BUCKETS = 32
MAX_DIST = 128
D_FF = 2816
EPS = 1e-6
NEG = -1e30
O_K = Q_WIDTH
O_V = O_K + KV_WIDTH
O_F = O_V + KV_WIDTH
O_G = O_F + F_WIDTH
IN_WIDTH = O_G + 2 * D_MODEL

F32 = jnp.float32
BF16 = jnp.bfloat16

V7X_LANES = 128
V7X_MXU_DIM = 256
V7X_VMEM_LIMIT_BYTES = 56 * 1024 * 1024

TOKENS = BATCH * SEQ
ROW_TILE = 512
TILES_PER_SEQ = SEQ // ROW_TILE
BLOCKS_PER_TILE = ROW_TILE // BLK
FF_CHUNK = V7X_MXU_DIM
HALF = HD
SOFTMAX_ROWS = 32
LOG2E = float(np.log2(np.e))


def _t5_bucket_table():
    rel = (np.arange(3 * BLK)[None, :] - BLK) - np.arange(BLK)[:, None]
    half = N_BUCKETS // 2
    max_exact = half // 2
    ret = (rel > 0).astype(np.int32) * half
    n = np.abs(rel)
    n_safe = np.maximum(n, 1).astype(np.float32)
    large = max_exact + (np.log(n_safe / max_exact) / np.log(MAX_DIST / max_exact)
                         * (half - max_exact)).astype(np.int32)
    large = np.minimum(large, half - 1)
    return (ret + np.where(n < max_exact, n, large)).astype(np.int32)


@functools.lru_cache(maxsize=None)
def _dft_tables():
    def cos_sin(n):
        idx = np.arange(n)
        ang = 2.0 * np.pi * ((idx[:, None] * idx[None, :]) % n).astype(np.float64) / n
        return np.cos(ang) / np.sqrt(n), np.sin(ang) / np.sqrt(n)
    cc, sc = cos_sin(FG_DIM)
    cs, ss = cos_sin(SEQ)
    chan = np.concatenate([cc, sc], axis=1).astype(np.float32)
    pos = np.concatenate([cs, -ss], axis=1).astype(np.float32)
    return chan, pos


def _rms(x, g):
    return x * lax.rsqrt(jnp.mean(x * x, axis=-1, keepdims=True) + EPS) * g


def _sigmoid(z):
    return 1.0 / (1.0 + jnp.exp(-z))


def _dot(a, b):
    return jnp.dot(a, b, preferred_element_type=F32)


def _swiglu(h, wup_ref, wdn_ref, act_ref):
    for c in range(D_FF // FF_CHUNK):
        lo = c * FF_CHUNK
        gate = _dot(h, wup_ref[:, lo:lo + FF_CHUNK])
        up = _dot(h, wup_ref[:, D_FF + lo:D_FF + lo + FF_CHUNK])
        act_ref[:, lo:lo + FF_CHUNK] = (gate * _sigmoid(gate) * up).astype(BF16)
    return _dot(act_ref[...], wdn_ref[...])


def _ffn_proj_kernel(x_ref, g1_ref, wup_ref, wdn_ref, gmix_ref, win_ref, bg_ref,
                     x1_ref, q_ref, kv_ref, f_ref, gate_ref, act_ref):
    x = x_ref[...]
    h = _rms(x, g1_ref[...]).astype(BF16)
    x1 = x + 0.5 * _swiglu(h, wup_ref, wdn_ref, act_ref)
    x1_ref[...] = x1
    h2 = _rms(x1, gmix_ref[...]).astype(BF16)
    q_ref[...] = (_dot(h2, win_ref[:, 0:O_K]) * (HD ** -0.5 * LOG2E)).astype(BF16)
    kv_ref[...] = _dot(h2, win_ref[:, O_K:O_F]).astype(BF16)
    f_ref[...] = _dot(h2, win_ref[:, O_F:O_G]).astype(BF16)
    for c in range(2):
        lo = c * D_MODEL
        z = _dot(h2, win_ref[:, O_G + lo:O_G + lo + D_MODEL]) + bg_ref[:, lo:lo + D_MODEL]
        gate_ref[:, lo:lo + D_MODEL] = _sigmoid(z).astype(BF16)


def _ffn_final_kernel(x_ref, g2_ref, wup_ref, wdn_ref, gfin_ref, out_ref, act_ref):
    x = x_ref[...]
    h = _rms(x, g2_ref[...]).astype(BF16)
    x3 = x + 0.5 * _swiglu(h, wup_ref, wdn_ref, act_ref)
    out_ref[...] = _rms(x3, gfin_ref[...])


def _mixer_kernel(relb_ref, sink_ref, x1_ref, q_ref, kv_ref, f_ref, gate_ref, bucket_ref,
                  dftc_ref, dfts_ref, wa_ref, wb_ref, wout_ref, out_ref,
                  bias_ref, kvpad_ref, xcs_ref, o_ref, s_ref, p_ref, e_ref):
    b = pl.program_id(0)
    t = pl.program_id(1)

    @pl.when((b == 0) & (t == 0))
    def _build_bias():
        bucket = bucket_ref[...]
        qi = lax.broadcasted_iota(jnp.int32, (BLK, 3 * BLK), 0)
        kj = lax.broadcasted_iota(jnp.int32, (BLK, 3 * BLK), 1)
        in_window = jnp.abs(kj - BLK - qi) <= WINDOW
        valid = (in_window, in_window & (kj >= BLK), in_window & (kj < 2 * BLK))
        for h in range(N_HEADS):
            def pick(bk, acc, h=h):
                return jnp.where(bucket == bk, relb_ref[bk, h] * LOG2E, acc)
            base = lax.fori_loop(0, N_BUCKETS, pick, jnp.zeros((BLK, 3 * BLK), F32))
            for v in range(3):
                bias_ref[v, h] = jnp.where(valid[v], base, -jnp.inf)

    @pl.when(t == 0)
    def _per_sequence():
        kvpad_ref[0:BLK, :] = jnp.zeros((BLK, 2 * KV_WIDTH), BF16)
        kvpad_ref[BLK:BLK + SEQ, :] = kv_ref[...]
        kvpad_ref[BLK + SEQ:, :] = jnp.zeros((BLK, 2 * KV_WIDTH), BF16)
        for g in range(N_FGROUPS):
            lo = g * FG_DIM
            r = _dot(f_ref[:, lo:lo + FG_DIM], dftc_ref[...])
            xcs_ref[0:SEQ, lo:lo + FG_DIM] = r[:, :FG_DIM].astype(BF16)
            xcs_ref[SEQ:, lo:lo + FG_DIM] = r[:, FG_DIM:].astype(BF16)

    def attend_block(i):
        n = t * BLOCKS_PER_TILE + i
        row0 = i * BLK
        key0 = pl.multiple_of(n * BLK, BLK)
        variant = jnp.where(n == 0, 1, jnp.where(n == SEQ // BLK - 1, 2, 0))
        kvw = kvpad_ref[pl.ds(key0, 3 * BLK), :]
        kw = kvw[:, :KV_WIDTH]
        vw = kvw[:, KV_WIDTH:]
        low = lax.broadcasted_iota(jnp.int32, (3 * BLK, KV_WIDTH), 1) < HALF
        zeros = jnp.zeros_like(kw)
        k_only = (jnp.where(low, kw, zeros), jnp.where(low, zeros, kw))
        v_lo = jnp.where(low, vw, zeros)
        v_hi = jnp.where(low, zeros, vw)
        ones_low = jnp.where(low, 1.0, 0.0).astype(BF16)
        row_sums = jnp.concatenate([ones_low, 1.0 - ones_low], axis=0)
        v_stack = (
            jnp.concatenate([jnp.concatenate([v_lo, pltpu.roll(v_lo, HALF, 1)], axis=0), row_sums], axis=1),
            jnp.concatenate([jnp.concatenate([pltpu.roll(v_hi, HALF, 1), v_hi], axis=0), row_sums], axis=1))
        for p in range(N_HEADS // 2):
            kh = (2 * p) // GQ
            qp = q_ref[row0:row0 + BLK, p * 2 * HD:(p + 1) * 2 * HD]
            lhs = jnp.concatenate([qp, pltpu.roll(qp, HALF, 1)], axis=0)
            s_ref[p] = lax.dot_general(lhs, k_only[kh], (((1,), (1,)), ((), ())),
                                       preferred_element_type=F32)
            for half, h in ((0, 2 * p + kh), (1, 2 * p + 1 - kh)):
                odd = h % 2
                sink = sink_ref[h] * LOG2E
                for r in range(0, BLK, SOFTMAX_ROWS):
                    s = (s_ref[p, half * BLK + r:half * BLK + r + SOFTMAX_ROWS, :]
                         + bias_ref[variant, h, r:r + SOFTMAX_ROWS, :])
                    m = jnp.maximum(jnp.max(s, axis=-1, keepdims=True), sink)
                    p_ref[p, r:r + SOFTMAX_ROWS, odd * 3 * BLK:(odd + 1) * 3 * BLK] = (
                        jnp.exp2(s - m).astype(BF16))
                    e_ref[p, r:r + SOFTMAX_ROWS, odd * HALF:(odd + 1) * HALF] = jnp.broadcast_to(
                        jnp.exp2(sink - m), (SOFTMAX_ROWS, HALF))
            pv = _dot(p_ref[p], v_stack[kh])
            den = pv[:, 2 * HD:] + e_ref[p]
            o_ref[row0:row0 + BLK, p * 2 * HD:(p + 1) * 2 * HD] = (pv[:, :2 * HD] / den).astype(BF16)

    for i in range(BLOCKS_PER_TILE):
        attend_block(i)

    y_a = _dot(o_ref[...], wa_ref[...])
    fmix = _dot(dfts_ref[...], xcs_ref[...]).astype(BF16)
    y_b = _dot(fmix, wb_ref[...])
    mix = (gate_ref[:, :D_MODEL].astype(F32) * y_a + gate_ref[:, D_MODEL:].astype(F32) * y_b)
    out_ref[...] = x1_ref[...] + _dot(mix.astype(BF16), wout_ref[...])


def _resident(shape):
    return pl.BlockSpec(shape, lambda *_: (0,) * len(shape), pipeline_mode=pl.Buffered(1))


def _rows(width):
    return pl.BlockSpec((ROW_TILE, width), lambda i: (i, 0))


def _ffn_proj(x, g1, wup, wdn, gmix, win, bg):
    return pl.pallas_call(
        _ffn_proj_kernel,
        grid=(TOKENS // ROW_TILE,),
        in_specs=[_rows(D_MODEL), _resident((1, D_MODEL)), _resident((D_MODEL, 2 * D_FF)),
                  _resident((D_FF, D_MODEL)), _resident((1, D_MODEL)),
                  _resident((D_MODEL, IN_WIDTH)), _resident((1, 2 * D_MODEL))],
        out_specs=[_rows(D_MODEL), _rows(Q_WIDTH), _rows(2 * KV_WIDTH), _rows(F_WIDTH),
                   _rows(2 * D_MODEL)],
        out_shape=[jax.ShapeDtypeStruct((TOKENS, D_MODEL), F32),
                   jax.ShapeDtypeStruct((TOKENS, Q_WIDTH), BF16),
                   jax.ShapeDtypeStruct((TOKENS, 2 * KV_WIDTH), BF16),
                   jax.ShapeDtypeStruct((TOKENS, F_WIDTH), BF16),
                   jax.ShapeDtypeStruct((TOKENS, 2 * D_MODEL), BF16)],
        scratch_shapes=[pltpu.VMEM((ROW_TILE, D_FF), BF16)],
        compiler_params=pltpu.CompilerParams(dimension_semantics=("arbitrary",),
                                             vmem_limit_bytes=V7X_VMEM_LIMIT_BYTES),
        name="ffn_proj",
    )(x, g1, wup, wdn, gmix, win, bg)


def _ffn_final(x, g2, wup, wdn, gfin):
    return pl.pallas_call(
        _ffn_final_kernel,
        grid=(TOKENS // ROW_TILE,),
        in_specs=[_rows(D_MODEL), _resident((1, D_MODEL)), _resident((D_MODEL, 2 * D_FF)),
                  _resident((D_FF, D_MODEL)), _resident((1, D_MODEL))],
        out_specs=_rows(D_MODEL),
        out_shape=jax.ShapeDtypeStruct((TOKENS, D_MODEL), F32),
        scratch_shapes=[pltpu.VMEM((ROW_TILE, D_FF), BF16)],
        compiler_params=pltpu.CompilerParams(dimension_semantics=("arbitrary",),
                                             vmem_limit_bytes=V7X_VMEM_LIMIT_BYTES),
        name="ffn_final",
    )(x, g2, wup, wdn, gfin)


def _mixer(rel_bias, sink, x1, q, kv, f, gates, bucket, dftc, dfts, wa, wb, wout):
    def tile_rows(width):
        return pl.BlockSpec((ROW_TILE, width), lambda b, t: (b * TILES_PER_SEQ + t, 0))

    def seq_rows(width):
        return pl.BlockSpec((SEQ, width), lambda b, t: (b, 0))

    smem = pl.BlockSpec(memory_space=pltpu.SMEM)
    return pl.pallas_call(
        _mixer_kernel,
        grid=(BATCH, TILES_PER_SEQ),
        in_specs=[smem, smem, tile_rows(D_MODEL), tile_rows(Q_WIDTH), seq_rows(2 * KV_WIDTH),
                  seq_rows(F_WIDTH), tile_rows(2 * D_MODEL), _resident((BLK, 3 * BLK)),
                  _resident((FG_DIM, 2 * FG_DIM)),
                  pl.BlockSpec((ROW_TILE, 2 * SEQ), lambda b, t: (t, 0)),
                  _resident((Q_WIDTH, D_MODEL)), _resident((F_WIDTH, D_MODEL)),
                  _resident((D_MODEL, D_MODEL))],
        out_specs=tile_rows(D_MODEL),
        out_shape=jax.ShapeDtypeStruct((TOKENS, D_MODEL), F32),
        scratch_shapes=[pltpu.VMEM((3, N_HEADS, BLK, 3 * BLK), F32),
                        pltpu.VMEM((SEQ + 2 * BLK, 2 * KV_WIDTH), BF16),
                        pltpu.VMEM((2 * SEQ, F_WIDTH), BF16),
                        pltpu.VMEM((ROW_TILE, Q_WIDTH), BF16),
                        pltpu.VMEM((N_HEADS // 2, 2 * BLK, 3 * BLK), F32),
                        pltpu.VMEM((N_HEADS // 2, BLK, 6 * BLK), BF16),
                        pltpu.VMEM((N_HEADS // 2, BLK, 2 * HD), F32)],
        compiler_params=pltpu.CompilerParams(dimension_semantics=("arbitrary", "arbitrary"),
                                             vmem_limit_bytes=V7X_VMEM_LIMIT_BYTES),
        name="mixer",
    )(rel_bias, sink, x1, q, kv, f, gates, bucket, dftc, dfts, wa, wb, wout)


def kernel(x, g_ffn1, w_up1, w_down1, g_mix, w_in, b_gate, sink, rel_bias, w_branch_a, w_branch_b, w_out, g_ffn2, w_up2, w_down2, g_final):
    assert x.shape == (BATCH, SEQ, D_MODEL) and w_up1.shape[0] == DEPTH == 1
    chan, pos = _dft_tables()
    bucket = jnp.asarray(_t5_bucket_table())
    dftc = jnp.asarray(chan).astype(BF16)
    dfts = jnp.asarray(pos).astype(BF16)
    row = lambda v: v.reshape(1, -1)
    x0 = x.reshape(TOKENS, D_MODEL)
    x1, q, kv, f, gates = _ffn_proj(
        x0, row(g_ffn1[0]), w_up1[0].astype(BF16), w_down1[0].astype(BF16), row(g_mix[0]),
        w_in[0].astype(BF16), row(b_gate[0]))
    x2 = _mixer(rel_bias, sink[0], x1, q, kv, f, gates, bucket, dftc, dfts,
                w_branch_a[0].astype(BF16), w_branch_b[0].astype(BF16), w_out[0].astype(BF16))
    out = _ffn_final(x2, row(g_ffn2[0]), w_up2[0].astype(BF16), w_down2[0].astype(BF16),
                     row(g_final))
    return out.reshape(BATCH, SEQ, D_MODEL)
```

```python
import functools

import numpy as np
import jax
import jax.numpy as jnp
from jax import lax
from jax.experimental import pallas as pl
from jax.experimental.pallas import tpu as pltpu

D_MODEL = 1024
BATCH = 8
SEQ = 2048
DEPTH = 1
N_HEADS = 8
N_KV = 2
GQ = N_HEADS // N_KV
HD = 64
Q_WIDTH = N_HEADS * HD
KV_WIDTH = N_KV * HD
WINDOW = 128
BLK = 128
N_FGROUPS = 4
FG_DIM = 128
F_WIDTH = N_FGROUPS * FG_DIM
N_BUCKETS = 32
MAX_DIST = 128
D_FF = 2816
EPS = 1e-6
O_K = Q_WIDTH
O_V = O_K + KV_WIDTH
O_F = O_V + KV_WIDTH
O_G = O_F + F_WIDTH
IN_WIDTH = O_G + 2 * D_MODEL

F32 = jnp.float32
BF16 = jnp.bfloat16

V7X_MXU_DIM = 256
V7X_BF16_SUBLANES = 16
V7X_VMEM_LIMIT_BYTES = 56 * 1024 * 1024

TOKENS = BATCH * SEQ
ROW_TILE = 512
TILES_PER_SEQ = SEQ // ROW_TILE
BLOCKS_PER_TILE = ROW_TILE // BLK
FF_CHUNK = V7X_MXU_DIM
VT_ROWS = HD + V7X_BF16_SUBLANES
KEY_CHUNK = 64
LOG2E = float(np.log2(np.e))


def _t5_bucket_table():
    rel = (np.arange(3 * BLK)[None, :] - BLK) - np.arange(BLK)[:, None]
    half = N_BUCKETS // 2
    max_exact = half // 2
    ret = (rel > 0).astype(np.int32) * half
    n = np.abs(rel)
    n_safe = np.maximum(n, 1).astype(np.float32)
    large = max_exact + (np.log(n_safe / max_exact) / np.log(MAX_DIST / max_exact)
                         * (half - max_exact)).astype(np.int32)
    large = np.minimum(large, half - 1)
    return (ret + np.where(n < max_exact, n, large)).astype(np.int32)


@functools.lru_cache(maxsize=None)
def _dft_tables():
    def cos_sin(n):
        idx = np.arange(n)
        ang = 2.0 * np.pi * ((idx[:, None] * idx[None, :]) % n).astype(np.float64) / n
        return np.cos(ang) / np.sqrt(n), np.sin(ang) / np.sqrt(n)
    cc, sc = cos_sin(FG_DIM)
    cs, ss = cos_sin(SEQ)
    chan = np.concatenate([cc, sc], axis=1).astype(np.float32)
    pos = np.concatenate([cs, -ss], axis=1).astype(np.float32)
    return chan, pos


def _rms(x, g):
    return x * lax.rsqrt(jnp.mean(x * x, axis=-1, keepdims=True) + EPS) * g


def _sigmoid(z):
    return 1.0 / (1.0 + jnp.exp(-z))


def _dot(a, b):
    return jnp.dot(a, b, preferred_element_type=F32)


def _swiglu(h, wup_ref, wdn_ref, act_ref):
    for c in range(D_FF // FF_CHUNK):
        lo = c * FF_CHUNK
        gate = _dot(h, wup_ref[:, lo:lo + FF_CHUNK])
        up = _dot(h, wup_ref[:, D_FF + lo:D_FF + lo + FF_CHUNK])
        act_ref[:, lo:lo + FF_CHUNK] = (gate * _sigmoid(gate) * up).astype(BF16)
    return _dot(act_ref[...], wdn_ref[...])


def _ffn_proj_kernel(x_ref, g1_ref, wup_ref, wdn_ref, gmix_ref, win_ref, bg_ref,
                     x1_ref, q_ref, kv_ref, f_ref, gate_ref, act_ref):
    x = x_ref[...]
    h = _rms(x, g1_ref[...]).astype(BF16)
    x1 = x + 0.5 * _swiglu(h, wup_ref, wdn_ref, act_ref)
    x1_ref[...] = x1
    h2 = _rms(x1, gmix_ref[...]).astype(BF16)
    q_ref[...] = (_dot(h2, win_ref[:, 0:O_K]) * (HD ** -0.5 * LOG2E)).astype(BF16)
    kv_ref[...] = _dot(h2, win_ref[:, O_K:O_F]).astype(BF16)
    f_ref[...] = _dot(h2, win_ref[:, O_F:O_G]).astype(BF16)
    for c in range(2):
        lo = c * D_MODEL
        z = _dot(h2, win_ref[:, O_G + lo:O_G + lo + D_MODEL]) + bg_ref[:, lo:lo + D_MODEL]
        gate_ref[:, lo:lo + D_MODEL] = _sigmoid(z).astype(BF16)


def _ffn_final_kernel(x_ref, g2_ref, wup_ref, wdn_ref, gfin_ref, out_ref, act_ref):
    x = x_ref[...]
    h = _rms(x, g2_ref[...]).astype(BF16)
    x3 = x + 0.5 * _swiglu(h, wup_ref, wdn_ref, act_ref)
    out_ref[...] = _rms(x3, gfin_ref[...])


def _mixer_kernel(relb_ref, sink_ref, x1_ref, q_ref, kv_ref, f_ref, gate_ref, bucket_ref,
                  dftc_ref, dfts_ref, wa_ref, wb_ref, wout_ref, out_ref,
                  bias_ref, kpad_ref, vt_ref, xcs_ref, o_ref, s_ref, p_ref):
    b = pl.program_id(0)
    t = pl.program_id(1)

    @pl.when((b == 0) & (t == 0))
    def _build_bias():
        bucket = bucket_ref[...]
        kj = lax.broadcasted_iota(jnp.int32, (3 * BLK, BLK), 0)
        qi = lax.broadcasted_iota(jnp.int32, (3 * BLK, BLK), 1)
        in_window = jnp.abs(kj - BLK - qi) <= WINDOW
        valid = (in_window, in_window & (kj >= BLK), in_window & (kj < 2 * BLK))
        for h in range(N_HEADS):
            def pick(bk, acc, h=h):
                return jnp.where(bucket == bk, relb_ref[bk, h] * LOG2E, acc)
            base = lax.fori_loop(0, N_BUCKETS, pick, jnp.zeros((3 * BLK, BLK), F32))
            for v in range(3):
                bias_ref[v, h] = jnp.where(valid[v], base, -jnp.inf)

    @pl.when(t == 0)
    def _per_sequence():
        kpad_ref[0:BLK, :] = jnp.zeros((BLK, KV_WIDTH), BF16)
        kpad_ref[BLK:BLK + SEQ, :] = kv_ref[:, :KV_WIDTH]
        kpad_ref[BLK + SEQ:, :] = jnp.zeros((BLK, KV_WIDTH), BF16)
        vt_ref[:, 0:BLK] = jnp.zeros((N_KV * VT_ROWS, BLK), BF16)
        vt_ref[:, BLK + SEQ:] = jnp.zeros((N_KV * VT_ROWS, BLK), BF16)
        for c in range(SEQ // BLK):
            v_t = kv_ref[c * BLK:(c + 1) * BLK, KV_WIDTH:].T
            for kh in range(N_KV):
                vt_ref[kh * VT_ROWS:kh * VT_ROWS + HD, (c + 1) * BLK:(c + 2) * BLK] = (
                    v_t[kh * HD:(kh + 1) * HD])
        for kh in range(N_KV):
            vt_ref[kh * VT_ROWS + HD:(kh + 1) * VT_ROWS, :] = jnp.ones(
                (VT_ROWS - HD, SEQ + 2 * BLK), BF16)
        for g in range(N_FGROUPS):
            lo = g * FG_DIM
            r = _dot(f_ref[:, lo:lo + FG_DIM], dftc_ref[...])
            xcs_ref[0:SEQ, lo:lo + FG_DIM] = r[:, :FG_DIM].astype(BF16)
            xcs_ref[SEQ:, lo:lo + FG_DIM] = r[:, FG_DIM:].astype(BF16)

    def block_keys(i):
        n = t * BLOCKS_PER_TILE + i
        variant = jnp.where(n == 0, 1, jnp.where(n == SEQ // BLK - 1, 2, 0))
        return pl.multiple_of(n * BLK, BLK), variant

    def scores(i):
        key0, _ = block_keys(i)
        q_t = q_ref[i * BLK:(i + 1) * BLK, :].T
        zero = jnp.zeros((HD, BLK), BF16)
        cols = []
        for h in range(N_HEADS):
            q_h = q_t[h * HD:(h + 1) * HD]
            cols.append(jnp.concatenate([q_h, zero] if h < GQ else [zero, q_h], axis=0))
        s_ref[i % 2] = _dot(kpad_ref[pl.ds(key0, 3 * BLK), :], jnp.concatenate(cols, axis=1))

    def softmax(i):
        _, variant = block_keys(i)
        buf = i % 2
        sink_terms = []
        for h in range(N_HEADS):
            sink = sink_ref[h] * LOG2E
            cols = slice(h * BLK, (h + 1) * BLK)
            top = s_ref[buf, 0:KEY_CHUNK, cols] + bias_ref[variant, h, 0:KEY_CHUNK, :]
            for k0 in range(KEY_CHUNK, 3 * BLK, KEY_CHUNK):
                top = jnp.maximum(top, s_ref[buf, k0:k0 + KEY_CHUNK, cols]
                                  + bias_ref[variant, h, k0:k0 + KEY_CHUNK, :])
            m = jnp.maximum(jnp.max(top, axis=0, keepdims=True), sink)
            for k0 in range(0, 3 * BLK, KEY_CHUNK):
                z = (s_ref[buf, k0:k0 + KEY_CHUNK, cols] - m) + bias_ref[variant, h, k0:k0 + KEY_CHUNK, :]
                p_ref[buf, k0:k0 + KEY_CHUNK, cols] = jnp.exp2(z).astype(BF16)
            sink_terms.append(jnp.exp2(sink - m))
        return sink_terms

    def weighted_values(i, sink_terms):
        key0, _ = block_keys(i)
        for kh in range(N_KV):
            o_t = _dot(vt_ref[kh * VT_ROWS:(kh + 1) * VT_ROWS, pl.ds(key0, 3 * BLK)],
                       p_ref[i % 2, :, kh * GQ * BLK:(kh + 1) * GQ * BLK])
            den = o_t[HD:HD + 1] + jnp.concatenate(sink_terms[kh * GQ:(kh + 1) * GQ], axis=1)
            o_n = o_t[:HD] / den
            for pr in range(GQ // 2):
                pair_t = jnp.concatenate([o_n[:, 2 * pr * BLK:(2 * pr + 1) * BLK],
                                          o_n[:, (2 * pr + 1) * BLK:(2 * pr + 2) * BLK]], axis=0)
                lane0 = (kh * GQ + 2 * pr) * HD
                o_ref[i * BLK:(i + 1) * BLK, lane0:lane0 + 2 * HD] = pair_t.T.astype(BF16)

    scores(0)
    for i in range(BLOCKS_PER_TILE):
        if i + 1 < BLOCKS_PER_TILE:
            scores(i + 1)
        weighted_values(i, softmax(i))

    y_a = _dot(o_ref[...], wa_ref[...])
    fmix = _dot(dfts_ref[...], xcs_ref[...]).astype(BF16)
    y_b = _dot(fmix, wb_ref[...])
    mix = (gate_ref[:, :D_MODEL].astype(F32) * y_a + gate_ref[:, D_MODEL:].astype(F32) * y_b)
    out_ref[...] = x1_ref[...] + _dot(mix.astype(BF16), wout_ref[...])


def _resident(shape):
    return pl.BlockSpec(shape, lambda *_: (0,) * len(shape), pipeline_mode=pl.Buffered(1))


def _rows(width):
    return pl.BlockSpec((ROW_TILE, width), lambda i: (i, 0))


def _ffn_proj(x, g1, wup, wdn, gmix, win, bg):
    return pl.pallas_call(
        _ffn_proj_kernel,
        grid=(TOKENS // ROW_TILE,),
        in_specs=[_rows(D_MODEL), _resident((1, D_MODEL)), _resident((D_MODEL, 2 * D_FF)),
                  _resident((D_FF, D_MODEL)), _resident((1, D_MODEL)),
                  _resident((D_MODEL, IN_WIDTH)), _resident((1, 2 * D_MODEL))],
        out_specs=[_rows(D_MODEL), _rows(Q_WIDTH), _rows(2 * KV_WIDTH), _rows(F_WIDTH),
                   _rows(2 * D_MODEL)],
        out_shape=[jax.ShapeDtypeStruct((TOKENS, D_MODEL), F32),
                   jax.ShapeDtypeStruct((TOKENS, Q_WIDTH), BF16),
                   jax.ShapeDtypeStruct((TOKENS, 2 * KV_WIDTH), BF16),
                   jax.ShapeDtypeStruct((TOKENS, F_WIDTH), BF16),
                   jax.ShapeDtypeStruct((TOKENS, 2 * D_MODEL), BF16)],
        scratch_shapes=[pltpu.VMEM((ROW_TILE, D_FF), BF16)],
        compiler_params=pltpu.CompilerParams(dimension_semantics=("arbitrary",),
                                             vmem_limit_bytes=V7X_VMEM_LIMIT_BYTES),
        name="ffn_proj",
    )(x, g1, wup, wdn, gmix, win, bg)


def _ffn_final(x, g2, wup, wdn, gfin):
    return pl.pallas_call(
        _ffn_final_kernel,
        grid=(TOKENS // ROW_TILE,),
        in_specs=[_rows(D_MODEL), _resident((1, D_MODEL)), _resident((D_MODEL, 2 * D_FF)),
                  _resident((D_FF, D_MODEL)), _resident((1, D_MODEL))],
        out_specs=_rows(D_MODEL),
        out_shape=jax.ShapeDtypeStruct((TOKENS, D_MODEL), F32),
        scratch_shapes=[pltpu.VMEM((ROW_TILE, D_FF), BF16)],
        compiler_params=pltpu.CompilerParams(dimension_semantics=("arbitrary",),
                                             vmem_limit_bytes=V7X_VMEM_LIMIT_BYTES),
        name="ffn_final",
    )(x, g2, wup, wdn, gfin)


def _mixer(rel_bias, sink, x1, q, kv, f, gates, bucket, dftc, dfts, wa, wb, wout):
    def tile_rows(width):
        return pl.BlockSpec((ROW_TILE, width), lambda b, t: (b * TILES_PER_SEQ + t, 0))

    def seq_rows(width):
        return pl.BlockSpec((SEQ, width), lambda b, t: (b, 0))

    smem = pl.BlockSpec(memory_space=pltpu.SMEM)
    return pl.pallas_call(
        _mixer_kernel,
        grid=(BATCH, TILES_PER_SEQ),
        in_specs=[smem, smem, tile_rows(D_MODEL), tile_rows(Q_WIDTH), seq_rows(2 * KV_WIDTH),
                  seq_rows(F_WIDTH), tile_rows(2 * D_MODEL), _resident((3 * BLK, BLK)),
                  _resident((FG_DIM, 2 * FG_DIM)),
                  pl.BlockSpec((ROW_TILE, 2 * SEQ), lambda b, t: (t, 0)),
                  _resident((Q_WIDTH, D_MODEL)), _resident((F_WIDTH, D_MODEL)),
                  _resident((D_MODEL, D_MODEL))],
        out_specs=tile_rows(D_MODEL),
        out_shape=jax.ShapeDtypeStruct((TOKENS, D_MODEL), F32),
        scratch_shapes=[pltpu.VMEM((3, N_HEADS, 3 * BLK, BLK), F32),
                        pltpu.VMEM((SEQ + 2 * BLK, KV_WIDTH), BF16),
                        pltpu.VMEM((N_KV * VT_ROWS, SEQ + 2 * BLK), BF16),
                        pltpu.VMEM((2 * SEQ, F_WIDTH), BF16),
                        pltpu.VMEM((ROW_TILE, Q_WIDTH), BF16),
                        pltpu.VMEM((2, 3 * BLK, N_HEADS * BLK), F32),
                        pltpu.VMEM((2, 3 * BLK, N_HEADS * BLK), BF16)],
        compiler_params=pltpu.CompilerParams(dimension_semantics=("arbitrary", "arbitrary"),
                                             vmem_limit_bytes=V7X_VMEM_LIMIT_BYTES),
        name="mixer",
    )(rel_bias, sink, x1, q, kv, f, gates, bucket, dftc, dfts, wa, wb, wout)


def kernel(x, g_ffn1, w_up1, w_down1, g_mix, w_in, b_gate, sink, rel_bias, w_branch_a, w_branch_b, w_out, g_ffn2, w_up2, w_down2, g_final):
    assert x.shape == (BATCH, SEQ, D_MODEL) and w_up1.shape[0] == DEPTH == 1
    chan, pos = _dft_tables()
    bucket = jnp.asarray(_t5_bucket_table().T)
    dftc = jnp.asarray(chan).astype(BF16)
    dfts = jnp.asarray(pos).astype(BF16)
    row = lambda v: v.reshape(1, -1)
    x0 = x.reshape(TOKENS, D_MODEL)
    x1, q, kv, f, gates = _ffn_proj(
        x0, row(g_ffn1[0]), w_up1[0].astype(BF16), w_down1[0].astype(BF16), row(g_mix[0]),
        w_in[0].astype(BF16), row(b_gate[0]))
    x2 = _mixer(rel_bias, sink[0], x1, q, kv, f, gates, bucket, dftc, dfts,
                w_branch_a[0].astype(BF16), w_branch_b[0].astype(BF16), w_out[0].astype(BF16))
    out = _ffn_final(x2, row(g_ffn2[0]), w_up2[0].astype(BF16), w_down2[0].astype(BF16),
                     row(g_final))
    return out.reshape(BATCH, SEQ, D_MODEL)
```

```python
import functools

import numpy as np
import jax
import jax.numpy as jnp
from jax import lax
from jax.experimental import pallas as pl
from jax.experimental.pallas import tpu as pltpu

D_MODEL = 1024
BATCH = 8
SEQ = 2048
DEPTH = 1
N_HEADS = 8
N_KV = 2
GQ = N_HEADS // N_KV
HD = 64
Q_WIDTH = N_HEADS * HD
KV_WIDTH = N_KV * HD
WINDOW = 128
BLK = 128
N_FGROUPS = 4
FG_DIM = 128
F_WIDTH = N_FGROUPS * FG_DIM
N_BUCKETS = 32
MAX_DIST = 128
D_FF = 2816
EPS = 1e-6
O_K = Q_WIDTH
O_V = O_K + KV_WIDTH
O_F = O_V + KV_WIDTH
O_G = O_F + F_WIDTH
IN_WIDTH = O_G + 2 * D_MODEL

F32 = jnp.float32
BF16 = jnp.bfloat16

V7X_MXU_DIM = 256
V7X_BF16_SUBLANES = 16
V7X_VMEM_LIMIT_BYTES = 56 * 1024 * 1024

TOKENS = BATCH * SEQ
HALF_SEQ = SEQ // 2
ROW_TILE = 512
TILES_PER_SEQ = SEQ // ROW_TILE
BLOCKS_PER_TILE = ROW_TILE // BLK
FF_CHUNK = V7X_MXU_DIM
VT_ROWS = HD + V7X_BF16_SUBLANES
KEY_CHUNK = 64
LOG2E = float(np.log2(np.e))


def _t5_bucket_table():
    rel = (np.arange(3 * BLK)[None, :] - BLK) - np.arange(BLK)[:, None]
    half = N_BUCKETS // 2
    max_exact = half // 2
    ret = (rel > 0).astype(np.int32) * half
    n = np.abs(rel)
    n_safe = np.maximum(n, 1).astype(np.float32)
    large = max_exact + (np.log(n_safe / max_exact) / np.log(MAX_DIST / max_exact)
                         * (half - max_exact)).astype(np.int32)
    large = np.minimum(large, half - 1)
    return (ret + np.where(n < max_exact, n, large)).astype(np.int32)


@functools.lru_cache(maxsize=None)
def _dft_tables():
    def cos_sin(n):
        idx = np.arange(n)
        ang = 2.0 * np.pi * ((idx[:, None] * idx[None, :]) % n).astype(np.float64) / n
        return np.cos(ang) / np.sqrt(n), np.sin(ang) / np.sqrt(n)
    cc, sc = cos_sin(FG_DIM)
    cs, ss = cos_sin(SEQ)
    chan = np.block([[cc, sc], [cc, -sc]]).astype(np.float32)
    neg_sin = -ss[:, :HALF_SEQ]
    neg_sin[:, 0] = np.where(np.arange(SEQ) % 2 == 0, 1.0, -1.0) / np.sqrt(SEQ)
    pos = np.concatenate([cs[:, :HALF_SEQ], neg_sin], axis=1).astype(np.float32)
    return chan, pos


def _rms(x, g):
    return x * lax.rsqrt(jnp.mean(x * x, axis=-1, keepdims=True) + EPS) * g


def _sigmoid(z):
    return 1.0 / (1.0 + jnp.exp(-z))


def _dot(a, b):
    return jnp.dot(a, b, preferred_element_type=F32)


def _swiglu(h, wup_ref, wdn_ref, act_ref):
    for c in range(D_FF // FF_CHUNK):
        lo = c * FF_CHUNK
        gate = _dot(h, wup_ref[:, lo:lo + FF_CHUNK])
        up = _dot(h, wup_ref[:, D_FF + lo:D_FF + lo + FF_CHUNK])
        act_ref[:, lo:lo + FF_CHUNK] = (gate * _sigmoid(gate) * up).astype(BF16)
    return _dot(act_ref[...], wdn_ref[...])


def _ffn_proj_kernel(x_ref, g1_ref, wup_ref, wdn_ref, gmix_ref, win_ref, bg_ref,
                     x1_ref, q_ref, kv_ref, f_ref, gate_ref, act_ref):
    x = x_ref[...]
    h = _rms(x, g1_ref[...]).astype(BF16)
    x1 = x + 0.5 * _swiglu(h, wup_ref, wdn_ref, act_ref)
    x1_ref[...] = x1
    h2 = _rms(x1, gmix_ref[...]).astype(BF16)
    q_ref[...] = (_dot(h2, win_ref[:, 0:O_K]) * (HD ** -0.5 * LOG2E)).astype(BF16)
    kv_ref[...] = _dot(h2, win_ref[:, O_K:O_F]).astype(BF16)
    f_ref[...] = _dot(h2, win_ref[:, O_F:O_G]).astype(BF16)
    for c in range(2):
        lo = c * D_MODEL
        z = _dot(h2, win_ref[:, O_G + lo:O_G + lo + D_MODEL]) + bg_ref[:, lo:lo + D_MODEL]
        gate_ref[:, lo:lo + D_MODEL] = _sigmoid(z).astype(BF16)


def _ffn_final_kernel(x_ref, g2_ref, wup_ref, wdn_ref, gfin_ref, out_ref, act_ref):
    x = x_ref[...]
    h = _rms(x, g2_ref[...]).astype(BF16)
    x3 = x + 0.5 * _swiglu(h, wup_ref, wdn_ref, act_ref)
    out_ref[...] = _rms(x3, gfin_ref[...])


def _mixer_kernel(relb_ref, sink_ref, x1_ref, q_ref, kv_ref, f_ref, gate_ref, bucket_ref,
                  dftc_ref, dfts_ref, wa_ref, wb_ref, wout_ref, out_ref,
                  bias_ref, kpad_ref, vt_ref, frev_ref, uw_ref, o_ref, s_ref, p_ref):
    b = pl.program_id(0)
    t = pl.program_id(1)

    @pl.when((b == 0) & (t == 0))
    def _build_bias():
        bucket = bucket_ref[...]
        kj = lax.broadcasted_iota(jnp.int32, (3 * BLK, BLK), 0)
        qi = lax.broadcasted_iota(jnp.int32, (3 * BLK, BLK), 1)
        in_window = jnp.abs(kj - BLK - qi) <= WINDOW
        valid = (in_window, in_window & (kj >= BLK), in_window & (kj < 2 * BLK))
        for h in range(N_HEADS):
            def pick(bk, acc, h=h):
                return jnp.where(bucket == bk, relb_ref[bk, h] * LOG2E, acc)
            base = lax.fori_loop(0, N_BUCKETS, pick, jnp.zeros((3 * BLK, BLK), F32))
            for v in range(3):
                bias_ref[v, h] = jnp.where(valid[v], base, -jnp.inf)

    @pl.when(t == 0)
    def _per_sequence():
        kpad_ref[0:BLK, :] = jnp.zeros((BLK, KV_WIDTH), BF16)
        kpad_ref[BLK:BLK + SEQ, :] = kv_ref[:, :KV_WIDTH]
        kpad_ref[BLK + SEQ:, :] = jnp.zeros((BLK, KV_WIDTH), BF16)
        vt_ref[:, 0:BLK] = jnp.zeros((N_KV * VT_ROWS, BLK), BF16)
        vt_ref[:, BLK + SEQ:] = jnp.zeros((N_KV * VT_ROWS, BLK), BF16)
        for c in range(SEQ // BLK):
            v_t = kv_ref[c * BLK:(c + 1) * BLK, KV_WIDTH:].T
            for kh in range(N_KV):
                vt_ref[kh * VT_ROWS:kh * VT_ROWS + HD, (c + 1) * BLK:(c + 2) * BLK] = (
                    v_t[kh * HD:(kh + 1) * HD])
        for kh in range(N_KV):
            vt_ref[kh * VT_ROWS + HD:(kh + 1) * VT_ROWS, :] = jnp.ones(
                (VT_ROWS - HD, SEQ + 2 * BLK), BF16)
        jj = lax.broadcasted_iota(jnp.int32, (BLK, 2 * BLK), 0)
        cc = lax.broadcasted_iota(jnp.int32, (BLK, 2 * BLK), 1)
        flip = jnp.where(jj + cc == BLK, 1.0, 0.0).astype(BF16)
        for blk in range(HALF_SEQ // BLK):
            src = SEQ - (blk + 1) * BLK
            if blk == 0:
                rev = _dot(flip[:, :BLK], f_ref[src:src + BLK, :])
            else:
                rev = _dot(flip, f_ref[src:src + 2 * BLK, :])
            frev_ref[blk * BLK:(blk + 1) * BLK, :] = rev.astype(BF16)
        first_row = lax.broadcasted_iota(jnp.int32, (V7X_BF16_SUBLANES, FG_DIM), 0) == 0
        for g in range(N_FGROUPS):
            lo = g * FG_DIM
            both = jnp.concatenate([f_ref[0:HALF_SEQ, lo:lo + FG_DIM],
                                    frev_ref[:, lo:lo + FG_DIM]], axis=1)
            r = _dot(both, dftc_ref[...])
            uw_ref[0:HALF_SEQ, lo:lo + FG_DIM] = r[:, :FG_DIM].astype(BF16)
            uw_ref[HALF_SEQ:, lo:lo + FG_DIM] = r[:, FG_DIM:].astype(BF16)
            mid = _dot(f_ref[HALF_SEQ:HALF_SEQ + V7X_BF16_SUBLANES, lo:lo + FG_DIM],
                       dftc_ref[0:FG_DIM, 0:FG_DIM])
            head = uw_ref[HALF_SEQ:HALF_SEQ + V7X_BF16_SUBLANES, lo:lo + FG_DIM]
            uw_ref[HALF_SEQ:HALF_SEQ + V7X_BF16_SUBLANES, lo:lo + FG_DIM] = jnp.where(
                first_row, mid, head.astype(F32)).astype(BF16)

    def block_keys(i):
        n = t * BLOCKS_PER_TILE + i
        variant = jnp.where(n == 0, 1, jnp.where(n == SEQ // BLK - 1, 2, 0))
        return pl.multiple_of(n * BLK, BLK), variant

    def scores(i):
        key0, _ = block_keys(i)
        q_t = q_ref[i * BLK:(i + 1) * BLK, :].T
        zero = jnp.zeros((HD, BLK), BF16)
        cols = []
        for h in range(N_HEADS):
            q_h = q_t[h * HD:(h + 1) * HD]
            cols.append(jnp.concatenate([q_h, zero] if h < GQ else [zero, q_h], axis=0))
        s_ref[i % 2] = _dot(kpad_ref[pl.ds(key0, 3 * BLK), :], jnp.concatenate(cols, axis=1))

    def softmax(i):
        _, variant = block_keys(i)
        buf = i % 2
        sink_terms = []
        for h in range(N_HEADS):
            sink = sink_ref[h] * LOG2E
            cols = slice(h * BLK, (h + 1) * BLK)
            top = s_ref[buf, 0:KEY_CHUNK, cols] + bias_ref[variant, h, 0:KEY_CHUNK, :]
            for k0 in range(KEY_CHUNK, 3 * BLK, KEY_CHUNK):
                top = jnp.maximum(top, s_ref[buf, k0:k0 + KEY_CHUNK, cols]
                                  + bias_ref[variant, h, k0:k0 + KEY_CHUNK, :])
            m = jnp.maximum(jnp.max(top, axis=0, keepdims=True), sink)
            for k0 in range(0, 3 * BLK, KEY_CHUNK):
                z = (s_ref[buf, k0:k0 + KEY_CHUNK, cols] - m) + bias_ref[variant, h, k0:k0 + KEY_CHUNK, :]
                p_ref[buf, k0:k0 + KEY_CHUNK, cols] = jnp.exp2(z).astype(BF16)
            sink_terms.append(jnp.exp2(sink - m))
        return sink_terms

    def weighted_values(i, sink_terms):
        key0, _ = block_keys(i)
        for kh in range(N_KV):
            o_t = _dot(vt_ref[kh * VT_ROWS:(kh + 1) * VT_ROWS, pl.ds(key0, 3 * BLK)],
                       p_ref[i % 2, :, kh * GQ * BLK:(kh + 1) * GQ * BLK])
            den = o_t[HD:HD + 1] + jnp.concatenate(sink_terms[kh * GQ:(kh + 1) * GQ], axis=1)
            o_n = o_t[:HD] / den
            for pr in range(GQ // 2):
                pair_t = jnp.concatenate([o_n[:, 2 * pr * BLK:(2 * pr + 1) * BLK],
                                          o_n[:, (2 * pr + 1) * BLK:(2 * pr + 2) * BLK]], axis=0)
                lane0 = (kh * GQ + 2 * pr) * HD
                o_ref[i * BLK:(i + 1) * BLK, lane0:lane0 + 2 * HD] = pair_t.T.astype(BF16)

    scores(0)
    for i in range(BLOCKS_PER_TILE):
        if i + 1 < BLOCKS_PER_TILE:
            scores(i + 1)
        weighted_values(i, softmax(i))

    y_a = _dot(o_ref[...], wa_ref[...])
    fmix = _dot(dfts_ref[...], uw_ref[...]).astype(BF16)
    y_b = _dot(fmix, wb_ref[...])
    mix = (gate_ref[:, :D_MODEL].astype(F32) * y_a + gate_ref[:, D_MODEL:].astype(F32) * y_b)
    out_ref[...] = x1_ref[...] + _dot(mix.astype(BF16), wout_ref[...])


def _resident(shape):
    return pl.BlockSpec(shape, lambda *_: (0,) * len(shape), pipeline_mode=pl.Buffered(1))


def _rows(width):
    return pl.BlockSpec((ROW_TILE, width), lambda i: (i, 0))


def _ffn_proj(x, g1, wup, wdn, gmix, win, bg):
    return pl.pallas_call(
        _ffn_proj_kernel,
        grid=(TOKENS // ROW_TILE,),
        in_specs=[_rows(D_MODEL), _resident((1, D_MODEL)), _resident((D_MODEL, 2 * D_FF)),
                  _resident((D_FF, D_MODEL)), _resident((1, D_MODEL)),
                  _resident((D_MODEL, IN_WIDTH)), _resident((1, 2 * D_MODEL))],
        out_specs=[_rows(D_MODEL), _rows(Q_WIDTH), _rows(2 * KV_WIDTH), _rows(F_WIDTH),
                   _rows(2 * D_MODEL)],
        out_shape=[jax.ShapeDtypeStruct((TOKENS, D_MODEL), F32),
                   jax.ShapeDtypeStruct((TOKENS, Q_WIDTH), BF16),
                   jax.ShapeDtypeStruct((TOKENS, 2 * KV_WIDTH), BF16),
                   jax.ShapeDtypeStruct((TOKENS, F_WIDTH), BF16),
                   jax.ShapeDtypeStruct((TOKENS, 2 * D_MODEL), BF16)],
        scratch_shapes=[pltpu.VMEM((ROW_TILE, D_FF), BF16)],
        compiler_params=pltpu.CompilerParams(dimension_semantics=("arbitrary",),
                                             vmem_limit_bytes=V7X_VMEM_LIMIT_BYTES),
        name="ffn_proj",
    )(x, g1, wup, wdn, gmix, win, bg)


def _ffn_final(x, g2, wup, wdn, gfin):
    return pl.pallas_call(
        _ffn_final_kernel,
        grid=(TOKENS // ROW_TILE,),
        in_specs=[_rows(D_MODEL), _resident((1, D_MODEL)), _resident((D_MODEL, 2 * D_FF)),
                  _resident((D_FF, D_MODEL)), _resident((1, D_MODEL))],
        out_specs=_rows(D_MODEL),
        out_shape=jax.ShapeDtypeStruct((TOKENS, D_MODEL), F32),
        scratch_shapes=[pltpu.VMEM((ROW_TILE, D_FF), BF16)],
        compiler_params=pltpu.CompilerParams(dimension_semantics=("arbitrary",),
                                             vmem_limit_bytes=V7X_VMEM_LIMIT_BYTES),
        name="ffn_final",
    )(x, g2, wup, wdn, gfin)


def _mixer(rel_bias, sink, x1, q, kv, f, gates, bucket, dftc, dfts, wa, wb, wout):
    def tile_rows(width):
        return pl.BlockSpec((ROW_TILE, width), lambda b, t: (b * TILES_PER_SEQ + t, 0))

    def seq_rows(width):
        return pl.BlockSpec((SEQ, width), lambda b, t: (b, 0))

    smem = pl.BlockSpec(memory_space=pltpu.SMEM)
    return pl.pallas_call(
        _mixer_kernel,
        grid=(BATCH, TILES_PER_SEQ),
        in_specs=[smem, smem, tile_rows(D_MODEL), tile_rows(Q_WIDTH), seq_rows(2 * KV_WIDTH),
                  seq_rows(F_WIDTH), tile_rows(2 * D_MODEL), _resident((3 * BLK, BLK)),
                  _resident((2 * FG_DIM, 2 * FG_DIM)),
                  pl.BlockSpec((ROW_TILE, SEQ), lambda b, t: (t, 0)),
                  _resident((Q_WIDTH, D_MODEL)), _resident((F_WIDTH, D_MODEL)),
                  _resident((D_MODEL, D_MODEL))],
        out_specs=tile_rows(D_MODEL),
        out_shape=jax.ShapeDtypeStruct((TOKENS, D_MODEL), F32),
        scratch_shapes=[pltpu.VMEM((3, N_HEADS, 3 * BLK, BLK), F32),
                        pltpu.VMEM((SEQ + 2 * BLK, KV_WIDTH), BF16),
                        pltpu.VMEM((N_KV * VT_ROWS, SEQ + 2 * BLK), BF16),
                        pltpu.VMEM((HALF_SEQ, F_WIDTH), BF16),
                        pltpu.VMEM((SEQ, F_WIDTH), BF16),
                        pltpu.VMEM((ROW_TILE, Q_WIDTH), BF16),
                        pltpu.VMEM((2, 3 * BLK, N_HEADS * BLK), F32),
                        pltpu.VMEM((2, 3 * BLK, N_HEADS * BLK), BF16)],
        compiler_params=pltpu.CompilerParams(dimension_semantics=("arbitrary", "arbitrary"),
                                             vmem_limit_bytes=V7X_VMEM_LIMIT_BYTES),
        name="mixer",
    )(rel_bias, sink, x1, q, kv, f, gates, bucket, dftc, dfts, wa, wb, wout)


def kernel(x, g_ffn1, w_up1, w_down1, g_mix, w_in, b_gate, sink, rel_bias, w_branch_a, w_branch_b, w_out, g_ffn2, w_up2, w_down2, g_final):
    assert x.shape == (BATCH, SEQ, D_MODEL) and w_up1.shape[0] == DEPTH == 1
    chan, pos = _dft_tables()
    bucket = jnp.asarray(_t5_bucket_table().T)
    dftc = jnp.asarray(chan).astype(BF16)
    dfts = jnp.asarray(pos).astype(BF16)
    row = lambda v: v.reshape(1, -1)
    x0 = x.reshape(TOKENS, D_MODEL)
    x1, q, kv, f, gates = _ffn_proj(
        x0, row(g_ffn1[0]), w_up1[0].astype(BF16), w_down1[0].astype(BF16), row(g_mix[0]),
        w_in[0].astype(BF16), row(b_gate[0]))
    x2 = _mixer(rel_bias, sink[0], x1, q, kv, f, gates, bucket, dftc, dfts,
                w_branch_a[0].astype(BF16), w_branch_b[0].astype(BF16), w_out[0].astype(BF16))
    out = _ffn_final(x2, row(g_ffn2[0]), w_up2[0].astype(BF16), w_down2[0].astype(BF16),
                     row(g_final))
    return out.reshape(BATCH, SEQ, D_MODEL)
```

```python
import functools

import numpy as np
import jax
import jax.numpy as jnp
from jax import lax
from jax.experimental import pallas as pl
from jax.experimental.pallas import tpu as pltpu

D_MODEL = 1024
BATCH = 8
SEQ = 2048
DEPTH = 1
N_HEADS = 8
N_KV = 2
GQ = N_HEADS // N_KV
HD = 64
Q_WIDTH = N_HEADS * HD
KV_WIDTH = N_KV * HD
WINDOW = 128
BLK = 128
N_FGROUPS = 4
FG_DIM = 128
F_WIDTH = N_FGROUPS * FG_DIM
N_BUCKETS = 32
MAX_DIST = 128
D_FF = 2816
EPS = 1e-6
O_K = Q_WIDTH
O_V = O_K + KV_WIDTH
O_F = O_V + KV_WIDTH
O_G = O_F + F_WIDTH
IN_WIDTH = O_G + 2 * D_MODEL

F32 = jnp.float32
BF16 = jnp.bfloat16

V7X_MXU_DIM = 256
V7X_BF16_SUBLANES = 16
V7X_VMEM_LIMIT_BYTES = 56 * 1024 * 1024

TOKENS = BATCH * SEQ
HALF_SEQ = SEQ // 2
ROW_TILE = 512
TILES_PER_SEQ = SEQ // ROW_TILE
BLOCKS_PER_TILE = ROW_TILE // BLK
FF_CHUNK = V7X_MXU_DIM
VT_ROWS = HD + V7X_BF16_SUBLANES
KEY_CHUNK = 64
LOG2E = float(np.log2(np.e))


def _t5_bucket_table():
    rel = (np.arange(3 * BLK)[None, :] - BLK) - np.arange(BLK)[:, None]
    half = N_BUCKETS // 2
    max_exact = half // 2
    ret = (rel > 0).astype(np.int32) * half
    n = np.abs(rel)
    n_safe = np.maximum(n, 1).astype(np.float32)
    large = max_exact + (np.log(n_safe / max_exact) / np.log(MAX_DIST / max_exact)
                         * (half - max_exact)).astype(np.int32)
    large = np.minimum(large, half - 1)
    return (ret + np.where(n < max_exact, n, large)).astype(np.int32)


@functools.lru_cache(maxsize=None)
def _dft_tables():
    def cos_sin(n):
        idx = np.arange(n)
        ang = 2.0 * np.pi * ((idx[:, None] * idx[None, :]) % n).astype(np.float64) / n
        return np.cos(ang) / np.sqrt(n), np.sin(ang) / np.sqrt(n)
    cc, sc = cos_sin(FG_DIM)
    cs, ss = cos_sin(SEQ)
    chan = np.block([[cc, sc], [cc, -sc]]).astype(np.float32)
    neg_sin = -ss[:, :HALF_SEQ]
    neg_sin[:, 0] = np.where(np.arange(SEQ) % 2 == 0, 1.0, -1.0) / np.sqrt(SEQ)
    pos = np.concatenate([cs[:, :HALF_SEQ], neg_sin], axis=1).astype(np.float32)
    return chan, pos


def _rms(x, g):
    return x * lax.rsqrt(jnp.mean(x * x, axis=-1, keepdims=True) + EPS) * g


def _sigmoid(z):
    return 1.0 / (1.0 + jnp.exp(-z))


def _dot(a, b):
    return jnp.dot(a, b, preferred_element_type=F32)


def _swiglu(h, wup_ref, wdn_ref, act_ref):
    for c in range(D_FF // FF_CHUNK):
        lo = c * FF_CHUNK
        gate = _dot(h, wup_ref[:, lo:lo + FF_CHUNK])
        up = _dot(h, wup_ref[:, D_FF + lo:D_FF + lo + FF_CHUNK])
        act_ref[:, lo:lo + FF_CHUNK] = (gate * _sigmoid(gate) * up).astype(BF16)
    return _dot(act_ref[...], wdn_ref[...])


def _ffn_proj_kernel(x_ref, g1_ref, wup_ref, wdn_ref, gmix_ref, win_ref, bg_ref,
                     x1_ref, q_ref, kv_ref, f_ref, gate_ref, act_ref):
    x = x_ref[...]
    h = _rms(x, g1_ref[...]).astype(BF16)
    x1 = x + 0.5 * _swiglu(h, wup_ref, wdn_ref, act_ref)
    x1_ref[...] = x1
    h2 = _rms(x1, gmix_ref[...]).astype(BF16)
    q_ref[...] = (_dot(h2, win_ref[:, 0:O_K]) * (HD ** -0.5 * LOG2E)).astype(BF16)
    kv_ref[...] = _dot(h2, win_ref[:, O_K:O_F]).astype(BF16)
    f_ref[...] = _dot(h2, win_ref[:, O_F:O_G]).astype(BF16)
    for c in range(2):
        lo = c * D_MODEL
        z = _dot(h2, win_ref[:, O_G + lo:O_G + lo + D_MODEL]) + bg_ref[:, lo:lo + D_MODEL]
        gate_ref[:, lo:lo + D_MODEL] = _sigmoid(z).astype(BF16)


def _ffn_final_kernel(x_ref, g2_ref, wup_ref, wdn_ref, gfin_ref, out_ref, act_ref):
    x = x_ref[...]
    h = _rms(x, g2_ref[...]).astype(BF16)
    x3 = x + 0.5 * _swiglu(h, wup_ref, wdn_ref, act_ref)
    out_ref[...] = _rms(x3, gfin_ref[...])


def _mixer_kernel(relb_ref, sink_ref, x1_ref, q_ref, kv_ref, f_ref, gate_ref, bucket_ref,
                  dftc_ref, dfts_ref, wa_ref, wb_ref, wout_ref, out_ref,
                  bias_ref, kpad_ref, vt_ref, frev_ref, uw_ref, o_ref, s_ref, p_ref,
                  fmix_ref, yb_ref):
    b = pl.program_id(0)
    t = pl.program_id(1)

    @pl.when((b == 0) & (t == 0))
    def _build_bias():
        bucket = bucket_ref[...]
        kj = lax.broadcasted_iota(jnp.int32, (3 * BLK, BLK), 0)
        qi = lax.broadcasted_iota(jnp.int32, (3 * BLK, BLK), 1)
        in_window = jnp.abs(kj - BLK - qi) <= WINDOW
        valid = (in_window, in_window & (kj >= BLK), in_window & (kj < 2 * BLK))
        for h in range(N_HEADS):
            def pick(bk, acc, h=h):
                return jnp.where(bucket == bk, relb_ref[bk, h] * LOG2E, acc)
            base = lax.fori_loop(0, N_BUCKETS, pick, jnp.zeros((3 * BLK, BLK), F32))
            for v in range(3):
                bias_ref[v, h] = jnp.where(valid[v], base, -jnp.inf)

    @pl.when(t == 0)
    def _per_sequence():
        kpad_ref[0:BLK, :] = jnp.zeros((BLK, KV_WIDTH), BF16)
        kpad_ref[BLK:BLK + SEQ, :] = kv_ref[:, :KV_WIDTH]
        kpad_ref[BLK + SEQ:, :] = jnp.zeros((BLK, KV_WIDTH), BF16)
        vt_ref[:, 0:BLK] = jnp.zeros((N_KV * VT_ROWS, BLK), BF16)
        vt_ref[:, BLK + SEQ:] = jnp.zeros((N_KV * VT_ROWS, BLK), BF16)
        for c in range(SEQ // BLK):
            v_t = kv_ref[c * BLK:(c + 1) * BLK, KV_WIDTH:].T
            for kh in range(N_KV):
                vt_ref[kh * VT_ROWS:kh * VT_ROWS + HD, (c + 1) * BLK:(c + 2) * BLK] = (
                    v_t[kh * HD:(kh + 1) * HD])
        for kh in range(N_KV):
            vt_ref[kh * VT_ROWS + HD:(kh + 1) * VT_ROWS, :] = jnp.ones(
                (VT_ROWS - HD, SEQ + 2 * BLK), BF16)
        jj = lax.broadcasted_iota(jnp.int32, (BLK, 2 * BLK), 0)
        cc = lax.broadcasted_iota(jnp.int32, (BLK, 2 * BLK), 1)
        flip = jnp.where(jj + cc == BLK, 1.0, 0.0).astype(BF16)
        for blk in range(HALF_SEQ // BLK):
            src = SEQ - (blk + 1) * BLK
            if blk == 0:
                rev = _dot(flip[:, :BLK], f_ref[src:src + BLK, :])
            else:
                rev = _dot(flip, f_ref[src:src + 2 * BLK, :])
            frev_ref[blk * BLK:(blk + 1) * BLK, :] = rev.astype(BF16)
        first_row = lax.broadcasted_iota(jnp.int32, (V7X_BF16_SUBLANES, FG_DIM), 0) == 0
        for g in range(N_FGROUPS):
            lo = g * FG_DIM
            both = jnp.concatenate([f_ref[0:HALF_SEQ, lo:lo + FG_DIM],
                                    frev_ref[:, lo:lo + FG_DIM]], axis=1)
            r = _dot(both, dftc_ref[...])
            uw_ref[0:HALF_SEQ, lo:lo + FG_DIM] = r[:, :FG_DIM].astype(BF16)
            uw_ref[HALF_SEQ:, lo:lo + FG_DIM] = r[:, FG_DIM:].astype(BF16)
            mid = _dot(f_ref[HALF_SEQ:HALF_SEQ + V7X_BF16_SUBLANES, lo:lo + FG_DIM],
                       dftc_ref[0:FG_DIM, 0:FG_DIM])
            head = uw_ref[HALF_SEQ:HALF_SEQ + V7X_BF16_SUBLANES, lo:lo + FG_DIM]
            uw_ref[HALF_SEQ:HALF_SEQ + V7X_BF16_SUBLANES, lo:lo + FG_DIM] = jnp.where(
                first_row, mid, head.astype(F32)).astype(BF16)

    def block_keys(i):
        n = t * BLOCKS_PER_TILE + i
        variant = jnp.where(n == 0, 1, jnp.where(n == SEQ // BLK - 1, 2, 0))
        return pl.multiple_of(n * BLK, BLK), variant

    def scores(i):
        key0, _ = block_keys(i)
        q_t = q_ref[i * BLK:(i + 1) * BLK, :].T
        zero = jnp.zeros((HD, BLK), BF16)
        cols = []
        for h in range(N_HEADS):
            q_h = q_t[h * HD:(h + 1) * HD]
            cols.append(jnp.concatenate([q_h, zero] if h < GQ else [zero, q_h], axis=0))
        s_ref[i % 2] = _dot(kpad_ref[pl.ds(key0, 3 * BLK), :], jnp.concatenate(cols, axis=1))

    def softmax(i):
        _, variant = block_keys(i)
        buf = i % 2
        sink_terms = []
        for h in range(N_HEADS):
            sink = sink_ref[h] * LOG2E
            cols = slice(h * BLK, (h + 1) * BLK)
            top = s_ref[buf, 0:KEY_CHUNK, cols] + bias_ref[variant, h, 0:KEY_CHUNK, :]
            for k0 in range(KEY_CHUNK, 3 * BLK, KEY_CHUNK):
                top = jnp.maximum(top, s_ref[buf, k0:k0 + KEY_CHUNK, cols]
                                  + bias_ref[variant, h, k0:k0 + KEY_CHUNK, :])
            m = jnp.maximum(jnp.max(top, axis=0, keepdims=True), sink)
            for k0 in range(0, 3 * BLK, KEY_CHUNK):
                z = (s_ref[buf, k0:k0 + KEY_CHUNK, cols] - m) + bias_ref[variant, h, k0:k0 + KEY_CHUNK, :]
                p_ref[buf, k0:k0 + KEY_CHUNK, cols] = jnp.exp2(z).astype(BF16)
            sink_terms.append(jnp.exp2(sink - m))
        return sink_terms

    def weighted_values(i, sink_terms):
        key0, _ = block_keys(i)
        for kh in range(N_KV):
            o_t = _dot(vt_ref[kh * VT_ROWS:(kh + 1) * VT_ROWS, pl.ds(key0, 3 * BLK)],
                       p_ref[i % 2, :, kh * GQ * BLK:(kh + 1) * GQ * BLK])
            den = o_t[HD:HD + 1] + jnp.concatenate(sink_terms[kh * GQ:(kh + 1) * GQ], axis=1)
            o_n = o_t[:HD] / den
            for pr in range(GQ // 2):
                pair_t = jnp.concatenate([o_n[:, 2 * pr * BLK:(2 * pr + 1) * BLK],
                                          o_n[:, (2 * pr + 1) * BLK:(2 * pr + 2) * BLK]], axis=0)
                lane0 = (kh * GQ + 2 * pr) * HD
                o_ref[i * BLK:(i + 1) * BLK, lane0:lane0 + 2 * HD] = pair_t.T.astype(BF16)

    half_tile = ROW_TILE // 2
    scores(0)
    for i in range(BLOCKS_PER_TILE):
        if i + 1 < BLOCKS_PER_TILE:
            scores(i + 1)
        if i < 2:
            rows = slice(i * half_tile, (i + 1) * half_tile)
            fmix_ref[rows, :] = _dot(dfts_ref[rows, :], uw_ref[...]).astype(BF16)
        elif i == 2:
            yb_ref[...] = _dot(fmix_ref[...], wb_ref[...])
        weighted_values(i, softmax(i))

    y_a = _dot(o_ref[...], wa_ref[...])
    mix = (gate_ref[:, :D_MODEL].astype(F32) * y_a + gate_ref[:, D_MODEL:].astype(F32) * yb_ref[...])
    out_ref[...] = x1_ref[...] + _dot(mix.astype(BF16), wout_ref[...])


def _resident(shape):
    return pl.BlockSpec(shape, lambda *_: (0,) * len(shape), pipeline_mode=pl.Buffered(1))


def _rows(width):
    return pl.BlockSpec((ROW_TILE, width), lambda i: (i, 0))


def _ffn_proj(x, g1, wup, wdn, gmix, win, bg):
    return pl.pallas_call(
        _ffn_proj_kernel,
        grid=(TOKENS // ROW_TILE,),
        in_specs=[_rows(D_MODEL), _resident((1, D_MODEL)), _resident((D_MODEL, 2 * D_FF)),
                  _resident((D_FF, D_MODEL)), _resident((1, D_MODEL)),
                  _resident((D_MODEL, IN_WIDTH)), _resident((1, 2 * D_MODEL))],
        out_specs=[_rows(D_MODEL), _rows(Q_WIDTH), _rows(2 * KV_WIDTH), _rows(F_WIDTH),
                   _rows(2 * D_MODEL)],
        out_shape=[jax.ShapeDtypeStruct((TOKENS, D_MODEL), F32),
                   jax.ShapeDtypeStruct((TOKENS, Q_WIDTH), BF16),
                   jax.ShapeDtypeStruct((TOKENS, 2 * KV_WIDTH), BF16),
                   jax.ShapeDtypeStruct((TOKENS, F_WIDTH), BF16),
                   jax.ShapeDtypeStruct((TOKENS, 2 * D_MODEL), BF16)],
        scratch_shapes=[pltpu.VMEM((ROW_TILE, D_FF), BF16)],
        compiler_params=pltpu.CompilerParams(dimension_semantics=("arbitrary",),
                                             vmem_limit_bytes=V7X_VMEM_LIMIT_BYTES),
        name="ffn_proj",
    )(x, g1, wup, wdn, gmix, win, bg)


def _ffn_final(x, g2, wup, wdn, gfin):
    return pl.pallas_call(
        _ffn_final_kernel,
        grid=(TOKENS // ROW_TILE,),
        in_specs=[_rows(D_MODEL), _resident((1, D_MODEL)), _resident((D_MODEL, 2 * D_FF)),
                  _resident((D_FF, D_MODEL)), _resident((1, D_MODEL))],
        out_specs=_rows(D_MODEL),
        out_shape=jax.ShapeDtypeStruct((TOKENS, D_MODEL), F32),
        scratch_shapes=[pltpu.VMEM((ROW_TILE, D_FF), BF16)],
        compiler_params=pltpu.CompilerParams(dimension_semantics=("arbitrary",),
                                             vmem_limit_bytes=V7X_VMEM_LIMIT_BYTES),
        name="ffn_final",
    )(x, g2, wup, wdn, gfin)


def _mixer(rel_bias, sink, x1, q, kv, f, gates, bucket, dftc, dfts, wa, wb, wout):
    def tile_rows(width):
        return pl.BlockSpec((ROW_TILE, width), lambda b, t: (b * TILES_PER_SEQ + t, 0))

    def seq_rows(width):
        return pl.BlockSpec((SEQ, width), lambda b, t: (b, 0))

    smem = pl.BlockSpec(memory_space=pltpu.SMEM)
    return pl.pallas_call(
        _mixer_kernel,
        grid=(BATCH, TILES_PER_SEQ),
        in_specs=[smem, smem, tile_rows(D_MODEL), tile_rows(Q_WIDTH), seq_rows(2 * KV_WIDTH),
                  seq_rows(F_WIDTH), tile_rows(2 * D_MODEL), _resident((3 * BLK, BLK)),
                  _resident((2 * FG_DIM, 2 * FG_DIM)),
                  pl.BlockSpec((ROW_TILE, SEQ), lambda b, t: (t, 0)),
                  _resident((Q_WIDTH, D_MODEL)), _resident((F_WIDTH, D_MODEL)),
                  _resident((D_MODEL, D_MODEL))],
        out_specs=tile_rows(D_MODEL),
        out_shape=jax.ShapeDtypeStruct((TOKENS, D_MODEL), F32),
        scratch_shapes=[pltpu.VMEM((3, N_HEADS, 3 * BLK, BLK), F32),
                        pltpu.VMEM((SEQ + 2 * BLK, KV_WIDTH), BF16),
                        pltpu.VMEM((N_KV * VT_ROWS, SEQ + 2 * BLK), BF16),
                        pltpu.VMEM((HALF_SEQ, F_WIDTH), BF16),
                        pltpu.VMEM((SEQ, F_WIDTH), BF16),
                        pltpu.VMEM((ROW_TILE, Q_WIDTH), BF16),
                        pltpu.VMEM((2, 3 * BLK, N_HEADS * BLK), F32),
                        pltpu.VMEM((2, 3 * BLK, N_HEADS * BLK), BF16),
                        pltpu.VMEM((ROW_TILE, F_WIDTH), BF16),
                        pltpu.VMEM((ROW_TILE, D_MODEL), F32)],
        compiler_params=pltpu.CompilerParams(dimension_semantics=("arbitrary", "arbitrary"),
                                             vmem_limit_bytes=V7X_VMEM_LIMIT_BYTES),
        name="mixer",
    )(rel_bias, sink, x1, q, kv, f, gates, bucket, dftc, dfts, wa, wb, wout)


def kernel(x, g_ffn1, w_up1, w_down1, g_mix, w_in, b_gate, sink, rel_bias, w_branch_a, w_branch_b, w_out, g_ffn2, w_up2, w_down2, g_final):
    assert x.shape == (BATCH, SEQ, D_MODEL) and w_up1.shape[0] == DEPTH == 1
    chan, pos = _dft_tables()
    bucket = jnp.asarray(_t5_bucket_table().T)
    dftc = jnp.asarray(chan).astype(BF16)
    dfts = jnp.asarray(pos).astype(BF16)
    row = lambda v: v.reshape(1, -1)
    x0 = x.reshape(TOKENS, D_MODEL)
    x1, q, kv, f, gates = _ffn_proj(
        x0, row(g_ffn1[0]), w_up1[0].astype(BF16), w_down1[0].astype(BF16), row(g_mix[0]),
        w_in[0].astype(BF16), row(b_gate[0]))
    x2 = _mixer(rel_bias, sink[0], x1, q, kv, f, gates, bucket, dftc, dfts,
                w_branch_a[0].astype(BF16), w_branch_b[0].astype(BF16), w_out[0].astype(BF16))
    out = _ffn_final(x2, row(g_ffn2[0]), w_up2[0].astype(BF16), w_down2[0].astype(BF16),
                     row(g_final))
    return out.reshape(BATCH, SEQ, D_MODEL)
```

```python
import functools

import numpy as np
import jax
import jax.numpy as jnp
from jax import lax
from jax.experimental import pallas as pl
from jax.experimental.pallas import tpu as pltpu

D_MODEL = 1024
BATCH = 8
SEQ = 2048
DEPTH = 1
N_HEADS = 8
N_KV = 2
GQ = N_HEADS // N_KV
HD = 64
Q_WIDTH = N_HEADS * HD
KV_WIDTH = N_KV * HD
WINDOW = 128
BLK = 128
N_FGROUPS = 4
FG_DIM = 128
F_WIDTH = N_FGROUPS * FG_DIM
N_BUCKETS = 32
MAX_DIST = 128
D_FF = 2816
EPS = 1e-6
O_K = Q_WIDTH
O_V = O_K + KV_WIDTH
O_F = O_V + KV_WIDTH
O_G = O_F + F_WIDTH
IN_WIDTH = O_G + 2 * D_MODEL

F32 = jnp.float32
BF16 = jnp.bfloat16

V7X_MXU_DIM = 256
V7X_BF16_SUBLANES = 16
V7X_VMEM_LIMIT_BYTES = 56 * 1024 * 1024

TOKENS = BATCH * SEQ
HALF_SEQ = SEQ // 2
ROW_TILE = 512
TILES_PER_SEQ = SEQ // ROW_TILE
BLOCKS_PER_TILE = ROW_TILE // BLK
FF_CHUNK = V7X_MXU_DIM
VT_ROWS = HD + V7X_BF16_SUBLANES
KEY_CHUNK = 64
LOG2E = float(np.log2(np.e))


def _t5_bucket_table():
    rel = (np.arange(3 * BLK)[None, :] - BLK) - np.arange(BLK)[:, None]
    half = N_BUCKETS // 2
    max_exact = half // 2
    ret = (rel > 0).astype(np.int32) * half
    n = np.abs(rel)
    n_safe = np.maximum(n, 1).astype(np.float32)
    large = max_exact + (np.log(n_safe / max_exact) / np.log(MAX_DIST / max_exact)
                         * (half - max_exact)).astype(np.int32)
    large = np.minimum(large, half - 1)
    return (ret + np.where(n < max_exact, n, large)).astype(np.int32)


@functools.lru_cache(maxsize=None)
def _dft_tables():
    def cos_sin(n):
        idx = np.arange(n)
        ang = 2.0 * np.pi * ((idx[:, None] * idx[None, :]) % n).astype(np.float64) / n
        return np.cos(ang) / np.sqrt(n), np.sin(ang) / np.sqrt(n)
    cc, sc = cos_sin(FG_DIM)
    cs, ss = cos_sin(SEQ)
    chan = np.block([[cc, sc], [cc, -sc]]).astype(np.float32)
    neg_sin = -ss[:, :HALF_SEQ]
    neg_sin[:, 0] = np.where(np.arange(SEQ) % 2 == 0, 1.0, -1.0) / np.sqrt(SEQ)
    pos = np.concatenate([cs[:, :HALF_SEQ], neg_sin], axis=1).astype(np.float32)
    return chan, pos


def _rms(x, g):
    return x * lax.rsqrt(jnp.mean(x * x, axis=-1, keepdims=True) + EPS) * g


def _sigmoid(z):
    return 1.0 / (1.0 + jnp.exp(-z))


def _dot(a, b):
    return jnp.dot(a, b, preferred_element_type=F32)


def _swiglu(h, wup_ref, wdn_ref, act_ref):
    for c in range(D_FF // FF_CHUNK):
        lo = c * FF_CHUNK
        gate = _dot(h, wup_ref[:, lo:lo + FF_CHUNK])
        up = _dot(h, wup_ref[:, D_FF + lo:D_FF + lo + FF_CHUNK])
        act_ref[:, lo:lo + FF_CHUNK] = (gate * _sigmoid(gate) * up).astype(BF16)
    return _dot(act_ref[...], wdn_ref[...])


def _ffn_proj_kernel(x_ref, g1_ref, wup_ref, wdn_ref, gmix_ref, win_ref, bg_ref,
                     x1_ref, q_ref, kv_ref, f_ref, gate_ref, act_ref):
    x = x_ref[...]
    h = _rms(x, g1_ref[...]).astype(BF16)
    x1 = x + 0.5 * _swiglu(h, wup_ref, wdn_ref, act_ref)
    x1_ref[...] = x1
    h2 = _rms(x1, gmix_ref[...]).astype(BF16)
    q_ref[...] = (_dot(h2, win_ref[:, 0:O_K]) * (HD ** -0.5 * LOG2E)).astype(BF16)
    kv_ref[...] = _dot(h2, win_ref[:, O_K:O_F]).astype(BF16)
    f_ref[...] = _dot(h2, win_ref[:, O_F:O_G]).astype(BF16)
    for c in range(2):
        lo = c * D_MODEL
        z = _dot(h2, win_ref[:, O_G + lo:O_G + lo + D_MODEL]) + bg_ref[:, lo:lo + D_MODEL]
        gate_ref[:, lo:lo + D_MODEL] = _sigmoid(z).astype(BF16)


def _ffn_final_kernel(x_ref, g2_ref, wup_ref, wdn_ref, gfin_ref, out_ref, act_ref):
    x = x_ref[...]
    h = _rms(x, g2_ref[...]).astype(BF16)
    x3 = x + 0.5 * _swiglu(h, wup_ref, wdn_ref, act_ref)
    out_ref[...] = _rms(x3, gfin_ref[...])


def _mixer_kernel(relb_ref, sink_ref, x1_ref, q_ref, kv_ref, f_ref, gate_ref, bucket_ref,
                  dftc_ref, dfts_ref, wa_ref, wb_ref, wout_ref, out_ref,
                  bias_ref, kpad_ref, vt_ref, frev_ref, uw_ref, o_ref, s_ref, p_ref,
                  fmix_ref, yb_ref):
    b = pl.program_id(0)
    t = pl.program_id(1)

    @pl.when((b == 0) & (t == 0))
    def _build_bias():
        bucket = bucket_ref[...]
        kj = lax.broadcasted_iota(jnp.int32, (3 * BLK, BLK), 0)
        qi = lax.broadcasted_iota(jnp.int32, (3 * BLK, BLK), 1)
        in_window = jnp.abs(kj - BLK - qi) <= WINDOW
        valid = (in_window, in_window & (kj >= BLK), in_window & (kj < 2 * BLK))
        for h in range(N_HEADS):
            def pick(bk, acc, h=h):
                return jnp.where(bucket == bk, relb_ref[bk, h] * LOG2E, acc)
            base = lax.fori_loop(0, N_BUCKETS, pick, jnp.zeros((3 * BLK, BLK), F32))
            for v in range(3):
                bias_ref[v, h] = jnp.where(valid[v], base, -jnp.inf)

    @pl.when(t == 0)
    def _per_sequence():
        kpad_ref[0:BLK, :] = jnp.zeros((BLK, KV_WIDTH), BF16)
        kpad_ref[BLK:BLK + SEQ, :] = kv_ref[:, :KV_WIDTH]
        kpad_ref[BLK + SEQ:, :] = jnp.zeros((BLK, KV_WIDTH), BF16)
        vt_ref[:, 0:BLK] = jnp.zeros((N_KV * VT_ROWS, BLK), BF16)
        vt_ref[:, BLK + SEQ:] = jnp.zeros((N_KV * VT_ROWS, BLK), BF16)
        for c in range(SEQ // BLK):
            v_t = kv_ref[c * BLK:(c + 1) * BLK, KV_WIDTH:].T
            for kh in range(N_KV):
                vt_ref[kh * VT_ROWS:kh * VT_ROWS + HD, (c + 1) * BLK:(c + 2) * BLK] = (
                    v_t[kh * HD:(kh + 1) * HD])
        for kh in range(N_KV):
            vt_ref[kh * VT_ROWS + HD:(kh + 1) * VT_ROWS, :] = jnp.ones(
                (VT_ROWS - HD, SEQ + 2 * BLK), BF16)
        jj = lax.broadcasted_iota(jnp.int32, (BLK, 2 * BLK), 0)
        cc = lax.broadcasted_iota(jnp.int32, (BLK, 2 * BLK), 1)
        flip = jnp.where(jj + cc == BLK, 1.0, 0.0).astype(BF16)
        for blk in range(HALF_SEQ // BLK):
            src = SEQ - (blk + 1) * BLK
            if blk == 0:
                rev = _dot(flip[:, :BLK], f_ref[src:src + BLK, :])
            else:
                rev = _dot(flip, f_ref[src:src + 2 * BLK, :])
            frev_ref[blk * BLK:(blk + 1) * BLK, :] = rev.astype(BF16)
        first_row = lax.broadcasted_iota(jnp.int32, (V7X_BF16_SUBLANES, FG_DIM), 0) == 0
        for g in range(N_FGROUPS):
            lo = g * FG_DIM
            both = jnp.concatenate([f_ref[0:HALF_SEQ, lo:lo + FG_DIM],
                                    frev_ref[:, lo:lo + FG_DIM]], axis=1)
            r = _dot(both, dftc_ref[...])
            uw_ref[0:HALF_SEQ, lo:lo + FG_DIM] = r[:, :FG_DIM].astype(BF16)
            uw_ref[HALF_SEQ:, lo:lo + FG_DIM] = r[:, FG_DIM:].astype(BF16)
            mid = _dot(f_ref[HALF_SEQ:HALF_SEQ + V7X_BF16_SUBLANES, lo:lo + FG_DIM],
                       dftc_ref[0:FG_DIM, 0:FG_DIM])
            head = uw_ref[HALF_SEQ:HALF_SEQ + V7X_BF16_SUBLANES, lo:lo + FG_DIM]
            uw_ref[HALF_SEQ:HALF_SEQ + V7X_BF16_SUBLANES, lo:lo + FG_DIM] = jnp.where(
                first_row, mid, head.astype(F32)).astype(BF16)

    def block_keys(i):
        n = t * BLOCKS_PER_TILE + i
        variant = jnp.where(n == 0, 1, jnp.where(n == SEQ // BLK - 1, 2, 0))
        return pl.multiple_of(n * BLK, BLK), variant

    def scores(i):
        key0, _ = block_keys(i)
        q_t = q_ref[i * BLK:(i + 1) * BLK, :].T
        zero = jnp.zeros((HD, BLK), BF16)
        cols = []
        for h in range(N_HEADS):
            q_h = q_t[h * HD:(h + 1) * HD]
            cols.append(jnp.concatenate([q_h, zero] if h < GQ else [zero, q_h], axis=0))
        s_ref[i % 2] = _dot(kpad_ref[pl.ds(key0, 3 * BLK), :], jnp.concatenate(cols, axis=1))

    def softmax(i):
        _, variant = block_keys(i)
        buf = i % 2
        sink_terms = []
        for h in range(N_HEADS):
            sink = sink_ref[h] * LOG2E
            cols = slice(h * BLK, (h + 1) * BLK)
            top = s_ref[buf, 0:KEY_CHUNK, cols] + bias_ref[variant, h, 0:KEY_CHUNK, :]
            for k0 in range(KEY_CHUNK, 3 * BLK, KEY_CHUNK):
                top = jnp.maximum(top, s_ref[buf, k0:k0 + KEY_CHUNK, cols]
                                  + bias_ref[variant, h, k0:k0 + KEY_CHUNK, :])
            m = jnp.maximum(jnp.max(top, axis=0, keepdims=True), sink)
            for k0 in range(0, 3 * BLK, KEY_CHUNK):
                z = (s_ref[buf, k0:k0 + KEY_CHUNK, cols] - m) + bias_ref[variant, h, k0:k0 + KEY_CHUNK, :]
                p_ref[buf, k0:k0 + KEY_CHUNK, cols] = jnp.exp2(z).astype(BF16)
            sink_terms.append(jnp.exp2(sink - m))
        return sink_terms

    def weighted_values(i, sink_terms):
        key0, _ = block_keys(i)
        for kh in range(N_KV):
            o_t = _dot(vt_ref[kh * VT_ROWS:(kh + 1) * VT_ROWS, pl.ds(key0, 3 * BLK)],
                       p_ref[i % 2, :, kh * GQ * BLK:(kh + 1) * GQ * BLK])
            den = o_t[HD:HD + 1] + jnp.concatenate(sink_terms[kh * GQ:(kh + 1) * GQ], axis=1)
            o_n = o_t[:HD] / den
            for pr in range(GQ // 2):
                pair_t = jnp.concatenate([o_n[:, 2 * pr * BLK:(2 * pr + 1) * BLK],
                                          o_n[:, (2 * pr + 1) * BLK:(2 * pr + 2) * BLK]], axis=0)
                lane0 = (kh * GQ + 2 * pr) * HD
                o_ref[i * BLK:(i + 1) * BLK, lane0:lane0 + 2 * HD] = pair_t.T.astype(BF16)

    half_tile = ROW_TILE // 2
    scores(0)
    for i in range(BLOCKS_PER_TILE):
        if i + 1 < BLOCKS_PER_TILE:
            scores(i + 1)
        if i < 2:
            rows = slice(i * half_tile, (i + 1) * half_tile)
            freq0 = pl.multiple_of(t * ROW_TILE + i * half_tile, half_tile)
            fmix_ref[rows, :] = _dot(dfts_ref[pl.ds(freq0, half_tile), :], uw_ref[...]).astype(BF16)
        elif i == 2:
            yb_ref[...] = _dot(fmix_ref[...], wb_ref[...])
        weighted_values(i, softmax(i))

    y_a = _dot(o_ref[...], wa_ref[...])
    mix = (gate_ref[:, :D_MODEL].astype(F32) * y_a + gate_ref[:, D_MODEL:].astype(F32) * yb_ref[...])
    out_ref[...] = x1_ref[...] + _dot(mix.astype(BF16), wout_ref[...])


def _resident(shape):
    return pl.BlockSpec(shape, lambda *_: (0,) * len(shape), pipeline_mode=pl.Buffered(1))


def _rows(width):
    return pl.BlockSpec((ROW_TILE, width), lambda i: (i, 0))


def _ffn_proj(x, g1, wup, wdn, gmix, win, bg):
    return pl.pallas_call(
        _ffn_proj_kernel,
        grid=(TOKENS // ROW_TILE,),
        in_specs=[_rows(D_MODEL), _resident((1, D_MODEL)), _resident((D_MODEL, 2 * D_FF)),
                  _resident((D_FF, D_MODEL)), _resident((1, D_MODEL)),
                  _resident((D_MODEL, IN_WIDTH)), _resident((1, 2 * D_MODEL))],
        out_specs=[_rows(D_MODEL), _rows(Q_WIDTH), _rows(2 * KV_WIDTH), _rows(F_WIDTH),
                   _rows(2 * D_MODEL)],
        out_shape=[jax.ShapeDtypeStruct((TOKENS, D_MODEL), F32),
                   jax.ShapeDtypeStruct((TOKENS, Q_WIDTH), BF16),
                   jax.ShapeDtypeStruct((TOKENS, 2 * KV_WIDTH), BF16),
                   jax.ShapeDtypeStruct((TOKENS, F_WIDTH), BF16),
                   jax.ShapeDtypeStruct((TOKENS, 2 * D_MODEL), BF16)],
        scratch_shapes=[pltpu.VMEM((ROW_TILE, D_FF), BF16)],
        compiler_params=pltpu.CompilerParams(dimension_semantics=("arbitrary",),
                                             vmem_limit_bytes=V7X_VMEM_LIMIT_BYTES),
        name="ffn_proj",
    )(x, g1, wup, wdn, gmix, win, bg)


def _ffn_final(x, g2, wup, wdn, gfin):
    return pl.pallas_call(
        _ffn_final_kernel,
        grid=(TOKENS // ROW_TILE,),
        in_specs=[_rows(D_MODEL), _resident((1, D_MODEL)), _resident((D_MODEL, 2 * D_FF)),
                  _resident((D_FF, D_MODEL)), _resident((1, D_MODEL))],
        out_specs=_rows(D_MODEL),
        out_shape=jax.ShapeDtypeStruct((TOKENS, D_MODEL), F32),
        scratch_shapes=[pltpu.VMEM((ROW_TILE, D_FF), BF16)],
        compiler_params=pltpu.CompilerParams(dimension_semantics=("arbitrary",),
                                             vmem_limit_bytes=V7X_VMEM_LIMIT_BYTES),
        name="ffn_final",
    )(x, g2, wup, wdn, gfin)


def _mixer(rel_bias, sink, x1, q, kv, f, gates, bucket, dftc, dfts, wa, wb, wout):
    def tile_rows(width):
        return pl.BlockSpec((ROW_TILE, width), lambda b, t: (b * TILES_PER_SEQ + t, 0))

    def seq_rows(width):
        return pl.BlockSpec((SEQ, width), lambda b, t: (b, 0))

    smem = pl.BlockSpec(memory_space=pltpu.SMEM)
    return pl.pallas_call(
        _mixer_kernel,
        grid=(BATCH, TILES_PER_SEQ),
        in_specs=[smem, smem, tile_rows(D_MODEL), tile_rows(Q_WIDTH), seq_rows(2 * KV_WIDTH),
                  seq_rows(F_WIDTH), tile_rows(2 * D_MODEL), _resident((3 * BLK, BLK)),
                  _resident((2 * FG_DIM, 2 * FG_DIM)),
                  _resident((SEQ, SEQ)),
                  _resident((Q_WIDTH, D_MODEL)), _resident((F_WIDTH, D_MODEL)),
                  _resident((D_MODEL, D_MODEL))],
        out_specs=tile_rows(D_MODEL),
        out_shape=jax.ShapeDtypeStruct((TOKENS, D_MODEL), F32),
        scratch_shapes=[pltpu.VMEM((3, N_HEADS, 3 * BLK, BLK), F32),
                        pltpu.VMEM((SEQ + 2 * BLK, KV_WIDTH), BF16),
                        pltpu.VMEM((N_KV * VT_ROWS, SEQ + 2 * BLK), BF16),
                        pltpu.VMEM((HALF_SEQ, F_WIDTH), BF16),
                        pltpu.VMEM((SEQ, F_WIDTH), BF16),
                        pltpu.VMEM((ROW_TILE, Q_WIDTH), BF16),
                        pltpu.VMEM((2, 3 * BLK, N_HEADS * BLK), F32),
                        pltpu.VMEM((2, 3 * BLK, N_HEADS * BLK), BF16),
                        pltpu.VMEM((ROW_TILE, F_WIDTH), BF16),
                        pltpu.VMEM((ROW_TILE, D_MODEL), F32)],
        compiler_params=pltpu.CompilerParams(dimension_semantics=("arbitrary", "arbitrary"),
                                             vmem_limit_bytes=V7X_VMEM_LIMIT_BYTES),
        name="mixer",
    )(rel_bias, sink, x1, q, kv, f, gates, bucket, dftc, dfts, wa, wb, wout)


def kernel(x, g_ffn1, w_up1, w_down1, g_mix, w_in, b_gate, sink, rel_bias, w_branch_a, w_branch_b, w_out, g_ffn2, w_up2, w_down2, g_final):
    assert x.shape == (BATCH, SEQ, D_MODEL) and w_up1.shape[0] == DEPTH == 1
    chan, pos = _dft_tables()
    bucket = jnp.asarray(_t5_bucket_table().T)
    dftc = jnp.asarray(chan).astype(BF16)
    dfts = jnp.asarray(pos).astype(BF16)
    row = lambda v: v.reshape(1, -1)
    x0 = x.reshape(TOKENS, D_MODEL)
    x1, q, kv, f, gates = _ffn_proj(
        x0, row(g_ffn1[0]), w_up1[0].astype(BF16), w_down1[0].astype(BF16), row(g_mix[0]),
        w_in[0].astype(BF16), row(b_gate[0]))
    x2 = _mixer(rel_bias, sink[0], x1, q, kv, f, gates, bucket, dftc, dfts,
                w_branch_a[0].astype(BF16), w_branch_b[0].astype(BF16), w_out[0].astype(BF16))
    out = _ffn_final(x2, row(g_ffn2[0]), w_up2[0].astype(BF16), w_down2[0].astype(BF16),
                     row(g_final))
    return out.reshape(BATCH, SEQ, D_MODEL)
```

```python
import functools

import numpy as np
import jax
import jax.numpy as jnp
from jax import lax
from jax.experimental import pallas as pl
from jax.experimental.pallas import tpu as pltpu

D_MODEL = 1024
BATCH = 8
SEQ = 2048
DEPTH = 1
N_HEADS = 8
N_KV = 2
GQ = N_HEADS // N_KV
HD = 64
Q_WIDTH = N_HEADS * HD
KV_WIDTH = N_KV * HD
WINDOW = 128
BLK = 128
N_FGROUPS = 4
FG_DIM = 128
F_WIDTH = N_FGROUPS * FG_DIM
N_BUCKETS = 32
MAX_DIST = 128
D_FF = 2816
EPS = 1e-6
O_K = Q_WIDTH
O_V = O_K + KV_WIDTH
O_F = O_V + KV_WIDTH
O_G = O_F + F_WIDTH
IN_WIDTH = O_G + 2 * D_MODEL

F32 = jnp.float32
BF16 = jnp.bfloat16

V7X_MXU_DIM = 256
V7X_BF16_SUBLANES = 16
V7X_VMEM_LIMIT_BYTES = 56 * 1024 * 1024

TOKENS = BATCH * SEQ
HALF_SEQ = SEQ // 2
ROW_TILE = 512
TILES_PER_SEQ = SEQ // ROW_TILE
BLOCKS_PER_TILE = ROW_TILE // BLK
FF_CHUNK = V7X_MXU_DIM
VT_ROWS = HD + V7X_BF16_SUBLANES
N_LATER_WEIGHTS = 5
KEY_CHUNK = 64
LOG2E = float(np.log2(np.e))


def _t5_bucket_table():
    rel = (np.arange(3 * BLK)[None, :] - BLK) - np.arange(BLK)[:, None]
    half = N_BUCKETS // 2
    max_exact = half // 2
    ret = (rel > 0).astype(np.int32) * half
    n = np.abs(rel)
    n_safe = np.maximum(n, 1).astype(np.float32)
    large = max_exact + (np.log(n_safe / max_exact) / np.log(MAX_DIST / max_exact)
                         * (half - max_exact)).astype(np.int32)
    large = np.minimum(large, half - 1)
    return (ret + np.where(n < max_exact, n, large)).astype(np.int32)


@functools.lru_cache(maxsize=None)
def _dft_tables():
    def cos_sin(n):
        idx = np.arange(n)
        ang = 2.0 * np.pi * ((idx[:, None] * idx[None, :]) % n).astype(np.float64) / n
        return np.cos(ang) / np.sqrt(n), np.sin(ang) / np.sqrt(n)
    cc, sc = cos_sin(FG_DIM)
    cs, ss = cos_sin(SEQ)
    chan = np.block([[cc, sc], [cc, -sc]]).astype(np.float32)
    neg_sin = -ss[:, :HALF_SEQ]
    neg_sin[:, 0] = np.where(np.arange(SEQ) % 2 == 0, 1.0, -1.0) / np.sqrt(SEQ)
    pos = np.concatenate([cs[:, :HALF_SEQ], neg_sin], axis=1).astype(np.float32)
    return chan, pos


def _rms(x, g):
    return x * lax.rsqrt(jnp.mean(x * x, axis=-1, keepdims=True) + EPS) * g


def _sigmoid(z):
    return 1.0 / (1.0 + jnp.exp(-z))


def _dot(a, b):
    return jnp.dot(a, b, preferred_element_type=F32)


def _swiglu(h, wup_ref, wdn_ref, act_ref):
    for c in range(D_FF // FF_CHUNK):
        lo = c * FF_CHUNK
        gate = _dot(h, wup_ref[:, lo:lo + FF_CHUNK])
        up = _dot(h, wup_ref[:, D_FF + lo:D_FF + lo + FF_CHUNK])
        act_ref[:, lo:lo + FF_CHUNK] = (gate * _sigmoid(gate) * up).astype(BF16)
    return _dot(act_ref[...], wdn_ref[...])


def _ffn_proj_kernel(x_ref, g1_ref, wup_ref, wdn_ref, gmix_ref, win_ref, bg_ref, *refs):
    later_f32 = refs[:N_LATER_WEIGHTS]
    x1_ref, q_ref, kv_ref, f_ref, gate_ref = refs[N_LATER_WEIGHTS:N_LATER_WEIGHTS + 5]
    later_bf16 = refs[N_LATER_WEIGHTS + 5:2 * N_LATER_WEIGHTS + 5]
    act_ref = refs[2 * N_LATER_WEIGHTS + 5]
    for src, dst in zip(later_f32, later_bf16):
        dst[...] = src[...].astype(BF16)
    x = x_ref[...]
    h = _rms(x, g1_ref[...]).astype(BF16)
    x1 = x + 0.5 * _swiglu(h, wup_ref, wdn_ref, act_ref)
    x1_ref[...] = x1
    h2 = _rms(x1, gmix_ref[...]).astype(BF16)
    q_ref[...] = (_dot(h2, win_ref[:, 0:O_K]) * (HD ** -0.5 * LOG2E)).astype(BF16)
    kv_ref[...] = _dot(h2, win_ref[:, O_K:O_F]).astype(BF16)
    f_ref[...] = _dot(h2, win_ref[:, O_F:O_G]).astype(BF16)
    for c in range(2):
        lo = c * D_MODEL
        z = _dot(h2, win_ref[:, O_G + lo:O_G + lo + D_MODEL]) + bg_ref[:, lo:lo + D_MODEL]
        gate_ref[:, lo:lo + D_MODEL] = _sigmoid(z).astype(BF16)


def _ffn_final_kernel(x_ref, g2_ref, wup_ref, wdn_ref, gfin_ref, out_ref, act_ref):
    x = x_ref[...]
    h = _rms(x, g2_ref[...]).astype(BF16)
    x3 = x + 0.5 * _swiglu(h, wup_ref, wdn_ref, act_ref)
    out_ref[...] = _rms(x3, gfin_ref[...])


def _mixer_kernel(relb_ref, sink_ref, x1_ref, q_ref, kv_ref, f_ref, gate_ref, bucket_ref,
                  dftc_ref, dfts_ref, wa_ref, wb_ref, wout_ref, out_ref,
                  bias_ref, kpad_ref, vt_ref, frev_ref, uw_ref, o_ref, s_ref, p_ref,
                  fmix_ref, yb_ref):
    b = pl.program_id(0)
    t = pl.program_id(1)

    @pl.when((b == 0) & (t == 0))
    def _build_bias():
        bucket = bucket_ref[...]
        kj = lax.broadcasted_iota(jnp.int32, (3 * BLK, BLK), 0)
        qi = lax.broadcasted_iota(jnp.int32, (3 * BLK, BLK), 1)
        in_window = jnp.abs(kj - BLK - qi) <= WINDOW
        valid = (in_window, in_window & (kj >= BLK), in_window & (kj < 2 * BLK))
        for h in range(N_HEADS):
            def pick(bk, acc, h=h):
                return jnp.where(bucket == bk, relb_ref[bk, h] * LOG2E, acc)
            base = lax.fori_loop(0, N_BUCKETS, pick, jnp.zeros((3 * BLK, BLK), F32))
            for v in range(3):
                bias_ref[v, h] = jnp.where(valid[v], base, -jnp.inf)

    @pl.when(t == 0)
    def _per_sequence():
        kpad_ref[0:BLK, :] = jnp.zeros((BLK, KV_WIDTH), BF16)
        kpad_ref[BLK:BLK + SEQ, :] = kv_ref[:, :KV_WIDTH]
        kpad_ref[BLK + SEQ:, :] = jnp.zeros((BLK, KV_WIDTH), BF16)
        vt_ref[:, 0:BLK] = jnp.zeros((N_KV * VT_ROWS, BLK), BF16)
        vt_ref[:, BLK + SEQ:] = jnp.zeros((N_KV * VT_ROWS, BLK), BF16)
        for c in range(SEQ // BLK):
            v_t = kv_ref[c * BLK:(c + 1) * BLK, KV_WIDTH:].T
            for kh in range(N_KV):
                vt_ref[kh * VT_ROWS:kh * VT_ROWS + HD, (c + 1) * BLK:(c + 2) * BLK] = (
                    v_t[kh * HD:(kh + 1) * HD])
        for kh in range(N_KV):
            vt_ref[kh * VT_ROWS + HD:(kh + 1) * VT_ROWS, :] = jnp.ones(
                (VT_ROWS - HD, SEQ + 2 * BLK), BF16)
        jj = lax.broadcasted_iota(jnp.int32, (BLK, 2 * BLK), 0)
        cc = lax.broadcasted_iota(jnp.int32, (BLK, 2 * BLK), 1)
        flip = jnp.where(jj + cc == BLK, 1.0, 0.0).astype(BF16)
        for blk in range(HALF_SEQ // BLK):
            src = SEQ - (blk + 1) * BLK
            if blk == 0:
                rev = _dot(flip[:, :BLK], f_ref[src:src + BLK, :])
            else:
                rev = _dot(flip, f_ref[src:src + 2 * BLK, :])
            frev_ref[blk * BLK:(blk + 1) * BLK, :] = rev.astype(BF16)
        first_row = lax.broadcasted_iota(jnp.int32, (V7X_BF16_SUBLANES, FG_DIM), 0) == 0
        for g in range(N_FGROUPS):
            lo = g * FG_DIM
            both = jnp.concatenate([f_ref[0:HALF_SEQ, lo:lo + FG_DIM],
                                    frev_ref[:, lo:lo + FG_DIM]], axis=1)
            r = _dot(both, dftc_ref[...])
            uw_ref[0:HALF_SEQ, lo:lo + FG_DIM] = r[:, :FG_DIM].astype(BF16)
            uw_ref[HALF_SEQ:, lo:lo + FG_DIM] = r[:, FG_DIM:].astype(BF16)
            mid = _dot(f_ref[HALF_SEQ:HALF_SEQ + V7X_BF16_SUBLANES, lo:lo + FG_DIM],
                       dftc_ref[0:FG_DIM, 0:FG_DIM])
            head = uw_ref[HALF_SEQ:HALF_SEQ + V7X_BF16_SUBLANES, lo:lo + FG_DIM]
            uw_ref[HALF_SEQ:HALF_SEQ + V7X_BF16_SUBLANES, lo:lo + FG_DIM] = jnp.where(
                first_row, mid, head.astype(F32)).astype(BF16)

    def block_keys(i):
        n = t * BLOCKS_PER_TILE + i
        variant = jnp.where(n == 0, 1, jnp.where(n == SEQ // BLK - 1, 2, 0))
        return pl.multiple_of(n * BLK, BLK), variant

    def scores(i):
        key0, _ = block_keys(i)
        q_t = q_ref[i * BLK:(i + 1) * BLK, :].T
        zero = jnp.zeros((HD, BLK), BF16)
        cols = []
        for h in range(N_HEADS):
            q_h = q_t[h * HD:(h + 1) * HD]
            cols.append(jnp.concatenate([q_h, zero] if h < GQ else [zero, q_h], axis=0))
        s_ref[i % 2] = _dot(kpad_ref[pl.ds(key0, 3 * BLK), :], jnp.concatenate(cols, axis=1))

    def softmax(i):
        _, variant = block_keys(i)
        buf = i % 2
        sink_terms = []
        for h in range(N_HEADS):
            sink = sink_ref[h] * LOG2E
            cols = slice(h * BLK, (h + 1) * BLK)
            top = s_ref[buf, 0:KEY_CHUNK, cols] + bias_ref[variant, h, 0:KEY_CHUNK, :]
            for k0 in range(KEY_CHUNK, 3 * BLK, KEY_CHUNK):
                top = jnp.maximum(top, s_ref[buf, k0:k0 + KEY_CHUNK, cols]
                                  + bias_ref[variant, h, k0:k0 + KEY_CHUNK, :])
            m = jnp.maximum(jnp.max(top, axis=0, keepdims=True), sink)
            for k0 in range(0, 3 * BLK, KEY_CHUNK):
                z = (s_ref[buf, k0:k0 + KEY_CHUNK, cols] - m) + bias_ref[variant, h, k0:k0 + KEY_CHUNK, :]
                p_ref[buf, k0:k0 + KEY_CHUNK, cols] = jnp.exp2(z).astype(BF16)
            sink_terms.append(jnp.exp2(sink - m))
        return sink_terms

    def weighted_values(i, sink_terms):
        key0, _ = block_keys(i)
        for kh in range(N_KV):
            o_t = _dot(vt_ref[kh * VT_ROWS:(kh + 1) * VT_ROWS, pl.ds(key0, 3 * BLK)],
                       p_ref[i % 2, :, kh * GQ * BLK:(kh + 1) * GQ * BLK])
            den = o_t[HD:HD + 1] + jnp.concatenate(sink_terms[kh * GQ:(kh + 1) * GQ], axis=1)
            o_n = o_t[:HD] / den
            for pr in range(GQ // 2):
                pair_t = jnp.concatenate([o_n[:, 2 * pr * BLK:(2 * pr + 1) * BLK],
                                          o_n[:, (2 * pr + 1) * BLK:(2 * pr + 2) * BLK]], axis=0)
                lane0 = (kh * GQ + 2 * pr) * HD
                o_ref[i * BLK:(i + 1) * BLK, lane0:lane0 + 2 * HD] = pair_t.T.astype(BF16)

    half_tile = ROW_TILE // 2
    scores(0)
    for i in range(BLOCKS_PER_TILE):
        if i + 1 < BLOCKS_PER_TILE:
            scores(i + 1)
        if i < 2:
            rows = slice(i * half_tile, (i + 1) * half_tile)
            freq0 = pl.multiple_of(t * ROW_TILE + i * half_tile, half_tile)
            fmix_ref[rows, :] = _dot(dfts_ref[pl.ds(freq0, half_tile), :], uw_ref[...]).astype(BF16)
        elif i == 2:
            yb_ref[...] = _dot(fmix_ref[...], wb_ref[...])
        weighted_values(i, softmax(i))

    y_a = _dot(o_ref[...], wa_ref[...])
    mix = (gate_ref[:, :D_MODEL].astype(F32) * y_a + gate_ref[:, D_MODEL:].astype(F32) * yb_ref[...])
    out_ref[...] = x1_ref[...] + _dot(mix.astype(BF16), wout_ref[...])


def _resident(shape):
    return pl.BlockSpec(shape, lambda *_: (0,) * len(shape), pipeline_mode=pl.Buffered(1))


def _rows(width):
    return pl.BlockSpec((ROW_TILE, width), lambda i: (i, 0))


def _slab_spec(shape):
    rows, cols = shape
    steps = TOKENS // ROW_TILE
    slab = next(r for r in range(V7X_BF16_SUBLANES, rows + 1, V7X_BF16_SUBLANES)
                if rows % r == 0 and r * steps >= rows)
    return pl.BlockSpec((slab, cols), lambda i: (jnp.minimum(i, rows // slab - 1), 0))


def _ffn_proj(x, g1, wup, wdn, gmix, win, bg, later_weights):
    assert len(later_weights) == N_LATER_WEIGHTS
    slabs = [_slab_spec(w.shape) for w in later_weights]
    return pl.pallas_call(
        _ffn_proj_kernel,
        grid=(TOKENS // ROW_TILE,),
        in_specs=[_rows(D_MODEL), _resident((1, D_MODEL)), _resident((D_MODEL, 2 * D_FF)),
                  _resident((D_FF, D_MODEL)), _resident((1, D_MODEL)),
                  _resident((D_MODEL, IN_WIDTH)), _resident((1, 2 * D_MODEL))] + slabs,
        out_specs=[_rows(D_MODEL), _rows(Q_WIDTH), _rows(2 * KV_WIDTH), _rows(F_WIDTH),
                   _rows(2 * D_MODEL)] + slabs,
        out_shape=[jax.ShapeDtypeStruct((TOKENS, D_MODEL), F32),
                   jax.ShapeDtypeStruct((TOKENS, Q_WIDTH), BF16),
                   jax.ShapeDtypeStruct((TOKENS, 2 * KV_WIDTH), BF16),
                   jax.ShapeDtypeStruct((TOKENS, F_WIDTH), BF16),
                   jax.ShapeDtypeStruct((TOKENS, 2 * D_MODEL), BF16)]
        + [jax.ShapeDtypeStruct(w.shape, BF16) for w in later_weights],
        scratch_shapes=[pltpu.VMEM((ROW_TILE, D_FF), BF16)],
        compiler_params=pltpu.CompilerParams(dimension_semantics=("arbitrary",),
                                             vmem_limit_bytes=V7X_VMEM_LIMIT_BYTES),
        name="ffn_proj",
    )(x, g1, wup, wdn, gmix, win, bg, *later_weights)


def _ffn_final(x, g2, wup, wdn, gfin):
    return pl.pallas_call(
        _ffn_final_kernel,
        grid=(TOKENS // ROW_TILE,),
        in_specs=[_rows(D_MODEL), _resident((1, D_MODEL)), _resident((D_MODEL, 2 * D_FF)),
                  _resident((D_FF, D_MODEL)), _resident((1, D_MODEL))],
        out_specs=_rows(D_MODEL),
        out_shape=jax.ShapeDtypeStruct((TOKENS, D_MODEL), F32),
        scratch_shapes=[pltpu.VMEM((ROW_TILE, D_FF), BF16)],
        compiler_params=pltpu.CompilerParams(dimension_semantics=("arbitrary",),
                                             vmem_limit_bytes=V7X_VMEM_LIMIT_BYTES),
        name="ffn_final",
    )(x, g2, wup, wdn, gfin)


def _mixer(rel_bias, sink, x1, q, kv, f, gates, bucket, dftc, dfts, wa, wb, wout):
    def tile_rows(width):
        return pl.BlockSpec((ROW_TILE, width), lambda b, t: (b * TILES_PER_SEQ + t, 0))

    def seq_rows(width):
        return pl.BlockSpec((SEQ, width), lambda b, t: (b, 0))

    smem = pl.BlockSpec(memory_space=pltpu.SMEM)
    return pl.pallas_call(
        _mixer_kernel,
        grid=(BATCH, TILES_PER_SEQ),
        in_specs=[smem, smem, tile_rows(D_MODEL), tile_rows(Q_WIDTH), seq_rows(2 * KV_WIDTH),
                  seq_rows(F_WIDTH), tile_rows(2 * D_MODEL), _resident((3 * BLK, BLK)),
                  _resident((2 * FG_DIM, 2 * FG_DIM)),
                  _resident((SEQ, SEQ)),
                  _resident((Q_WIDTH, D_MODEL)), _resident((F_WIDTH, D_MODEL)),
                  _resident((D_MODEL, D_MODEL))],
        out_specs=tile_rows(D_MODEL),
        out_shape=jax.ShapeDtypeStruct((TOKENS, D_MODEL), F32),
        scratch_shapes=[pltpu.VMEM((3, N_HEADS, 3 * BLK, BLK), F32),
                        pltpu.VMEM((SEQ + 2 * BLK, KV_WIDTH), BF16),
                        pltpu.VMEM((N_KV * VT_ROWS, SEQ + 2 * BLK), BF16),
                        pltpu.VMEM((HALF_SEQ, F_WIDTH), BF16),
                        pltpu.VMEM((SEQ, F_WIDTH), BF16),
                        pltpu.VMEM((ROW_TILE, Q_WIDTH), BF16),
                        pltpu.VMEM((2, 3 * BLK, N_HEADS * BLK), F32),
                        pltpu.VMEM((2, 3 * BLK, N_HEADS * BLK), BF16),
                        pltpu.VMEM((ROW_TILE, F_WIDTH), BF16),
                        pltpu.VMEM((ROW_TILE, D_MODEL), F32)],
        compiler_params=pltpu.CompilerParams(dimension_semantics=("arbitrary", "arbitrary"),
                                             vmem_limit_bytes=V7X_VMEM_LIMIT_BYTES),
        name="mixer",
    )(rel_bias, sink, x1, q, kv, f, gates, bucket, dftc, dfts, wa, wb, wout)


def kernel(x, g_ffn1, w_up1, w_down1, g_mix, w_in, b_gate, sink, rel_bias, w_branch_a, w_branch_b, w_out, g_ffn2, w_up2, w_down2, g_final):
    assert x.shape == (BATCH, SEQ, D_MODEL) and w_up1.shape[0] == DEPTH == 1
    chan, pos = _dft_tables()
    bucket = jnp.asarray(_t5_bucket_table().T)
    dftc = jnp.asarray(chan).astype(BF16)
    dfts = jnp.asarray(pos).astype(BF16)
    row = lambda v: v.reshape(1, -1)
    x0 = x.reshape(TOKENS, D_MODEL)
    x1, q, kv, f, gates, wa, wb, wout, wup2, wdn2 = _ffn_proj(
        x0, row(g_ffn1[0]), w_up1[0].astype(BF16), w_down1[0].astype(BF16), row(g_mix[0]),
        w_in[0].astype(BF16), row(b_gate[0]),
        [w_branch_a[0], w_branch_b[0], w_out[0], w_up2[0], w_down2[0]])
    x2 = _mixer(rel_bias, sink[0], x1, q, kv, f, gates, bucket, dftc, dfts, wa, wb, wout)
    out = _ffn_final(x2, row(g_ffn2[0]), wup2, wdn2, row(g_final))
    return out.reshape(BATCH, SEQ, D_MODEL)
```

```python
import functools

import numpy as np
import jax
import jax.numpy as jnp
from jax import lax
from jax.experimental import pallas as pl
from jax.experimental.pallas import tpu as pltpu

D_MODEL = 1024
BATCH = 8
SEQ = 2048
DEPTH = 1
N_HEADS = 8
N_KV = 2
GQ = N_HEADS // N_KV
HD = 64
Q_WIDTH = N_HEADS * HD
KV_WIDTH = N_KV * HD
WINDOW = 128
BLK = 128
N_FGROUPS = 4
FG_DIM = 128
F_WIDTH = N_FGROUPS * FG_DIM
N_BUCKETS = 32
MAX_DIST = 128
D_FF = 2816
EPS = 1e-6
O_K = Q_WIDTH
O_V = O_K + KV_WIDTH
O_F = O_V + KV_WIDTH
O_G = O_F + F_WIDTH
IN_WIDTH = O_G + 2 * D_MODEL

F32 = jnp.float32
BF16 = jnp.bfloat16

V7X_MXU_DIM = 256
V7X_BF16_SUBLANES = 16
V7X_VMEM_LIMIT_BYTES = 56 * 1024 * 1024

TOKENS = BATCH * SEQ
HALF_SEQ = SEQ // 2
ROW_TILE = 512
TILES_PER_SEQ = SEQ // ROW_TILE
BLOCKS_PER_TILE = ROW_TILE // BLK
FF_CHUNK = V7X_MXU_DIM
VT_ROWS = HD + V7X_BF16_SUBLANES
N_LATER_WEIGHTS = 5
STAGE_ROWS = 64
KEY_CHUNK = 64
LOG2E = float(np.log2(np.e))


def _t5_bucket_table():
    rel = (np.arange(3 * BLK)[None, :] - BLK) - np.arange(BLK)[:, None]
    half = N_BUCKETS // 2
    max_exact = half // 2
    ret = (rel > 0).astype(np.int32) * half
    n = np.abs(rel)
    n_safe = np.maximum(n, 1).astype(np.float32)
    large = max_exact + (np.log(n_safe / max_exact) / np.log(MAX_DIST / max_exact)
                         * (half - max_exact)).astype(np.int32)
    large = np.minimum(large, half - 1)
    return (ret + np.where(n < max_exact, n, large)).astype(np.int32)


@functools.lru_cache(maxsize=None)
def _dft_tables():
    def cos_sin(n):
        idx = np.arange(n)
        ang = 2.0 * np.pi * ((idx[:, None] * idx[None, :]) % n).astype(np.float64) / n
        return np.cos(ang) / np.sqrt(n), np.sin(ang) / np.sqrt(n)
    cc, sc = cos_sin(FG_DIM)
    cs, ss = cos_sin(SEQ)
    chan = np.block([[cc, sc], [cc, -sc]]).astype(np.float32)
    neg_sin = -ss[:, :HALF_SEQ]
    neg_sin[:, 0] = np.where(np.arange(SEQ) % 2 == 0, 1.0, -1.0) / np.sqrt(SEQ)
    pos = np.concatenate([cs[:, :HALF_SEQ], neg_sin], axis=1).astype(np.float32)
    return chan, pos


def _rms(x, g):
    return x * lax.rsqrt(jnp.mean(x * x, axis=-1, keepdims=True) + EPS) * g


def _sigmoid(z):
    return 1.0 / (1.0 + jnp.exp(-z))


def _dot(a, b):
    return jnp.dot(a, b, preferred_element_type=F32)


def _swiglu(h, wup_ref, wdn_ref, act_ref):
    for c in range(D_FF // FF_CHUNK):
        lo = c * FF_CHUNK
        gate = _dot(h, wup_ref[:, lo:lo + FF_CHUNK])
        up = _dot(h, wup_ref[:, D_FF + lo:D_FF + lo + FF_CHUNK])
        act_ref[:, lo:lo + FF_CHUNK] = (gate * _sigmoid(gate) * up).astype(BF16)
    return _dot(act_ref[...], wdn_ref[...])


def _stage_weight(src_hbm, dst_ref, stage_ref, sem):
    rows, cols = dst_ref.shape
    n_slabs = rows // STAGE_ROWS

    def slab_copy(c, slot):
        return pltpu.make_async_copy(src_hbm.at[pl.ds(c * STAGE_ROWS, STAGE_ROWS), :],
                                     stage_ref.at[slot, :, pl.ds(0, cols)], sem.at[slot])

    slab_copy(0, 0).start()

    def step(c, carry):
        slot = c % 2

        @pl.when(c + 1 < n_slabs)
        def _start_next():
            slab_copy(c + 1, 1 - slot).start()

        slab_copy(c, slot).wait()
        row0 = pl.multiple_of(c * STAGE_ROWS, STAGE_ROWS)
        dst_ref[pl.ds(row0, STAGE_ROWS), :] = stage_ref[slot, :, pl.ds(0, cols)].astype(BF16)
        return carry

    lax.fori_loop(0, n_slabs, step, 0)


def _ffn_proj_kernel(x_ref, g1_ref, wup_hbm, wdn_hbm, gmix_ref, win_hbm, bg_ref, *refs):
    later_f32 = refs[:N_LATER_WEIGHTS]
    x1_ref, q_ref, kv_ref, f_ref, gate_ref = refs[N_LATER_WEIGHTS:N_LATER_WEIGHTS + 5]
    later_bf16 = refs[N_LATER_WEIGHTS + 5:2 * N_LATER_WEIGHTS + 5]
    act_ref, wup_ref, wdn_ref, win_ref, stage_ref, sem = refs[2 * N_LATER_WEIGHTS + 5:]

    @pl.when(pl.program_id(0) == 0)
    def _stage_own_weights():
        _stage_weight(wup_hbm, wup_ref, stage_ref, sem)
        _stage_weight(wdn_hbm, wdn_ref, stage_ref, sem)
        _stage_weight(win_hbm, win_ref, stage_ref, sem)

    for src, dst in zip(later_f32, later_bf16):
        dst[...] = src[...].astype(BF16)
    x = x_ref[...]
    h = _rms(x, g1_ref[...]).astype(BF16)
    x1 = x + 0.5 * _swiglu(h, wup_ref, wdn_ref, act_ref)
    x1_ref[...] = x1
    h2 = _rms(x1, gmix_ref[...]).astype(BF16)
    q_ref[...] = (_dot(h2, win_ref[:, 0:O_K]) * (HD ** -0.5 * LOG2E)).astype(BF16)
    kv_ref[...] = _dot(h2, win_ref[:, O_K:O_F]).astype(BF16)
    f_ref[...] = _dot(h2, win_ref[:, O_F:O_G]).astype(BF16)
    for c in range(2):
        lo = c * D_MODEL
        z = _dot(h2, win_ref[:, O_G + lo:O_G + lo + D_MODEL]) + bg_ref[:, lo:lo + D_MODEL]
        gate_ref[:, lo:lo + D_MODEL] = _sigmoid(z).astype(BF16)


def _ffn_final_kernel(x_ref, g2_ref, wup_ref, wdn_ref, gfin_ref, out_ref, act_ref):
    x = x_ref[...]
    h = _rms(x, g2_ref[...]).astype(BF16)
    x3 = x + 0.5 * _swiglu(h, wup_ref, wdn_ref, act_ref)
    out_ref[...] = _rms(x3, gfin_ref[...])


def _mixer_kernel(relb_ref, sink_ref, x1_ref, q_ref, kv_ref, f_ref, gate_ref, bucket_ref,
                  dftc_ref, dfts_ref, wa_ref, wb_ref, wout_ref, out_ref,
                  bias_ref, kpad_ref, vt_ref, frev_ref, uw_ref, o_ref, s_ref, p_ref,
                  fmix_ref, yb_ref):
    b = pl.program_id(0)
    t = pl.program_id(1)

    @pl.when((b == 0) & (t == 0))
    def _build_bias():
        bucket = bucket_ref[...]
        kj = lax.broadcasted_iota(jnp.int32, (3 * BLK, BLK), 0)
        qi = lax.broadcasted_iota(jnp.int32, (3 * BLK, BLK), 1)
        in_window = jnp.abs(kj - BLK - qi) <= WINDOW
        valid = (in_window, in_window & (kj >= BLK), in_window & (kj < 2 * BLK))
        for h in range(N_HEADS):
            def pick(bk, acc, h=h):
                return jnp.where(bucket == bk, relb_ref[bk, h] * LOG2E, acc)
            base = lax.fori_loop(0, N_BUCKETS, pick, jnp.zeros((3 * BLK, BLK), F32))
            for v in range(3):
                bias_ref[v, h] = jnp.where(valid[v], base, -jnp.inf)

    @pl.when(t == 0)
    def _per_sequence():
        kpad_ref[0:BLK, :] = jnp.zeros((BLK, KV_WIDTH), BF16)
        kpad_ref[BLK:BLK + SEQ, :] = kv_ref[:, :KV_WIDTH]
        kpad_ref[BLK + SEQ:, :] = jnp.zeros((BLK, KV_WIDTH), BF16)
        vt_ref[:, 0:BLK] = jnp.zeros((N_KV * VT_ROWS, BLK), BF16)
        vt_ref[:, BLK + SEQ:] = jnp.zeros((N_KV * VT_ROWS, BLK), BF16)
        for c in range(SEQ // BLK):
            v_t = kv_ref[c * BLK:(c + 1) * BLK, KV_WIDTH:].T
            for kh in range(N_KV):
                vt_ref[kh * VT_ROWS:kh * VT_ROWS + HD, (c + 1) * BLK:(c + 2) * BLK] = (
                    v_t[kh * HD:(kh + 1) * HD])
        for kh in range(N_KV):
            vt_ref[kh * VT_ROWS + HD:(kh + 1) * VT_ROWS, :] = jnp.ones(
                (VT_ROWS - HD, SEQ + 2 * BLK), BF16)
        jj = lax.broadcasted_iota(jnp.int32, (BLK, 2 * BLK), 0)
        cc = lax.broadcasted_iota(jnp.int32, (BLK, 2 * BLK), 1)
        flip = jnp.where(jj + cc == BLK, 1.0, 0.0).astype(BF16)
        for blk in range(HALF_SEQ // BLK):
            src = SEQ - (blk + 1) * BLK
            if blk == 0:
                rev = _dot(flip[:, :BLK], f_ref[src:src + BLK, :])
            else:
                rev = _dot(flip, f_ref[src:src + 2 * BLK, :])
            frev_ref[blk * BLK:(blk + 1) * BLK, :] = rev.astype(BF16)
        first_row = lax.broadcasted_iota(jnp.int32, (V7X_BF16_SUBLANES, FG_DIM), 0) == 0
        for g in range(N_FGROUPS):
            lo = g * FG_DIM
            both = jnp.concatenate([f_ref[0:HALF_SEQ, lo:lo + FG_DIM],
                                    frev_ref[:, lo:lo + FG_DIM]], axis=1)
            r = _dot(both, dftc_ref[...])
            uw_ref[0:HALF_SEQ, lo:lo + FG_DIM] = r[:, :FG_DIM].astype(BF16)
            uw_ref[HALF_SEQ:, lo:lo + FG_DIM] = r[:, FG_DIM:].astype(BF16)
            mid = _dot(f_ref[HALF_SEQ:HALF_SEQ + V7X_BF16_SUBLANES, lo:lo + FG_DIM],
                       dftc_ref[0:FG_DIM, 0:FG_DIM])
            head = uw_ref[HALF_SEQ:HALF_SEQ + V7X_BF16_SUBLANES, lo:lo + FG_DIM]
            uw_ref[HALF_SEQ:HALF_SEQ + V7X_BF16_SUBLANES, lo:lo + FG_DIM] = jnp.where(
                first_row, mid, head.astype(F32)).astype(BF16)

    def block_keys(i):
        n = t * BLOCKS_PER_TILE + i
        variant = jnp.where(n == 0, 1, jnp.where(n == SEQ // BLK - 1, 2, 0))
        return pl.multiple_of(n * BLK, BLK), variant

    def scores(i):
        key0, _ = block_keys(i)
        q_t = q_ref[i * BLK:(i + 1) * BLK, :].T
        zero = jnp.zeros((HD, BLK), BF16)
        cols = []
        for h in range(N_HEADS):
            q_h = q_t[h * HD:(h + 1) * HD]
            cols.append(jnp.concatenate([q_h, zero] if h < GQ else [zero, q_h], axis=0))
        s_ref[i % 2] = _dot(kpad_ref[pl.ds(key0, 3 * BLK), :], jnp.concatenate(cols, axis=1))

    def softmax(i):
        _, variant = block_keys(i)
        buf = i % 2
        sink_terms = []
        for h in range(N_HEADS):
            sink = sink_ref[h] * LOG2E
            cols = slice(h * BLK, (h + 1) * BLK)
            top = s_ref[buf, 0:KEY_CHUNK, cols] + bias_ref[variant, h, 0:KEY_CHUNK, :]
            for k0 in range(KEY_CHUNK, 3 * BLK, KEY_CHUNK):
                top = jnp.maximum(top, s_ref[buf, k0:k0 + KEY_CHUNK, cols]
                                  + bias_ref[variant, h, k0:k0 + KEY_CHUNK, :])
            m = jnp.maximum(jnp.max(top, axis=0, keepdims=True), sink)
            for k0 in range(0, 3 * BLK, KEY_CHUNK):
                z = (s_ref[buf, k0:k0 + KEY_CHUNK, cols] - m) + bias_ref[variant, h, k0:k0 + KEY_CHUNK, :]
                p_ref[buf, k0:k0 + KEY_CHUNK, cols] = jnp.exp2(z).astype(BF16)
            sink_terms.append(jnp.exp2(sink - m))
        return sink_terms

    def weighted_values(i, sink_terms):
        key0, _ = block_keys(i)
        for kh in range(N_KV):
            o_t = _dot(vt_ref[kh * VT_ROWS:(kh + 1) * VT_ROWS, pl.ds(key0, 3 * BLK)],
                       p_ref[i % 2, :, kh * GQ * BLK:(kh + 1) * GQ * BLK])
            den = o_t[HD:HD + 1] + jnp.concatenate(sink_terms[kh * GQ:(kh + 1) * GQ], axis=1)
            o_n = o_t[:HD] / den
            for pr in range(GQ // 2):
                pair_t = jnp.concatenate([o_n[:, 2 * pr * BLK:(2 * pr + 1) * BLK],
                                          o_n[:, (2 * pr + 1) * BLK:(2 * pr + 2) * BLK]], axis=0)
                lane0 = (kh * GQ + 2 * pr) * HD
                o_ref[i * BLK:(i + 1) * BLK, lane0:lane0 + 2 * HD] = pair_t.T.astype(BF16)

    half_tile = ROW_TILE // 2
    scores(0)
    for i in range(BLOCKS_PER_TILE):
        if i + 1 < BLOCKS_PER_TILE:
            scores(i + 1)
        if i < 2:
            rows = slice(i * half_tile, (i + 1) * half_tile)
            freq0 = pl.multiple_of(t * ROW_TILE + i * half_tile, half_tile)
            fmix_ref[rows, :] = _dot(dfts_ref[pl.ds(freq0, half_tile), :], uw_ref[...]).astype(BF16)
        elif i == 2:
            yb_ref[...] = _dot(fmix_ref[...], wb_ref[...])
        weighted_values(i, softmax(i))

    y_a = _dot(o_ref[...], wa_ref[...])
    mix = (gate_ref[:, :D_MODEL].astype(F32) * y_a + gate_ref[:, D_MODEL:].astype(F32) * yb_ref[...])
    out_ref[...] = x1_ref[...] + _dot(mix.astype(BF16), wout_ref[...])


def _resident(shape):
    return pl.BlockSpec(shape, lambda *_: (0,) * len(shape), pipeline_mode=pl.Buffered(1))


def _rows(width):
    return pl.BlockSpec((ROW_TILE, width), lambda i: (i, 0))


def _slab_spec(shape):
    rows, cols = shape
    steps = TOKENS // ROW_TILE
    slab = next(r for r in range(V7X_BF16_SUBLANES, rows + 1, V7X_BF16_SUBLANES)
                if rows % r == 0 and r * steps >= rows)
    return pl.BlockSpec((slab, cols), lambda i: (jnp.minimum(i, rows // slab - 1), 0))


def _ffn_proj(x, g1, wup, wdn, gmix, win, bg, later_weights):
    assert len(later_weights) == N_LATER_WEIGHTS
    slabs = [_slab_spec(w.shape) for w in later_weights]
    hbm = pl.BlockSpec(memory_space=pl.ANY)
    return pl.pallas_call(
        _ffn_proj_kernel,
        grid=(TOKENS // ROW_TILE,),
        in_specs=[_rows(D_MODEL), _resident((1, D_MODEL)), hbm, hbm, _resident((1, D_MODEL)),
                  hbm, _resident((1, 2 * D_MODEL))] + slabs,
        out_specs=[_rows(D_MODEL), _rows(Q_WIDTH), _rows(2 * KV_WIDTH), _rows(F_WIDTH),
                   _rows(2 * D_MODEL)] + slabs,
        out_shape=[jax.ShapeDtypeStruct((TOKENS, D_MODEL), F32),
                   jax.ShapeDtypeStruct((TOKENS, Q_WIDTH), BF16),
                   jax.ShapeDtypeStruct((TOKENS, 2 * KV_WIDTH), BF16),
                   jax.ShapeDtypeStruct((TOKENS, F_WIDTH), BF16),
                   jax.ShapeDtypeStruct((TOKENS, 2 * D_MODEL), BF16)]
        + [jax.ShapeDtypeStruct(w.shape, BF16) for w in later_weights],
        scratch_shapes=[pltpu.VMEM((ROW_TILE, D_FF), BF16),
                        pltpu.VMEM((D_MODEL, 2 * D_FF), BF16),
                        pltpu.VMEM((D_FF, D_MODEL), BF16),
                        pltpu.VMEM((D_MODEL, IN_WIDTH), BF16),
                        pltpu.VMEM((2, STAGE_ROWS, 2 * D_FF), F32),
                        pltpu.SemaphoreType.DMA((2,))],
        compiler_params=pltpu.CompilerParams(dimension_semantics=("arbitrary",),
                                             vmem_limit_bytes=V7X_VMEM_LIMIT_BYTES),
        name="ffn_proj",
    )(x, g1, wup, wdn, gmix, win, bg, *later_weights)


def _ffn_final(x, g2, wup, wdn, gfin):
    return pl.pallas_call(
        _ffn_final_kernel,
        grid=(TOKENS // ROW_TILE,),
        in_specs=[_rows(D_MODEL), _resident((1, D_MODEL)), _resident((D_MODEL, 2 * D_FF)),
                  _resident((D_FF, D_MODEL)), _resident((1, D_MODEL))],
        out_specs=_rows(D_MODEL),
        out_shape=jax.ShapeDtypeStruct((TOKENS, D_MODEL), F32),
        scratch_shapes=[pltpu.VMEM((ROW_TILE, D_FF), BF16)],
        compiler_params=pltpu.CompilerParams(dimension_semantics=("arbitrary",),
                                             vmem_limit_bytes=V7X_VMEM_LIMIT_BYTES),
        name="ffn_final",
    )(x, g2, wup, wdn, gfin)


def _mixer(rel_bias, sink, x1, q, kv, f, gates, bucket, dftc, dfts, wa, wb, wout):
    def tile_rows(width):
        return pl.BlockSpec((ROW_TILE, width), lambda b, t: (b * TILES_PER_SEQ + t, 0))

    def seq_rows(width):
        return pl.BlockSpec((SEQ, width), lambda b, t: (b, 0))

    smem = pl.BlockSpec(memory_space=pltpu.SMEM)
    return pl.pallas_call(
        _mixer_kernel,
        grid=(BATCH, TILES_PER_SEQ),
        in_specs=[smem, smem, tile_rows(D_MODEL), tile_rows(Q_WIDTH), seq_rows(2 * KV_WIDTH),
                  seq_rows(F_WIDTH), tile_rows(2 * D_MODEL), _resident((3 * BLK, BLK)),
                  _resident((2 * FG_DIM, 2 * FG_DIM)),
                  _resident((SEQ, SEQ)),
                  _resident((Q_WIDTH, D_MODEL)), _resident((F_WIDTH, D_MODEL)),
                  _resident((D_MODEL, D_MODEL))],
        out_specs=tile_rows(D_MODEL),
        out_shape=jax.ShapeDtypeStruct((TOKENS, D_MODEL), F32),
        scratch_shapes=[pltpu.VMEM((3, N_HEADS, 3 * BLK, BLK), F32),
                        pltpu.VMEM((SEQ + 2 * BLK, KV_WIDTH), BF16),
                        pltpu.VMEM((N_KV * VT_ROWS, SEQ + 2 * BLK), BF16),
                        pltpu.VMEM((HALF_SEQ, F_WIDTH), BF16),
                        pltpu.VMEM((SEQ, F_WIDTH), BF16),
                        pltpu.VMEM((ROW_TILE, Q_WIDTH), BF16),
                        pltpu.VMEM((2, 3 * BLK, N_HEADS * BLK), F32),
                        pltpu.VMEM((2, 3 * BLK, N_HEADS * BLK), BF16),
                        pltpu.VMEM((ROW_TILE, F_WIDTH), BF16),
                        pltpu.VMEM((ROW_TILE, D_MODEL), F32)],
        compiler_params=pltpu.CompilerParams(dimension_semantics=("arbitrary", "arbitrary"),
                                             vmem_limit_bytes=V7X_VMEM_LIMIT_BYTES),
        name="mixer",
    )(rel_bias, sink, x1, q, kv, f, gates, bucket, dftc, dfts, wa, wb, wout)


def kernel(x, g_ffn1, w_up1, w_down1, g_mix, w_in, b_gate, sink, rel_bias, w_branch_a, w_branch_b, w_out, g_ffn2, w_up2, w_down2, g_final):
    assert x.shape == (BATCH, SEQ, D_MODEL) and w_up1.shape[0] == DEPTH == 1
    chan, pos = _dft_tables()
    bucket = jnp.asarray(_t5_bucket_table().T)
    dftc = jnp.asarray(chan).astype(BF16)
    dfts = jnp.asarray(pos).astype(BF16)
    row = lambda v: v.reshape(1, -1)
    x0 = x.reshape(TOKENS, D_MODEL)
    x1, q, kv, f, gates, wa, wb, wout, wup2, wdn2 = _ffn_proj(
        x0, row(g_ffn1[0]), w_up1[0], w_down1[0], row(g_mix[0]), w_in[0], row(b_gate[0]),
        [w_branch_a[0], w_branch_b[0], w_out[0], w_up2[0], w_down2[0]])
    x2 = _mixer(rel_bias, sink[0], x1, q, kv, f, gates, bucket, dftc, dfts, wa, wb, wout)
    out = _ffn_final(x2, row(g_ffn2[0]), wup2, wdn2, row(g_final))
    return out.reshape(BATCH, SEQ, D_MODEL)
```

```python
import functools

import numpy as np
import jax
import jax.numpy as jnp
from jax import lax
from jax.experimental import pallas as pl
from jax.experimental.pallas import tpu as pltpu

D_MODEL = 1024
BATCH = 8
SEQ = 2048
DEPTH = 1
N_HEADS = 8
N_KV = 2
GQ = N_HEADS // N_KV
HD = 64
Q_WIDTH = N_HEADS * HD
KV_WIDTH = N_KV * HD
WINDOW = 128
BLK = 128
N_FGROUPS = 4
FG_DIM = 128
F_WIDTH = N_FGROUPS * FG_DIM
N_BUCKETS = 32
MAX_DIST = 128
D_FF = 2816
EPS = 1e-6
O_K = Q_WIDTH
O_V = O_K + KV_WIDTH
O_F = O_V + KV_WIDTH
O_G = O_F + F_WIDTH
IN_WIDTH = O_G + 2 * D_MODEL

F32 = jnp.float32
BF16 = jnp.bfloat16

V7X_MXU_DIM = 256
V7X_BF16_SUBLANES = 16
V7X_VMEM_LIMIT_BYTES = 56 * 1024 * 1024

TOKENS = BATCH * SEQ
HALF_SEQ = SEQ // 2
ROW_TILE = 512
TILES_PER_SEQ = SEQ // ROW_TILE
BLOCKS_PER_TILE = ROW_TILE // BLK
FF_CHUNK = V7X_MXU_DIM
VT_ROWS = HD + V7X_BF16_SUBLANES
N_LATER_WEIGHTS = 5
STAGE_SLOTS = 4
STAGE_BYTES = 3 * 512 * 1024
KEY_CHUNK = 64
LOG2E = float(np.log2(np.e))


def _t5_bucket_table():
    rel = (np.arange(3 * BLK)[None, :] - BLK) - np.arange(BLK)[:, None]
    half = N_BUCKETS // 2
    max_exact = half // 2
    ret = (rel > 0).astype(np.int32) * half
    n = np.abs(rel)
    n_safe = np.maximum(n, 1).astype(np.float32)
    large = max_exact + (np.log(n_safe / max_exact) / np.log(MAX_DIST / max_exact)
                         * (half - max_exact)).astype(np.int32)
    large = np.minimum(large, half - 1)
    return (ret + np.where(n < max_exact, n, large)).astype(np.int32)


@functools.lru_cache(maxsize=None)
def _dft_tables():
    def cos_sin(n):
        idx = np.arange(n)
        ang = 2.0 * np.pi * ((idx[:, None] * idx[None, :]) % n).astype(np.float64) / n
        return np.cos(ang) / np.sqrt(n), np.sin(ang) / np.sqrt(n)
    cc, sc = cos_sin(FG_DIM)
    cs, ss = cos_sin(SEQ)
    chan = np.block([[cc, sc], [cc, -sc]]).astype(np.float32)
    neg_sin = -ss[:, :HALF_SEQ]
    neg_sin[:, 0] = np.where(np.arange(SEQ) % 2 == 0, 1.0, -1.0) / np.sqrt(SEQ)
    pos = np.concatenate([cs[:, :HALF_SEQ], neg_sin], axis=1).astype(np.float32)
    return chan, pos


def _rms(x, g):
    return x * lax.rsqrt(jnp.mean(x * x, axis=-1, keepdims=True) + EPS) * g


def _sigmoid(z):
    return 1.0 / (1.0 + jnp.exp(-z))


def _dot(a, b):
    return jnp.dot(a, b, preferred_element_type=F32)


def _swiglu(h, wup_ref, wdn_ref, act_ref):
    for c in range(D_FF // FF_CHUNK):
        lo = c * FF_CHUNK
        gate = _dot(h, wup_ref[:, lo:lo + FF_CHUNK])
        up = _dot(h, wup_ref[:, D_FF + lo:D_FF + lo + FF_CHUNK])
        act_ref[:, lo:lo + FF_CHUNK] = (gate * _sigmoid(gate) * up).astype(BF16)
    return _dot(act_ref[...], wdn_ref[...])


def _stage_weight(src_hbm, dst_ref):
    rows, cols = dst_ref.shape
    slab = max(r for r in range(V7X_BF16_SUBLANES, rows + 1, V7X_BF16_SUBLANES)
               if rows % r == 0 and r * cols * 4 <= STAGE_BYTES)
    n_slabs = rows // slab

    def staged(stage_ref, sem):
        def slab_copy(c, slot):
            return pltpu.make_async_copy(src_hbm.at[pl.ds(c * slab, slab), :],
                                         stage_ref.at[slot], sem.at[slot])

        for c in range(min(STAGE_SLOTS - 1, n_slabs)):
            slab_copy(c, c).start()

        def step(c, carry):
            ahead = c + STAGE_SLOTS - 1

            @pl.when(ahead < n_slabs)
            def _start_ahead():
                slab_copy(ahead, ahead % STAGE_SLOTS).start()

            slot = c % STAGE_SLOTS
            slab_copy(c, slot).wait()
            dst_ref[pl.ds(pl.multiple_of(c * slab, slab), slab), :] = stage_ref[slot].astype(BF16)
            return carry

        lax.fori_loop(0, n_slabs, step, 0)

    pl.run_scoped(staged, pltpu.VMEM((STAGE_SLOTS, slab, cols), F32),
                  pltpu.SemaphoreType.DMA((STAGE_SLOTS,)))


def _ffn_proj_kernel(x_ref, g1_ref, wup_hbm, wdn_hbm, gmix_ref, win_hbm, bg_ref, *refs):
    later_f32 = refs[:N_LATER_WEIGHTS]
    x1_ref, q_ref, kv_ref, f_ref, gate_ref = refs[N_LATER_WEIGHTS:N_LATER_WEIGHTS + 5]
    later_bf16 = refs[N_LATER_WEIGHTS + 5:2 * N_LATER_WEIGHTS + 5]
    act_ref, wup_ref, wdn_ref, win_ref = refs[2 * N_LATER_WEIGHTS + 5:]

    @pl.when(pl.program_id(0) == 0)
    def _stage_own_weights():
        _stage_weight(wup_hbm, wup_ref)
        _stage_weight(wdn_hbm, wdn_ref)
        _stage_weight(win_hbm, win_ref)

    for src, dst in zip(later_f32, later_bf16):
        dst[...] = src[...].astype(BF16)
    x = x_ref[...]
    h = _rms(x, g1_ref[...]).astype(BF16)
    x1 = x + 0.5 * _swiglu(h, wup_ref, wdn_ref, act_ref)
    x1_ref[...] = x1
    h2 = _rms(x1, gmix_ref[...]).astype(BF16)
    q_ref[...] = (_dot(h2, win_ref[:, 0:O_K]) * (HD ** -0.5 * LOG2E)).astype(BF16)
    kv_ref[...] = _dot(h2, win_ref[:, O_K:O_F]).astype(BF16)
    f_ref[...] = _dot(h2, win_ref[:, O_F:O_G]).astype(BF16)
    for c in range(2):
        lo = c * D_MODEL
        z = _dot(h2, win_ref[:, O_G + lo:O_G + lo + D_MODEL]) + bg_ref[:, lo:lo + D_MODEL]
        gate_ref[:, lo:lo + D_MODEL] = _sigmoid(z).astype(BF16)


def _ffn_final_kernel(x_ref, g2_ref, wup_ref, wdn_ref, gfin_ref, out_ref, act_ref):
    x = x_ref[...]
    h = _rms(x, g2_ref[...]).astype(BF16)
    x3 = x + 0.5 * _swiglu(h, wup_ref, wdn_ref, act_ref)
    out_ref[...] = _rms(x3, gfin_ref[...])


def _mixer_kernel(relb_ref, sink_ref, x1_ref, q_ref, kv_ref, f_ref, gate_ref, bucket_ref,
                  dftc_ref, dfts_ref, wa_ref, wb_ref, wout_ref, out_ref,
                  bias_ref, kpad_ref, vt_ref, frev_ref, uw_ref, o_ref, s_ref, p_ref,
                  fmix_ref, yb_ref):
    b = pl.program_id(0)
    t = pl.program_id(1)

    @pl.when((b == 0) & (t == 0))
    def _build_bias():
        bucket = bucket_ref[...]
        kj = lax.broadcasted_iota(jnp.int32, (3 * BLK, BLK), 0)
        qi = lax.broadcasted_iota(jnp.int32, (3 * BLK, BLK), 1)
        in_window = jnp.abs(kj - BLK - qi) <= WINDOW
        valid = (in_window, in_window & (kj >= BLK), in_window & (kj < 2 * BLK))
        for h in range(N_HEADS):
            def pick(bk, acc, h=h):
                return jnp.where(bucket == bk, relb_ref[bk, h] * LOG2E, acc)
            base = lax.fori_loop(0, N_BUCKETS, pick, jnp.zeros((3 * BLK, BLK), F32))
            for v in range(3):
                bias_ref[v, h] = jnp.where(valid[v], base, -jnp.inf)

    @pl.when(t == 0)
    def _per_sequence():
        kpad_ref[0:BLK, :] = jnp.zeros((BLK, KV_WIDTH), BF16)
        kpad_ref[BLK:BLK + SEQ, :] = kv_ref[:, :KV_WIDTH]
        kpad_ref[BLK + SEQ:, :] = jnp.zeros((BLK, KV_WIDTH), BF16)
        vt_ref[:, 0:BLK] = jnp.zeros((N_KV * VT_ROWS, BLK), BF16)
        vt_ref[:, BLK + SEQ:] = jnp.zeros((N_KV * VT_ROWS, BLK), BF16)
        for c in range(SEQ // BLK):
            v_t = kv_ref[c * BLK:(c + 1) * BLK, KV_WIDTH:].T
            for kh in range(N_KV):
                vt_ref[kh * VT_ROWS:kh * VT_ROWS + HD, (c + 1) * BLK:(c + 2) * BLK] = (
                    v_t[kh * HD:(kh + 1) * HD])
        for kh in range(N_KV):
            vt_ref[kh * VT_ROWS + HD:(kh + 1) * VT_ROWS, :] = jnp.ones(
                (VT_ROWS - HD, SEQ + 2 * BLK), BF16)
        jj = lax.broadcasted_iota(jnp.int32, (BLK, 2 * BLK), 0)
        cc = lax.broadcasted_iota(jnp.int32, (BLK, 2 * BLK), 1)
        flip = jnp.where(jj + cc == BLK, 1.0, 0.0).astype(BF16)
        for blk in range(HALF_SEQ // BLK):
            src = SEQ - (blk + 1) * BLK
            if blk == 0:
                rev = _dot(flip[:, :BLK], f_ref[src:src + BLK, :])
            else:
                rev = _dot(flip, f_ref[src:src + 2 * BLK, :])
            frev_ref[blk * BLK:(blk + 1) * BLK, :] = rev.astype(BF16)
        first_row = lax.broadcasted_iota(jnp.int32, (V7X_BF16_SUBLANES, FG_DIM), 0) == 0
        for g in range(N_FGROUPS):
            lo = g * FG_DIM
            both = jnp.concatenate([f_ref[0:HALF_SEQ, lo:lo + FG_DIM],
                                    frev_ref[:, lo:lo + FG_DIM]], axis=1)
            r = _dot(both, dftc_ref[...])
            uw_ref[0:HALF_SEQ, lo:lo + FG_DIM] = r[:, :FG_DIM].astype(BF16)
            uw_ref[HALF_SEQ:, lo:lo + FG_DIM] = r[:, FG_DIM:].astype(BF16)
            mid = _dot(f_ref[HALF_SEQ:HALF_SEQ + V7X_BF16_SUBLANES, lo:lo + FG_DIM],
                       dftc_ref[0:FG_DIM, 0:FG_DIM])
            head = uw_ref[HALF_SEQ:HALF_SEQ + V7X_BF16_SUBLANES, lo:lo + FG_DIM]
            uw_ref[HALF_SEQ:HALF_SEQ + V7X_BF16_SUBLANES, lo:lo + FG_DIM] = jnp.where(
                first_row, mid, head.astype(F32)).astype(BF16)

    def block_keys(i):
        n = t * BLOCKS_PER_TILE + i
        variant = jnp.where(n == 0, 1, jnp.where(n == SEQ // BLK - 1, 2, 0))
        return pl.multiple_of(n * BLK, BLK), variant

    def scores(i):
        key0, _ = block_keys(i)
        q_t = q_ref[i * BLK:(i + 1) * BLK, :].T
        zero = jnp.zeros((HD, BLK), BF16)
        cols = []
        for h in range(N_HEADS):
            q_h = q_t[h * HD:(h + 1) * HD]
            cols.append(jnp.concatenate([q_h, zero] if h < GQ else [zero, q_h], axis=0))
        s_ref[i % 2] = _dot(kpad_ref[pl.ds(key0, 3 * BLK), :], jnp.concatenate(cols, axis=1))

    def softmax(i):
        _, variant = block_keys(i)
        buf = i % 2
        sink_terms = []
        for h in range(N_HEADS):
            sink = sink_ref[h] * LOG2E
            cols = slice(h * BLK, (h + 1) * BLK)
            top = s_ref[buf, 0:KEY_CHUNK, cols] + bias_ref[variant, h, 0:KEY_CHUNK, :]
            for k0 in range(KEY_CHUNK, 3 * BLK, KEY_CHUNK):
                top = jnp.maximum(top, s_ref[buf, k0:k0 + KEY_CHUNK, cols]
                                  + bias_ref[variant, h, k0:k0 + KEY_CHUNK, :])
            m = jnp.maximum(jnp.max(top, axis=0, keepdims=True), sink)
            for k0 in range(0, 3 * BLK, KEY_CHUNK):
                z = (s_ref[buf, k0:k0 + KEY_CHUNK, cols] - m) + bias_ref[variant, h, k0:k0 + KEY_CHUNK, :]
                p_ref[buf, k0:k0 + KEY_CHUNK, cols] = jnp.exp2(z).astype(BF16)
            sink_terms.append(jnp.exp2(sink - m))
        return sink_terms

    def weighted_values(i, sink_terms):
        key0, _ = block_keys(i)
        for kh in range(N_KV):
            o_t = _dot(vt_ref[kh * VT_ROWS:(kh + 1) * VT_ROWS, pl.ds(key0, 3 * BLK)],
                       p_ref[i % 2, :, kh * GQ * BLK:(kh + 1) * GQ * BLK])
            den = o_t[HD:HD + 1] + jnp.concatenate(sink_terms[kh * GQ:(kh + 1) * GQ], axis=1)
            o_n = o_t[:HD] / den
            for pr in range(GQ // 2):
                pair_t = jnp.concatenate([o_n[:, 2 * pr * BLK:(2 * pr + 1) * BLK],
                                          o_n[:, (2 * pr + 1) * BLK:(2 * pr + 2) * BLK]], axis=0)
                lane0 = (kh * GQ + 2 * pr) * HD
                o_ref[i * BLK:(i + 1) * BLK, lane0:lane0 + 2 * HD] = pair_t.T.astype(BF16)

    half_tile = ROW_TILE // 2
    scores(0)
    for i in range(BLOCKS_PER_TILE):
        if i + 1 < BLOCKS_PER_TILE:
            scores(i + 1)
        if i < 2:
            rows = slice(i * half_tile, (i + 1) * half_tile)
            freq0 = pl.multiple_of(t * ROW_TILE + i * half_tile, half_tile)
            fmix_ref[rows, :] = _dot(dfts_ref[pl.ds(freq0, half_tile), :], uw_ref[...]).astype(BF16)
        elif i == 2:
            yb_ref[...] = _dot(fmix_ref[...], wb_ref[...])
        weighted_values(i, softmax(i))

    y_a = _dot(o_ref[...], wa_ref[...])
    mix = (gate_ref[:, :D_MODEL].astype(F32) * y_a + gate_ref[:, D_MODEL:].astype(F32) * yb_ref[...])
    out_ref[...] = x1_ref[...] + _dot(mix.astype(BF16), wout_ref[...])


def _resident(shape):
    return pl.BlockSpec(shape, lambda *_: (0,) * len(shape), pipeline_mode=pl.Buffered(1))


def _rows(width):
    return pl.BlockSpec((ROW_TILE, width), lambda i: (i, 0))


def _slab_spec(shape):
    rows, cols = shape
    steps = TOKENS // ROW_TILE
    slab = next(r for r in range(V7X_BF16_SUBLANES, rows + 1, V7X_BF16_SUBLANES)
                if rows % r == 0 and r * steps >= rows)
    return pl.BlockSpec((slab, cols), lambda i: (jnp.minimum(i, rows // slab - 1), 0))


def _ffn_proj(x, g1, wup, wdn, gmix, win, bg, later_weights):
    assert len(later_weights) == N_LATER_WEIGHTS
    slabs = [_slab_spec(w.shape) for w in later_weights]
    hbm = pl.BlockSpec(memory_space=pl.ANY)
    return pl.pallas_call(
        _ffn_proj_kernel,
        grid=(TOKENS // ROW_TILE,),
        in_specs=[_rows(D_MODEL), _resident((1, D_MODEL)), hbm, hbm, _resident((1, D_MODEL)),
                  hbm, _resident((1, 2 * D_MODEL))] + slabs,
        out_specs=[_rows(D_MODEL), _rows(Q_WIDTH), _rows(2 * KV_WIDTH), _rows(F_WIDTH),
                   _rows(2 * D_MODEL)] + slabs,
        out_shape=[jax.ShapeDtypeStruct((TOKENS, D_MODEL), F32),
                   jax.ShapeDtypeStruct((TOKENS, Q_WIDTH), BF16),
                   jax.ShapeDtypeStruct((TOKENS, 2 * KV_WIDTH), BF16),
                   jax.ShapeDtypeStruct((TOKENS, F_WIDTH), BF16),
                   jax.ShapeDtypeStruct((TOKENS, 2 * D_MODEL), BF16)]
        + [jax.ShapeDtypeStruct(w.shape, BF16) for w in later_weights],
        scratch_shapes=[pltpu.VMEM((ROW_TILE, D_FF), BF16),
                        pltpu.VMEM((D_MODEL, 2 * D_FF), BF16),
                        pltpu.VMEM((D_FF, D_MODEL), BF16),
                        pltpu.VMEM((D_MODEL, IN_WIDTH), BF16)],
        compiler_params=pltpu.CompilerParams(dimension_semantics=("arbitrary",),
                                             vmem_limit_bytes=V7X_VMEM_LIMIT_BYTES),
        name="ffn_proj",
    )(x, g1, wup, wdn, gmix, win, bg, *later_weights)


def _ffn_final(x, g2, wup, wdn, gfin):
    return pl.pallas_call(
        _ffn_final_kernel,
        grid=(TOKENS // ROW_TILE,),
        in_specs=[_rows(D_MODEL), _resident((1, D_MODEL)), _resident((D_MODEL, 2 * D_FF)),
                  _resident((D_FF, D_MODEL)), _resident((1, D_MODEL))],
        out_specs=_rows(D_MODEL),
        out_shape=jax.ShapeDtypeStruct((TOKENS, D_MODEL), F32),
        scratch_shapes=[pltpu.VMEM((ROW_TILE, D_FF), BF16)],
        compiler_params=pltpu.CompilerParams(dimension_semantics=("arbitrary",),
                                             vmem_limit_bytes=V7X_VMEM_LIMIT_BYTES),
        name="ffn_final",
    )(x, g2, wup, wdn, gfin)


def _mixer(rel_bias, sink, x1, q, kv, f, gates, bucket, dftc, dfts, wa, wb, wout):
    def tile_rows(width):
        return pl.BlockSpec((ROW_TILE, width), lambda b, t: (b * TILES_PER_SEQ + t, 0))

    def seq_rows(width):
        return pl.BlockSpec((SEQ, width), lambda b, t: (b, 0))

    smem = pl.BlockSpec(memory_space=pltpu.SMEM)
    return pl.pallas_call(
        _mixer_kernel,
        grid=(BATCH, TILES_PER_SEQ),
        in_specs=[smem, smem, tile_rows(D_MODEL), tile_rows(Q_WIDTH), seq_rows(2 * KV_WIDTH),
                  seq_rows(F_WIDTH), tile_rows(2 * D_MODEL), _resident((3 * BLK, BLK)),
                  _resident((2 * FG_DIM, 2 * FG_DIM)),
                  _resident((SEQ, SEQ)),
                  _resident((Q_WIDTH, D_MODEL)), _resident((F_WIDTH, D_MODEL)),
                  _resident((D_MODEL, D_MODEL))],
        out_specs=tile_rows(D_MODEL),
        out_shape=jax.ShapeDtypeStruct((TOKENS, D_MODEL), F32),
        scratch_shapes=[pltpu.VMEM((3, N_HEADS, 3 * BLK, BLK), F32),
                        pltpu.VMEM((SEQ + 2 * BLK, KV_WIDTH), BF16),
                        pltpu.VMEM((N_KV * VT_ROWS, SEQ + 2 * BLK), BF16),
                        pltpu.VMEM((HALF_SEQ, F_WIDTH), BF16),
                        pltpu.VMEM((SEQ, F_WIDTH), BF16),
                        pltpu.VMEM((ROW_TILE, Q_WIDTH), BF16),
                        pltpu.VMEM((2, 3 * BLK, N_HEADS * BLK), F32),
                        pltpu.VMEM((2, 3 * BLK, N_HEADS * BLK), BF16),
                        pltpu.VMEM((ROW_TILE, F_WIDTH), BF16),
                        pltpu.VMEM((ROW_TILE, D_MODEL), F32)],
        compiler_params=pltpu.CompilerParams(dimension_semantics=("arbitrary", "arbitrary"),
                                             vmem_limit_bytes=V7X_VMEM_LIMIT_BYTES),
        name="mixer",
    )(rel_bias, sink, x1, q, kv, f, gates, bucket, dftc, dfts, wa, wb, wout)


def kernel(x, g_ffn1, w_up1, w_down1, g_mix, w_in, b_gate, sink, rel_bias, w_branch_a, w_branch_b, w_out, g_ffn2, w_up2, w_down2, g_final):
    assert x.shape == (BATCH, SEQ, D_MODEL) and w_up1.shape[0] == DEPTH == 1
    chan, pos = _dft_tables()
    bucket = jnp.asarray(_t5_bucket_table().T)
    dftc = jnp.asarray(chan).astype(BF16)
    dfts = jnp.asarray(pos).astype(BF16)
    row = lambda v: v.reshape(1, -1)
    x0 = x.reshape(TOKENS, D_MODEL)
    x1, q, kv, f, gates, wa, wb, wout, wup2, wdn2 = _ffn_proj(
        x0, row(g_ffn1[0]), w_up1[0], w_down1[0], row(g_mix[0]), w_in[0], row(b_gate[0]),
        [w_branch_a[0], w_branch_b[0], w_out[0], w_up2[0], w_down2[0]])
    x2 = _mixer(rel_bias, sink[0], x1, q, kv, f, gates, bucket, dftc, dfts, wa, wb, wout)
    out = _ffn_final(x2, row(g_ffn2[0]), wup2, wdn2, row(g_final))
    return out.reshape(BATCH, SEQ, D_MODEL)
```

```python
import functools

import numpy as np
import jax
import jax.numpy as jnp
from jax import lax
from jax.experimental import pallas as pl
from jax.experimental.pallas import tpu as pltpu

D_MODEL = 1024
BATCH = 8
SEQ = 2048
DEPTH = 1
N_HEADS = 8
N_KV = 2
GQ = N_HEADS // N_KV
HD = 64
Q_WIDTH = N_HEADS * HD
KV_WIDTH = N_KV * HD
WINDOW = 128
BLK = 128
N_FGROUPS = 4
FG_DIM = 128
F_WIDTH = N_FGROUPS * FG_DIM
N_BUCKETS = 32
MAX_DIST = 128
D_FF = 2816
EPS = 1e-6
O_K = Q_WIDTH
O_V = O_K + KV_WIDTH
O_F = O_V + KV_WIDTH
O_G = O_F + F_WIDTH
IN_WIDTH = O_G + 2 * D_MODEL

F32 = jnp.float32
BF16 = jnp.bfloat16

V7X_MXU_DIM = 256
V7X_BF16_SUBLANES = 16
V7X_VMEM_LIMIT_BYTES = 56 * 1024 * 1024

TOKENS = BATCH * SEQ
HALF_SEQ = SEQ // 2
ROW_TILE = 512
TILES_PER_SEQ = SEQ // ROW_TILE
BLOCKS_PER_TILE = ROW_TILE // BLK
FF_CHUNK = V7X_MXU_DIM
FFN_SUBTILES = 2
VT_ROWS = HD + V7X_BF16_SUBLANES
N_LATER_WEIGHTS = 5
STAGE_SLOTS = 4
STAGE_BYTES = 3 * 512 * 1024
KEY_CHUNK = 64
LOG2E = float(np.log2(np.e))


def _t5_bucket_table():
    rel = (np.arange(3 * BLK)[None, :] - BLK) - np.arange(BLK)[:, None]
    half = N_BUCKETS // 2
    max_exact = half // 2
    ret = (rel > 0).astype(np.int32) * half
    n = np.abs(rel)
    n_safe = np.maximum(n, 1).astype(np.float32)
    large = max_exact + (np.log(n_safe / max_exact) / np.log(MAX_DIST / max_exact)
                         * (half - max_exact)).astype(np.int32)
    large = np.minimum(large, half - 1)
    return (ret + np.where(n < max_exact, n, large)).astype(np.int32)


@functools.lru_cache(maxsize=None)
def _dft_tables():
    def cos_sin(n):
        idx = np.arange(n)
        ang = 2.0 * np.pi * ((idx[:, None] * idx[None, :]) % n).astype(np.float64) / n
        return np.cos(ang) / np.sqrt(n), np.sin(ang) / np.sqrt(n)
    cc, sc = cos_sin(FG_DIM)
    cs, ss = cos_sin(SEQ)
    chan = np.block([[cc, sc], [cc, -sc]]).astype(np.float32)
    neg_sin = -ss[:, :HALF_SEQ]
    neg_sin[:, 0] = np.where(np.arange(SEQ) % 2 == 0, 1.0, -1.0) / np.sqrt(SEQ)
    pos = np.concatenate([cs[:, :HALF_SEQ], neg_sin], axis=1).astype(np.float32)
    return chan, pos


def _rms(x, g):
    return x * lax.rsqrt(jnp.mean(x * x, axis=-1, keepdims=True) + EPS) * g


def _sigmoid(z):
    return 1.0 / (1.0 + jnp.exp(-z))


def _dot(a, b):
    return jnp.dot(a, b, preferred_element_type=F32)


def _skewed_ffn(x_ref, g_ref, wup_ref, wdn_ref, act_ref, finish):
    sub = ROW_TILE // FFN_SUBTILES
    rows = [slice(k * sub, (k + 1) * sub) for k in range(FFN_SUBTILES)]
    normed = [None] * FFN_SUBTILES

    def prologue(k):
        normed[k] = _rms(x_ref[rows[k], :], g_ref[...]).astype(BF16)

    def up_chunk(k, c):
        lo = c * FF_CHUNK
        gate = _dot(normed[k], wup_ref[:, lo:lo + FF_CHUNK])
        up = _dot(normed[k], wup_ref[:, D_FF + lo:D_FF + lo + FF_CHUNK])
        act_ref[rows[k], lo:lo + FF_CHUNK] = (gate * _sigmoid(gate) * up).astype(BF16)

    def epilogue(k):
        finish(k, rows[k], x_ref[rows[k], :] + 0.5 * _dot(act_ref[rows[k], :], wdn_ref[...]))

    prologue(0)
    for k in range(FFN_SUBTILES):
        for c in range(D_FF // FF_CHUNK):
            up_chunk(k, c)
            if c == 0 and k + 1 < FFN_SUBTILES:
                prologue(k + 1)
            if c == 1 and k > 0:
                epilogue(k - 1)
    epilogue(FFN_SUBTILES - 1)


def _stage_weight(src_hbm, dst_ref):
    rows, cols = dst_ref.shape
    slab = max(r for r in range(V7X_BF16_SUBLANES, rows + 1, V7X_BF16_SUBLANES)
               if rows % r == 0 and r * cols * 4 <= STAGE_BYTES)
    n_slabs = rows // slab

    def staged(stage_ref, sem):
        def slab_copy(c, slot):
            return pltpu.make_async_copy(src_hbm.at[pl.ds(c * slab, slab), :],
                                         stage_ref.at[slot], sem.at[slot])

        for c in range(min(STAGE_SLOTS - 1, n_slabs)):
            slab_copy(c, c).start()

        def step(c, carry):
            ahead = c + STAGE_SLOTS - 1

            @pl.when(ahead < n_slabs)
            def _start_ahead():
                slab_copy(ahead, ahead % STAGE_SLOTS).start()

            slot = c % STAGE_SLOTS
            slab_copy(c, slot).wait()
            dst_ref[pl.ds(pl.multiple_of(c * slab, slab), slab), :] = stage_ref[slot].astype(BF16)
            return carry

        lax.fori_loop(0, n_slabs, step, 0)

    pl.run_scoped(staged, pltpu.VMEM((STAGE_SLOTS, slab, cols), F32),
                  pltpu.SemaphoreType.DMA((STAGE_SLOTS,)))


def _ffn_proj_kernel(x_ref, g1_ref, wup_hbm, wdn_hbm, gmix_ref, win_hbm, bg_ref, *refs):
    later_f32 = refs[:N_LATER_WEIGHTS]
    x1_ref, q_ref, kv_ref, f_ref, gate_ref = refs[N_LATER_WEIGHTS:N_LATER_WEIGHTS + 5]
    later_bf16 = refs[N_LATER_WEIGHTS + 5:2 * N_LATER_WEIGHTS + 5]
    act_ref, wup_ref, wdn_ref, win_ref = refs[2 * N_LATER_WEIGHTS + 5:]

    @pl.when(pl.program_id(0) == 0)
    def _stage_own_weights():
        _stage_weight(wup_hbm, wup_ref)
        _stage_weight(wdn_hbm, wdn_ref)
        _stage_weight(win_hbm, win_ref)

    for src, dst in zip(later_f32, later_bf16):
        dst[...] = src[...].astype(BF16)

    def project(k, rows, x1):
        x1_ref[rows, :] = x1
        h2 = _rms(x1, gmix_ref[...]).astype(BF16)
        q_ref[rows, :] = (_dot(h2, win_ref[:, 0:O_K]) * (HD ** -0.5 * LOG2E)).astype(BF16)
        kv_ref[rows, :] = _dot(h2, win_ref[:, O_K:O_F]).astype(BF16)
        f_ref[rows, :] = _dot(h2, win_ref[:, O_F:O_G]).astype(BF16)
        for c in range(2):
            lo = c * D_MODEL
            z = _dot(h2, win_ref[:, O_G + lo:O_G + lo + D_MODEL]) + bg_ref[:, lo:lo + D_MODEL]
            gate_ref[rows, lo:lo + D_MODEL] = _sigmoid(z).astype(BF16)

    _skewed_ffn(x_ref, g1_ref, wup_ref, wdn_ref, act_ref, project)


def _ffn_final_kernel(x_ref, g2_ref, wup_ref, wdn_ref, gfin_ref, out_ref, act_ref):
    def final_norm(k, rows, x3):
        out_ref[rows, :] = _rms(x3, gfin_ref[...])

    _skewed_ffn(x_ref, g2_ref, wup_ref, wdn_ref, act_ref, final_norm)


def _mixer_kernel(relb_ref, sink_ref, x1_ref, q_ref, kv_ref, f_ref, gate_ref, bucket_ref,
                  dftc_ref, dfts_ref, wa_ref, wb_ref, wout_ref, out_ref,
                  bias_ref, kpad_ref, vt_ref, frev_ref, uw_ref, o_ref, s_ref, p_ref,
                  fmix_ref, yb_ref):
    b = pl.program_id(0)
    t = pl.program_id(1)

    @pl.when((b == 0) & (t == 0))
    def _build_bias():
        bucket = bucket_ref[...]
        kj = lax.broadcasted_iota(jnp.int32, (3 * BLK, BLK), 0)
        qi = lax.broadcasted_iota(jnp.int32, (3 * BLK, BLK), 1)
        in_window = jnp.abs(kj - BLK - qi) <= WINDOW
        valid = (in_window, in_window & (kj >= BLK), in_window & (kj < 2 * BLK))
        for h in range(N_HEADS):
            def pick(bk, acc, h=h):
                return jnp.where(bucket == bk, relb_ref[bk, h] * LOG2E, acc)
            base = lax.fori_loop(0, N_BUCKETS, pick, jnp.zeros((3 * BLK, BLK), F32))
            for v in range(3):
                bias_ref[v, h] = jnp.where(valid[v], base, -jnp.inf)

    @pl.when(t == 0)
    def _per_sequence():
        kpad_ref[0:BLK, :] = jnp.zeros((BLK, KV_WIDTH), BF16)
        kpad_ref[BLK:BLK + SEQ, :] = kv_ref[:, :KV_WIDTH]
        kpad_ref[BLK + SEQ:, :] = jnp.zeros((BLK, KV_WIDTH), BF16)
        vt_ref[:, 0:BLK] = jnp.zeros((N_KV * VT_ROWS, BLK), BF16)
        vt_ref[:, BLK + SEQ:] = jnp.zeros((N_KV * VT_ROWS, BLK), BF16)
        for c in range(SEQ // BLK):
            v_t = kv_ref[c * BLK:(c + 1) * BLK, KV_WIDTH:].T
            for kh in range(N_KV):
                vt_ref[kh * VT_ROWS:kh * VT_ROWS + HD, (c + 1) * BLK:(c + 2) * BLK] = (
                    v_t[kh * HD:(kh + 1) * HD])
        for kh in range(N_KV):
            vt_ref[kh * VT_ROWS + HD:(kh + 1) * VT_ROWS, :] = jnp.ones(
                (VT_ROWS - HD, SEQ + 2 * BLK), BF16)
        jj = lax.broadcasted_iota(jnp.int32, (BLK, 2 * BLK), 0)
        cc = lax.broadcasted_iota(jnp.int32, (BLK, 2 * BLK), 1)
        flip = jnp.where(jj + cc == BLK, 1.0, 0.0).astype(BF16)
        for blk in range(HALF_SEQ // BLK):
            src = SEQ - (blk + 1) * BLK
            if blk == 0:
                rev = _dot(flip[:, :BLK], f_ref[src:src + BLK, :])
            else:
                rev = _dot(flip, f_ref[src:src + 2 * BLK, :])
            frev_ref[blk * BLK:(blk + 1) * BLK, :] = rev.astype(BF16)
        first_row = lax.broadcasted_iota(jnp.int32, (V7X_BF16_SUBLANES, FG_DIM), 0) == 0
        for g in range(N_FGROUPS):
            lo = g * FG_DIM
            both = jnp.concatenate([f_ref[0:HALF_SEQ, lo:lo + FG_DIM],
                                    frev_ref[:, lo:lo + FG_DIM]], axis=1)
            r = _dot(both, dftc_ref[...])
            uw_ref[0:HALF_SEQ, lo:lo + FG_DIM] = r[:, :FG_DIM].astype(BF16)
            uw_ref[HALF_SEQ:, lo:lo + FG_DIM] = r[:, FG_DIM:].astype(BF16)
            mid = _dot(f_ref[HALF_SEQ:HALF_SEQ + V7X_BF16_SUBLANES, lo:lo + FG_DIM],
                       dftc_ref[0:FG_DIM, 0:FG_DIM])
            head = uw_ref[HALF_SEQ:HALF_SEQ + V7X_BF16_SUBLANES, lo:lo + FG_DIM]
            uw_ref[HALF_SEQ:HALF_SEQ + V7X_BF16_SUBLANES, lo:lo + FG_DIM] = jnp.where(
                first_row, mid, head.astype(F32)).astype(BF16)

    def block_keys(i):
        n = t * BLOCKS_PER_TILE + i
        variant = jnp.where(n == 0, 1, jnp.where(n == SEQ // BLK - 1, 2, 0))
        return pl.multiple_of(n * BLK, BLK), variant

    def scores(i):
        key0, _ = block_keys(i)
        q_t = q_ref[i * BLK:(i + 1) * BLK, :].T
        zero = jnp.zeros((HD, BLK), BF16)
        cols = []
        for h in range(N_HEADS):
            q_h = q_t[h * HD:(h + 1) * HD]
            cols.append(jnp.concatenate([q_h, zero] if h < GQ else [zero, q_h], axis=0))
        s_ref[i % 2] = _dot(kpad_ref[pl.ds(key0, 3 * BLK), :], jnp.concatenate(cols, axis=1))

    def softmax(i):
        _, variant = block_keys(i)
        buf = i % 2
        sink_terms = []
        for h in range(N_HEADS):
            sink = sink_ref[h] * LOG2E
            cols = slice(h * BLK, (h + 1) * BLK)
            top = s_ref[buf, 0:KEY_CHUNK, cols] + bias_ref[variant, h, 0:KEY_CHUNK, :]
            for k0 in range(KEY_CHUNK, 3 * BLK, KEY_CHUNK):
                top = jnp.maximum(top, s_ref[buf, k0:k0 + KEY_CHUNK, cols]
                                  + bias_ref[variant, h, k0:k0 + KEY_CHUNK, :])
            m = jnp.maximum(jnp.max(top, axis=0, keepdims=True), sink)
            for k0 in range(0, 3 * BLK, KEY_CHUNK):
                z = (s_ref[buf, k0:k0 + KEY_CHUNK, cols] - m) + bias_ref[variant, h, k0:k0 + KEY_CHUNK, :]
                p_ref[buf, k0:k0 + KEY_CHUNK, cols] = jnp.exp2(z).astype(BF16)
            sink_terms.append(jnp.exp2(sink - m))
        return sink_terms

    def weighted_values(i, sink_terms):
        key0, _ = block_keys(i)
        for kh in range(N_KV):
            o_t = _dot(vt_ref[kh * VT_ROWS:(kh + 1) * VT_ROWS, pl.ds(key0, 3 * BLK)],
                       p_ref[i % 2, :, kh * GQ * BLK:(kh + 1) * GQ * BLK])
            den = o_t[HD:HD + 1] + jnp.concatenate(sink_terms[kh * GQ:(kh + 1) * GQ], axis=1)
            o_n = o_t[:HD] / den
            for pr in range(GQ // 2):
                pair_t = jnp.concatenate([o_n[:, 2 * pr * BLK:(2 * pr + 1) * BLK],
                                          o_n[:, (2 * pr + 1) * BLK:(2 * pr + 2) * BLK]], axis=0)
                lane0 = (kh * GQ + 2 * pr) * HD
                o_ref[i * BLK:(i + 1) * BLK, lane0:lane0 + 2 * HD] = pair_t.T.astype(BF16)

    half_tile = ROW_TILE // 2
    scores(0)
    for i in range(BLOCKS_PER_TILE):
        if i + 1 < BLOCKS_PER_TILE:
            scores(i + 1)
        if i < 2:
            rows = slice(i * half_tile, (i + 1) * half_tile)
            freq0 = pl.multiple_of(t * ROW_TILE + i * half_tile, half_tile)
            fmix_ref[rows, :] = _dot(dfts_ref[pl.ds(freq0, half_tile), :], uw_ref[...]).astype(BF16)
        elif i == 2:
            yb_ref[...] = _dot(fmix_ref[...], wb_ref[...])
        weighted_values(i, softmax(i))

    y_a = _dot(o_ref[...], wa_ref[...])
    mix = (gate_ref[:, :D_MODEL].astype(F32) * y_a + gate_ref[:, D_MODEL:].astype(F32) * yb_ref[...])
    out_ref[...] = x1_ref[...] + _dot(mix.astype(BF16), wout_ref[...])


def _resident(shape):
    return pl.BlockSpec(shape, lambda *_: (0,) * len(shape), pipeline_mode=pl.Buffered(1))


def _rows(width):
    return pl.BlockSpec((ROW_TILE, width), lambda i: (i, 0))


def _slab_spec(shape):
    rows, cols = shape
    steps = TOKENS // ROW_TILE
    slab = next(r for r in range(V7X_BF16_SUBLANES, rows + 1, V7X_BF16_SUBLANES)
                if rows % r == 0 and r * steps >= rows)
    return pl.BlockSpec((slab, cols), lambda i: (jnp.minimum(i, rows // slab - 1), 0))


def _ffn_proj(x, g1, wup, wdn, gmix, win, bg, later_weights):
    assert len(later_weights) == N_LATER_WEIGHTS
    slabs = [_slab_spec(w.shape) for w in later_weights]
    hbm = pl.BlockSpec(memory_space=pl.ANY)
    return pl.pallas_call(
        _ffn_proj_kernel,
        grid=(TOKENS // ROW_TILE,),
        in_specs=[_rows(D_MODEL), _resident((1, D_MODEL)), hbm, hbm, _resident((1, D_MODEL)),
                  hbm, _resident((1, 2 * D_MODEL))] + slabs,
        out_specs=[_rows(D_MODEL), _rows(Q_WIDTH), _rows(2 * KV_WIDTH), _rows(F_WIDTH),
                   _rows(2 * D_MODEL)] + slabs,
        out_shape=[jax.ShapeDtypeStruct((TOKENS, D_MODEL), F32),
                   jax.ShapeDtypeStruct((TOKENS, Q_WIDTH), BF16),
                   jax.ShapeDtypeStruct((TOKENS, 2 * KV_WIDTH), BF16),
                   jax.ShapeDtypeStruct((TOKENS, F_WIDTH), BF16),
                   jax.ShapeDtypeStruct((TOKENS, 2 * D_MODEL), BF16)]
        + [jax.ShapeDtypeStruct(w.shape, BF16) for w in later_weights],
        scratch_shapes=[pltpu.VMEM((ROW_TILE, D_FF), BF16),
                        pltpu.VMEM((D_MODEL, 2 * D_FF), BF16),
                        pltpu.VMEM((D_FF, D_MODEL), BF16),
                        pltpu.VMEM((D_MODEL, IN_WIDTH), BF16)],
        compiler_params=pltpu.CompilerParams(dimension_semantics=("arbitrary",),
                                             vmem_limit_bytes=V7X_VMEM_LIMIT_BYTES),
        name="ffn_proj",
    )(x, g1, wup, wdn, gmix, win, bg, *later_weights)


def _ffn_final(x, g2, wup, wdn, gfin):
    return pl.pallas_call(
        _ffn_final_kernel,
        grid=(TOKENS // ROW_TILE,),
        in_specs=[_rows(D_MODEL), _resident((1, D_MODEL)), _resident((D_MODEL, 2 * D_FF)),
                  _resident((D_FF, D_MODEL)), _resident((1, D_MODEL))],
        out_specs=_rows(D_MODEL),
        out_shape=jax.ShapeDtypeStruct((TOKENS, D_MODEL), F32),
        scratch_shapes=[pltpu.VMEM((ROW_TILE, D_FF), BF16)],
        compiler_params=pltpu.CompilerParams(dimension_semantics=("arbitrary",),
                                             vmem_limit_bytes=V7X_VMEM_LIMIT_BYTES),
        name="ffn_final",
    )(x, g2, wup, wdn, gfin)


def _mixer(rel_bias, sink, x1, q, kv, f, gates, bucket, dftc, dfts, wa, wb, wout):
    def tile_rows(width):
        return pl.BlockSpec((ROW_TILE, width), lambda b, t: (b * TILES_PER_SEQ + t, 0))

    def seq_rows(width):
        return pl.BlockSpec((SEQ, width), lambda b, t: (b, 0))

    smem = pl.BlockSpec(memory_space=pltpu.SMEM)
    return pl.pallas_call(
        _mixer_kernel,
        grid=(BATCH, TILES_PER_SEQ),
        in_specs=[smem, smem, tile_rows(D_MODEL), tile_rows(Q_WIDTH), seq_rows(2 * KV_WIDTH),
                  seq_rows(F_WIDTH), tile_rows(2 * D_MODEL), _resident((3 * BLK, BLK)),
                  _resident((2 * FG_DIM, 2 * FG_DIM)),
                  _resident((SEQ, SEQ)),
                  _resident((Q_WIDTH, D_MODEL)), _resident((F_WIDTH, D_MODEL)),
                  _resident((D_MODEL, D_MODEL))],
        out_specs=tile_rows(D_MODEL),
        out_shape=jax.ShapeDtypeStruct((TOKENS, D_MODEL), F32),
        scratch_shapes=[pltpu.VMEM((3, N_HEADS, 3 * BLK, BLK), F32),
                        pltpu.VMEM((SEQ + 2 * BLK, KV_WIDTH), BF16),
                        pltpu.VMEM((N_KV * VT_ROWS, SEQ + 2 * BLK), BF16),
                        pltpu.VMEM((HALF_SEQ, F_WIDTH), BF16),
                        pltpu.VMEM((SEQ, F_WIDTH), BF16),
                        pltpu.VMEM((ROW_TILE, Q_WIDTH), BF16),
                        pltpu.VMEM((2, 3 * BLK, N_HEADS * BLK), F32),
                        pltpu.VMEM((2, 3 * BLK, N_HEADS * BLK), BF16),
                        pltpu.VMEM((ROW_TILE, F_WIDTH), BF16),
                        pltpu.VMEM((ROW_TILE, D_MODEL), F32)],
        compiler_params=pltpu.CompilerParams(dimension_semantics=("arbitrary", "arbitrary"),
                                             vmem_limit_bytes=V7X_VMEM_LIMIT_BYTES),
        name="mixer",
    )(rel_bias, sink, x1, q, kv, f, gates, bucket, dftc, dfts, wa, wb, wout)


def kernel(x, g_ffn1, w_up1, w_down1, g_mix, w_in, b_gate, sink, rel_bias, w_branch_a, w_branch_b, w_out, g_ffn2, w_up2, w_down2, g_final):
    assert x.shape == (BATCH, SEQ, D_MODEL) and w_up1.shape[0] == DEPTH == 1
    chan, pos = _dft_tables()
    bucket = jnp.asarray(_t5_bucket_table().T)
    dftc = jnp.asarray(chan).astype(BF16)
    dfts = jnp.asarray(pos).astype(BF16)
    row = lambda v: v.reshape(1, -1)
    x0 = x.reshape(TOKENS, D_MODEL)
    x1, q, kv, f, gates, wa, wb, wout, wup2, wdn2 = _ffn_proj(
        x0, row(g_ffn1[0]), w_up1[0], w_down1[0], row(g_mix[0]), w_in[0], row(b_gate[0]),
        [w_branch_a[0], w_branch_b[0], w_out[0], w_up2[0], w_down2[0]])
    x2 = _mixer(rel_bias, sink[0], x1, q, kv, f, gates, bucket, dftc, dfts, wa, wb, wout)
    out = _ffn_final(x2, row(g_ffn2[0]), wup2, wdn2, row(g_final))
    return out.reshape(BATCH, SEQ, D_MODEL)
```

```python
import functools

import numpy as np
import jax
import jax.numpy as jnp
from jax import lax
from jax.experimental import pallas as pl
from jax.experimental.pallas import tpu as pltpu

D_MODEL = 1024
BATCH = 8
SEQ = 2048
DEPTH = 1
N_HEADS = 8
N_KV = 2
GQ = N_HEADS // N_KV
HD = 64
Q_WIDTH = N_HEADS * HD
KV_WIDTH = N_KV * HD
WINDOW = 128
BLK = 128
N_FGROUPS = 4
FG_DIM = 128
F_WIDTH = N_FGROUPS * FG_DIM
N_BUCKETS = 32
MAX_DIST = 128
D_FF = 2816
EPS = 1e-6
O_K = Q_WIDTH
O_V = O_K + KV_WIDTH
O_F = O_V + KV_WIDTH
O_G = O_F + F_WIDTH
IN_WIDTH = O_G + 2 * D_MODEL

F32 = jnp.float32
BF16 = jnp.bfloat16

V7X_MXU_DIM = 256
V7X_BF16_SUBLANES = 16
V7X_VMEM_LIMIT_BYTES = 56 * 1024 * 1024

TOKENS = BATCH * SEQ
HALF_SEQ = SEQ // 2
ROW_TILE = 512
TILES_PER_SEQ = SEQ // ROW_TILE
BLOCKS_PER_TILE = ROW_TILE // BLK
FF_CHUNK = V7X_MXU_DIM
FFN_SUBTILES = 2
VT_ROWS = HD + V7X_BF16_SUBLANES
N_LATER_WEIGHTS = 5
STAGE_SLOTS = 4
STAGE_BYTES = 3 * 512 * 1024
KEY_CHUNK = 64
LOG2E = float(np.log2(np.e))


def _t5_bucket_table():
    rel = (np.arange(3 * BLK)[None, :] - BLK) - np.arange(BLK)[:, None]
    half = N_BUCKETS // 2
    max_exact = half // 2
    ret = (rel > 0).astype(np.int32) * half
    n = np.abs(rel)
    n_safe = np.maximum(n, 1).astype(np.float32)
    large = max_exact + (np.log(n_safe / max_exact) / np.log(MAX_DIST / max_exact)
                         * (half - max_exact)).astype(np.int32)
    large = np.minimum(large, half - 1)
    return (ret + np.where(n < max_exact, n, large)).astype(np.int32)


@functools.lru_cache(maxsize=None)
def _dft_tables():
    def cos_sin(n):
        idx = np.arange(n)
        ang = 2.0 * np.pi * ((idx[:, None] * idx[None, :]) % n).astype(np.float64) / n
        return np.cos(ang) / np.sqrt(n), np.sin(ang) / np.sqrt(n)
    cc, sc = cos_sin(FG_DIM)
    cs, ss = cos_sin(SEQ)
    chan = np.block([[cc, sc], [cc, -sc]]).astype(np.float32)
    neg_sin = -ss[:, :HALF_SEQ]
    neg_sin[:, 0] = np.where(np.arange(SEQ) % 2 == 0, 1.0, -1.0) / np.sqrt(SEQ)
    pos = np.concatenate([cs[:, :HALF_SEQ], neg_sin], axis=1).astype(np.float32)
    return chan, pos


def _rms(x, g):
    return x * lax.rsqrt(jnp.mean(x * x, axis=-1, keepdims=True) + EPS) * g


def _sigmoid(z):
    return 1.0 / (1.0 + jnp.exp(-z))


def _dot(a, b):
    return jnp.dot(a, b, preferred_element_type=F32)


def _skewed_ffn(x_ref, g_ref, wup_ref, wdn_ref, act_ref, finish, prepare=None):
    sub = ROW_TILE // FFN_SUBTILES
    rows = [slice(k * sub, (k + 1) * sub) for k in range(FFN_SUBTILES)]
    normed = [None] * FFN_SUBTILES

    def prologue(k):
        if prepare is not None:
            prepare(rows[k])
        normed[k] = _rms(x_ref[rows[k], :], g_ref[...]).astype(BF16)

    def up_chunk(k, c):
        lo = c * FF_CHUNK
        gate = _dot(normed[k], wup_ref[:, lo:lo + FF_CHUNK])
        up = _dot(normed[k], wup_ref[:, D_FF + lo:D_FF + lo + FF_CHUNK])
        act_ref[rows[k], lo:lo + FF_CHUNK] = (gate * _sigmoid(gate) * up).astype(BF16)

    def epilogue(k):
        finish(k, rows[k], x_ref[rows[k], :] + 0.5 * _dot(act_ref[rows[k], :], wdn_ref[...]))

    prologue(0)
    for k in range(FFN_SUBTILES):
        for c in range(D_FF // FF_CHUNK):
            up_chunk(k, c)
            if c == 0 and k + 1 < FFN_SUBTILES:
                prologue(k + 1)
            if c == 1 and k > 0:
                epilogue(k - 1)
    epilogue(FFN_SUBTILES - 1)


def _stage_weight(src_hbm, dst_ref):
    rows, cols = dst_ref.shape
    slab = max(r for r in range(V7X_BF16_SUBLANES, rows + 1, V7X_BF16_SUBLANES)
               if rows % r == 0 and r * cols * 4 <= STAGE_BYTES)
    n_slabs = rows // slab

    def staged(stage_ref, sem):
        def slab_copy(c, slot):
            return pltpu.make_async_copy(src_hbm.at[pl.ds(c * slab, slab), :],
                                         stage_ref.at[slot], sem.at[slot])

        for c in range(min(STAGE_SLOTS - 1, n_slabs)):
            slab_copy(c, c).start()

        def step(c, carry):
            ahead = c + STAGE_SLOTS - 1

            @pl.when(ahead < n_slabs)
            def _start_ahead():
                slab_copy(ahead, ahead % STAGE_SLOTS).start()

            slot = c % STAGE_SLOTS
            slab_copy(c, slot).wait()
            dst_ref[pl.ds(pl.multiple_of(c * slab, slab), slab), :] = stage_ref[slot].astype(BF16)
            return carry

        lax.fori_loop(0, n_slabs, step, 0)

    pl.run_scoped(staged, pltpu.VMEM((STAGE_SLOTS, slab, cols), F32),
                  pltpu.SemaphoreType.DMA((STAGE_SLOTS,)))


def _ffn_proj_kernel(x_ref, g1_ref, wup_hbm, wdn_hbm, gmix_ref, win_hbm, bg_ref, *refs):
    later_f32 = refs[:N_LATER_WEIGHTS]
    x1_ref, q_ref, kv_ref, f_ref, gate_ref = refs[N_LATER_WEIGHTS:N_LATER_WEIGHTS + 5]
    later_bf16 = refs[N_LATER_WEIGHTS + 5:2 * N_LATER_WEIGHTS + 5]
    act_ref, wup_ref, wdn_ref, win_ref = refs[2 * N_LATER_WEIGHTS + 5:]

    @pl.when(pl.program_id(0) == 0)
    def _stage_own_weights():
        _stage_weight(wup_hbm, wup_ref)
        _stage_weight(wdn_hbm, wdn_ref)
        _stage_weight(win_hbm, win_ref)

    for src, dst in zip(later_f32, later_bf16):
        dst[...] = src[...].astype(BF16)

    def project(k, rows, x1):
        x1_ref[rows, :] = x1
        h2 = _rms(x1, gmix_ref[...]).astype(BF16)
        q_ref[rows, :] = (_dot(h2, win_ref[:, 0:O_K]) * (HD ** -0.5 * LOG2E)).astype(BF16)
        kv_ref[rows, :] = _dot(h2, win_ref[:, O_K:O_F]).astype(BF16)
        f_ref[rows, :] = _dot(h2, win_ref[:, O_F:O_G]).astype(BF16)
        for c in range(2):
            lo = c * D_MODEL
            z = _dot(h2, win_ref[:, O_G + lo:O_G + lo + D_MODEL]) + bg_ref[:, lo:lo + D_MODEL]
            gate_ref[rows, lo:lo + D_MODEL] = _sigmoid(z).astype(BF16)

    _skewed_ffn(x_ref, g1_ref, wup_ref, wdn_ref, act_ref, project)


def _ffn_final_kernel(x1_ref, o_ref, fmix_ref, gate_ref, wa_ref, wb_ref, wout_ref, g2_ref,
                      wup_ref, wdn_ref, gfin_ref, out_ref, act_ref, x2_ref):
    def mix_branches(rows):
        y_a = _dot(o_ref[rows, :], wa_ref[...])
        y_b = _dot(fmix_ref[rows, :], wb_ref[...])
        mix = (gate_ref[rows, :D_MODEL].astype(F32) * y_a
               + gate_ref[rows, D_MODEL:].astype(F32) * y_b)
        x2_ref[rows, :] = x1_ref[rows, :] + _dot(mix.astype(BF16), wout_ref[...])

    def final_norm(k, rows, x3):
        out_ref[rows, :] = _rms(x3, gfin_ref[...])

    _skewed_ffn(x2_ref, g2_ref, wup_ref, wdn_ref, act_ref, final_norm, prepare=mix_branches)


def _mixer_kernel(relb_ref, sink_ref, q_ref, kv_ref, f_ref, bucket_ref, dftc_ref, dfts_ref,
                  o_ref, fmix_ref, bias_ref, kpad_ref, vt_ref, frev_ref, uw_ref, s_ref, p_ref):
    b = pl.program_id(0)
    t = pl.program_id(1)

    @pl.when((b == 0) & (t == 0))
    def _build_bias():
        bucket = bucket_ref[...]
        kj = lax.broadcasted_iota(jnp.int32, (3 * BLK, BLK), 0)
        qi = lax.broadcasted_iota(jnp.int32, (3 * BLK, BLK), 1)
        in_window = jnp.abs(kj - BLK - qi) <= WINDOW
        valid = (in_window, in_window & (kj >= BLK), in_window & (kj < 2 * BLK))
        for h in range(N_HEADS):
            def pick(bk, acc, h=h):
                return jnp.where(bucket == bk, relb_ref[bk, h] * LOG2E, acc)
            base = lax.fori_loop(0, N_BUCKETS, pick, jnp.zeros((3 * BLK, BLK), F32))
            for v in range(3):
                bias_ref[v, h] = jnp.where(valid[v], base, -jnp.inf)

    @pl.when(t == 0)
    def _per_sequence():
        kpad_ref[0:BLK, :] = jnp.zeros((BLK, KV_WIDTH), BF16)
        kpad_ref[BLK:BLK + SEQ, :] = kv_ref[:, :KV_WIDTH]
        kpad_ref[BLK + SEQ:, :] = jnp.zeros((BLK, KV_WIDTH), BF16)
        vt_ref[:, 0:BLK] = jnp.zeros((N_KV * VT_ROWS, BLK), BF16)
        vt_ref[:, BLK + SEQ:] = jnp.zeros((N_KV * VT_ROWS, BLK), BF16)
        for c in range(SEQ // BLK):
            v_t = kv_ref[c * BLK:(c + 1) * BLK, KV_WIDTH:].T
            for kh in range(N_KV):
                vt_ref[kh * VT_ROWS:kh * VT_ROWS + HD, (c + 1) * BLK:(c + 2) * BLK] = (
                    v_t[kh * HD:(kh + 1) * HD])
        for kh in range(N_KV):
            vt_ref[kh * VT_ROWS + HD:(kh + 1) * VT_ROWS, :] = jnp.ones(
                (VT_ROWS - HD, SEQ + 2 * BLK), BF16)
        jj = lax.broadcasted_iota(jnp.int32, (BLK, 2 * BLK), 0)
        cc = lax.broadcasted_iota(jnp.int32, (BLK, 2 * BLK), 1)
        flip = jnp.where(jj + cc == BLK, 1.0, 0.0).astype(BF16)
        for blk in range(HALF_SEQ // BLK):
            src = SEQ - (blk + 1) * BLK
            if blk == 0:
                rev = _dot(flip[:, :BLK], f_ref[src:src + BLK, :])
            else:
                rev = _dot(flip, f_ref[src:src + 2 * BLK, :])
            frev_ref[blk * BLK:(blk + 1) * BLK, :] = rev.astype(BF16)
        first_row = lax.broadcasted_iota(jnp.int32, (V7X_BF16_SUBLANES, FG_DIM), 0) == 0
        for g in range(N_FGROUPS):
            lo = g * FG_DIM
            both = jnp.concatenate([f_ref[0:HALF_SEQ, lo:lo + FG_DIM],
                                    frev_ref[:, lo:lo + FG_DIM]], axis=1)
            r = _dot(both, dftc_ref[...])
            uw_ref[0:HALF_SEQ, lo:lo + FG_DIM] = r[:, :FG_DIM].astype(BF16)
            uw_ref[HALF_SEQ:, lo:lo + FG_DIM] = r[:, FG_DIM:].astype(BF16)
            mid = _dot(f_ref[HALF_SEQ:HALF_SEQ + V7X_BF16_SUBLANES, lo:lo + FG_DIM],
                       dftc_ref[0:FG_DIM, 0:FG_DIM])
            head = uw_ref[HALF_SEQ:HALF_SEQ + V7X_BF16_SUBLANES, lo:lo + FG_DIM]
            uw_ref[HALF_SEQ:HALF_SEQ + V7X_BF16_SUBLANES, lo:lo + FG_DIM] = jnp.where(
                first_row, mid, head.astype(F32)).astype(BF16)

    def block_keys(i):
        n = t * BLOCKS_PER_TILE + i
        variant = jnp.where(n == 0, 1, jnp.where(n == SEQ // BLK - 1, 2, 0))
        return pl.multiple_of(n * BLK, BLK), variant

    def scores(i):
        key0, _ = block_keys(i)
        q_t = q_ref[i * BLK:(i + 1) * BLK, :].T
        zero = jnp.zeros((HD, BLK), BF16)
        cols = []
        for h in range(N_HEADS):
            q_h = q_t[h * HD:(h + 1) * HD]
            cols.append(jnp.concatenate([q_h, zero] if h < GQ else [zero, q_h], axis=0))
        s_ref[i % 2] = _dot(kpad_ref[pl.ds(key0, 3 * BLK), :], jnp.concatenate(cols, axis=1))

    def softmax(i):
        _, variant = block_keys(i)
        buf = i % 2
        sink_terms = []
        for h in range(N_HEADS):
            sink = sink_ref[h] * LOG2E
            cols = slice(h * BLK, (h + 1) * BLK)
            top = s_ref[buf, 0:KEY_CHUNK, cols] + bias_ref[variant, h, 0:KEY_CHUNK, :]
            for k0 in range(KEY_CHUNK, 3 * BLK, KEY_CHUNK):
                top = jnp.maximum(top, s_ref[buf, k0:k0 + KEY_CHUNK, cols]
                                  + bias_ref[variant, h, k0:k0 + KEY_CHUNK, :])
            m = jnp.maximum(jnp.max(top, axis=0, keepdims=True), sink)
            for k0 in range(0, 3 * BLK, KEY_CHUNK):
                z = (s_ref[buf, k0:k0 + KEY_CHUNK, cols] - m) + bias_ref[variant, h, k0:k0 + KEY_CHUNK, :]
                p_ref[buf, k0:k0 + KEY_CHUNK, cols] = jnp.exp2(z).astype(BF16)
            sink_terms.append(jnp.exp2(sink - m))
        return sink_terms

    def weighted_values(i, sink_terms):
        key0, _ = block_keys(i)
        for kh in range(N_KV):
            o_t = _dot(vt_ref[kh * VT_ROWS:(kh + 1) * VT_ROWS, pl.ds(key0, 3 * BLK)],
                       p_ref[i % 2, :, kh * GQ * BLK:(kh + 1) * GQ * BLK])
            den = o_t[HD:HD + 1] + jnp.concatenate(sink_terms[kh * GQ:(kh + 1) * GQ], axis=1)
            o_n = o_t[:HD] / den
            for pr in range(GQ // 2):
                pair_t = jnp.concatenate([o_n[:, 2 * pr * BLK:(2 * pr + 1) * BLK],
                                          o_n[:, (2 * pr + 1) * BLK:(2 * pr + 2) * BLK]], axis=0)
                lane0 = (kh * GQ + 2 * pr) * HD
                o_ref[i * BLK:(i + 1) * BLK, lane0:lane0 + 2 * HD] = pair_t.T.astype(BF16)

    half_tile = ROW_TILE // 2
    scores(0)
    for i in range(BLOCKS_PER_TILE):
        if i + 1 < BLOCKS_PER_TILE:
            scores(i + 1)
        if i < 2:
            rows = slice(i * half_tile, (i + 1) * half_tile)
            freq0 = pl.multiple_of(t * ROW_TILE + i * half_tile, half_tile)
            fmix_ref[rows, :] = _dot(dfts_ref[pl.ds(freq0, half_tile), :], uw_ref[...]).astype(BF16)
        weighted_values(i, softmax(i))


def _resident(shape):
    return pl.BlockSpec(shape, lambda *_: (0,) * len(shape), pipeline_mode=pl.Buffered(1))


def _rows(width):
    return pl.BlockSpec((ROW_TILE, width), lambda i: (i, 0))


def _slab_spec(shape):
    rows, cols = shape
    steps = TOKENS // ROW_TILE
    slab = next(r for r in range(V7X_BF16_SUBLANES, rows + 1, V7X_BF16_SUBLANES)
                if rows % r == 0 and r * steps >= rows)
    return pl.BlockSpec((slab, cols), lambda i: (jnp.minimum(i, rows // slab - 1), 0))


def _ffn_proj(x, g1, wup, wdn, gmix, win, bg, later_weights):
    assert len(later_weights) == N_LATER_WEIGHTS
    slabs = [_slab_spec(w.shape) for w in later_weights]
    hbm = pl.BlockSpec(memory_space=pl.ANY)
    return pl.pallas_call(
        _ffn_proj_kernel,
        grid=(TOKENS // ROW_TILE,),
        in_specs=[_rows(D_MODEL), _resident((1, D_MODEL)), hbm, hbm, _resident((1, D_MODEL)),
                  hbm, _resident((1, 2 * D_MODEL))] + slabs,
        out_specs=[_rows(D_MODEL), _rows(Q_WIDTH), _rows(2 * KV_WIDTH), _rows(F_WIDTH),
                   _rows(2 * D_MODEL)] + slabs,
        out_shape=[jax.ShapeDtypeStruct((TOKENS, D_MODEL), F32),
                   jax.ShapeDtypeStruct((TOKENS, Q_WIDTH), BF16),
                   jax.ShapeDtypeStruct((TOKENS, 2 * KV_WIDTH), BF16),
                   jax.ShapeDtypeStruct((TOKENS, F_WIDTH), BF16),
                   jax.ShapeDtypeStruct((TOKENS, 2 * D_MODEL), BF16)]
        + [jax.ShapeDtypeStruct(w.shape, BF16) for w in later_weights],
        scratch_shapes=[pltpu.VMEM((ROW_TILE, D_FF), BF16),
                        pltpu.VMEM((D_MODEL, 2 * D_FF), BF16),
                        pltpu.VMEM((D_FF, D_MODEL), BF16),
                        pltpu.VMEM((D_MODEL, IN_WIDTH), BF16)],
        compiler_params=pltpu.CompilerParams(dimension_semantics=("arbitrary",),
                                             vmem_limit_bytes=V7X_VMEM_LIMIT_BYTES),
        name="ffn_proj",
    )(x, g1, wup, wdn, gmix, win, bg, *later_weights)


def _ffn_final(x1, o, fmix, gates, wa, wb, wout, g2, wup, wdn, gfin):
    return pl.pallas_call(
        _ffn_final_kernel,
        grid=(TOKENS // ROW_TILE,),
        in_specs=[_rows(D_MODEL), _rows(Q_WIDTH), _rows(F_WIDTH), _rows(2 * D_MODEL),
                  _resident((Q_WIDTH, D_MODEL)), _resident((F_WIDTH, D_MODEL)),
                  _resident((D_MODEL, D_MODEL)), _resident((1, D_MODEL)),
                  _resident((D_MODEL, 2 * D_FF)), _resident((D_FF, D_MODEL)),
                  _resident((1, D_MODEL))],
        out_specs=_rows(D_MODEL),
        out_shape=jax.ShapeDtypeStruct((TOKENS, D_MODEL), F32),
        scratch_shapes=[pltpu.VMEM((ROW_TILE, D_FF), BF16), pltpu.VMEM((ROW_TILE, D_MODEL), F32)],
        compiler_params=pltpu.CompilerParams(dimension_semantics=("arbitrary",),
                                             vmem_limit_bytes=V7X_VMEM_LIMIT_BYTES),
        name="ffn_final",
    )(x1, o, fmix, gates, wa, wb, wout, g2, wup, wdn, gfin)


def _mixer(rel_bias, sink, q, kv, f, bucket, dftc, dfts):
    def tile_rows(width):
        return pl.BlockSpec((ROW_TILE, width), lambda b, t: (b * TILES_PER_SEQ + t, 0))

    def seq_rows(width):
        return pl.BlockSpec((SEQ, width), lambda b, t: (b, 0))

    smem = pl.BlockSpec(memory_space=pltpu.SMEM)
    return pl.pallas_call(
        _mixer_kernel,
        grid=(BATCH, TILES_PER_SEQ),
        in_specs=[smem, smem, tile_rows(Q_WIDTH), seq_rows(2 * KV_WIDTH), seq_rows(F_WIDTH),
                  _resident((3 * BLK, BLK)), _resident((2 * FG_DIM, 2 * FG_DIM)),
                  _resident((SEQ, SEQ))],
        out_specs=[tile_rows(Q_WIDTH), tile_rows(F_WIDTH)],
        out_shape=[jax.ShapeDtypeStruct((TOKENS, Q_WIDTH), BF16),
                   jax.ShapeDtypeStruct((TOKENS, F_WIDTH), BF16)],
        scratch_shapes=[pltpu.VMEM((3, N_HEADS, 3 * BLK, BLK), F32),
                        pltpu.VMEM((SEQ + 2 * BLK, KV_WIDTH), BF16),
                        pltpu.VMEM((N_KV * VT_ROWS, SEQ + 2 * BLK), BF16),
                        pltpu.VMEM((HALF_SEQ, F_WIDTH), BF16),
                        pltpu.VMEM((SEQ, F_WIDTH), BF16),
                        pltpu.VMEM((2, 3 * BLK, N_HEADS * BLK), F32),
                        pltpu.VMEM((2, 3 * BLK, N_HEADS * BLK), BF16)],
        compiler_params=pltpu.CompilerParams(dimension_semantics=("arbitrary", "arbitrary"),
                                             vmem_limit_bytes=V7X_VMEM_LIMIT_BYTES),
        name="mixer",
    )(rel_bias, sink, q, kv, f, bucket, dftc, dfts)


def kernel(x, g_ffn1, w_up1, w_down1, g_mix, w_in, b_gate, sink, rel_bias, w_branch_a, w_branch_b, w_out, g_ffn2, w_up2, w_down2, g_final):
    assert x.shape == (BATCH, SEQ, D_MODEL) and w_up1.shape[0] == DEPTH == 1
    chan, pos = _dft_tables()
    bucket = jnp.asarray(_t5_bucket_table().T)
    dftc = jnp.asarray(chan).astype(BF16)
    dfts = jnp.asarray(pos).astype(BF16)
    row = lambda v: v.reshape(1, -1)
    x0 = x.reshape(TOKENS, D_MODEL)
    x1, q, kv, f, gates, wa, wb, wout, wup2, wdn2 = _ffn_proj(
        x0, row(g_ffn1[0]), w_up1[0], w_down1[0], row(g_mix[0]), w_in[0], row(b_gate[0]),
        [w_branch_a[0], w_branch_b[0], w_out[0], w_up2[0], w_down2[0]])
    o, fmix = _mixer(rel_bias, sink[0], q, kv, f, bucket, dftc, dfts)
    out = _ffn_final(x1, o, fmix, gates, wa, wb, wout, row(g_ffn2[0]), wup2, wdn2, row(g_final))
    return out.reshape(BATCH, SEQ, D_MODEL)
```

```python
import functools

import numpy as np
import jax
import jax.numpy as jnp
from jax import lax
from jax.experimental import pallas as pl
from jax.experimental.pallas import tpu as pltpu

D_MODEL = 1024
BATCH = 8
SEQ = 2048
DEPTH = 1
N_HEADS = 8
N_KV = 2
GQ = N_HEADS // N_KV
HD = 64
Q_WIDTH = N_HEADS * HD
KV_WIDTH = N_KV * HD
WINDOW = 128
BLK = 128
N_FGROUPS = 4
FG_DIM = 128
F_WIDTH = N_FGROUPS * FG_DIM
N_BUCKETS = 32
MAX_DIST = 128
D_FF = 2816
EPS = 1e-6
O_K = Q_WIDTH
O_V = O_K + KV_WIDTH
O_F = O_V + KV_WIDTH
O_G = O_F + F_WIDTH
IN_WIDTH = O_G + 2 * D_MODEL

F32 = jnp.float32
BF16 = jnp.bfloat16

V7X_MXU_DIM = 256
V7X_BF16_SUBLANES = 16
V7X_VMEM_LIMIT_BYTES = 56 * 1024 * 1024

TOKENS = BATCH * SEQ
HALF_SEQ = SEQ // 2
ROW_TILE = 512
TILES_PER_SEQ = SEQ // ROW_TILE
BLOCKS_PER_TILE = ROW_TILE // BLK
FF_CHUNK = V7X_MXU_DIM
FFN_SUBTILES = 2
VT_ROWS = HD + V7X_BF16_SUBLANES
N_LATER_WEIGHTS = 5
STAGE_SLOTS = 4
STAGE_BYTES = 3 * 512 * 1024
KEY_CHUNK = 64
LOG2E = float(np.log2(np.e))


def _t5_bucket_table():
    rel = (np.arange(3 * BLK)[None, :] - BLK) - np.arange(BLK)[:, None]
    half = N_BUCKETS // 2
    max_exact = half // 2
    ret = (rel > 0).astype(np.int32) * half
    n = np.abs(rel)
    n_safe = np.maximum(n, 1).astype(np.float32)
    large = max_exact + (np.log(n_safe / max_exact) / np.log(MAX_DIST / max_exact)
                         * (half - max_exact)).astype(np.int32)
    large = np.minimum(large, half - 1)
    return (ret + np.where(n < max_exact, n, large)).astype(np.int32)


@functools.lru_cache(maxsize=None)
def _dft_tables():
    def cos_sin(n):
        idx = np.arange(n)
        ang = 2.0 * np.pi * ((idx[:, None] * idx[None, :]) % n).astype(np.float64) / n
        return np.cos(ang) / np.sqrt(n), np.sin(ang) / np.sqrt(n)
    cc, sc = cos_sin(FG_DIM)
    cs, ss = cos_sin(SEQ)
    chan = np.block([[cc, sc], [cc, -sc]]).astype(np.float32)
    neg_sin = -ss[:, :HALF_SEQ]
    neg_sin[:, 0] = np.where(np.arange(SEQ) % 2 == 0, 1.0, -1.0) / np.sqrt(SEQ)
    pos = np.concatenate([cs[:, :HALF_SEQ], neg_sin], axis=1).astype(np.float32)
    return chan, pos


def _rms(x, g):
    return x * lax.rsqrt(jnp.mean(x * x, axis=-1, keepdims=True) + EPS) * g


def _sigmoid(z):
    return 1.0 / (1.0 + jnp.exp(-z))


def _dot(a, b):
    return jnp.dot(a, b, preferred_element_type=F32)


def _skewed_ffn(x_ref, g_ref, wup_ref, wdn_ref, act_ref, finish):
    sub = ROW_TILE // FFN_SUBTILES
    rows = [slice(k * sub, (k + 1) * sub) for k in range(FFN_SUBTILES)]
    normed = [None] * FFN_SUBTILES

    def prologue(k):
        normed[k] = _rms(x_ref[rows[k], :], g_ref[...]).astype(BF16)

    def up_chunk(k, c):
        lo = c * FF_CHUNK
        gate = _dot(normed[k], wup_ref[:, lo:lo + FF_CHUNK])
        up = _dot(normed[k], wup_ref[:, D_FF + lo:D_FF + lo + FF_CHUNK])
        act_ref[rows[k], lo:lo + FF_CHUNK] = (gate * _sigmoid(gate) * up).astype(BF16)

    def epilogue(k):
        finish(k, rows[k], x_ref[rows[k], :] + 0.5 * _dot(act_ref[rows[k], :], wdn_ref[...]))

    prologue(0)
    for k in range(FFN_SUBTILES):
        for c in range(D_FF // FF_CHUNK):
            up_chunk(k, c)
            if c == 0 and k + 1 < FFN_SUBTILES:
                prologue(k + 1)
            if c == 1 and k > 0:
                epilogue(k - 1)
    epilogue(FFN_SUBTILES - 1)


def _stage_weight(src_hbm, dst_ref):
    rows, cols = dst_ref.shape
    slab = max(r for r in range(V7X_BF16_SUBLANES, rows + 1, V7X_BF16_SUBLANES)
               if rows % r == 0 and r * cols * 4 <= STAGE_BYTES)
    n_slabs = rows // slab

    def staged(stage_ref, sem):
        def slab_copy(c, slot):
            return pltpu.make_async_copy(src_hbm.at[pl.ds(c * slab, slab), :],
                                         stage_ref.at[slot], sem.at[slot])

        for c in range(min(STAGE_SLOTS - 1, n_slabs)):
            slab_copy(c, c).start()

        def step(c, carry):
            ahead = c + STAGE_SLOTS - 1

            @pl.when(ahead < n_slabs)
            def _start_ahead():
                slab_copy(ahead, ahead % STAGE_SLOTS).start()

            slot = c % STAGE_SLOTS
            slab_copy(c, slot).wait()
            dst_ref[pl.ds(pl.multiple_of(c * slab, slab), slab), :] = stage_ref[slot].astype(BF16)
            return carry

        lax.fori_loop(0, n_slabs, step, 0)

    pl.run_scoped(staged, pltpu.VMEM((STAGE_SLOTS, slab, cols), F32),
                  pltpu.SemaphoreType.DMA((STAGE_SLOTS,)))


def _ffn_proj_kernel(x_ref, g1_ref, wup_hbm, wdn_hbm, gmix_ref, win_hbm, bg_ref, *refs):
    later_f32 = refs[:N_LATER_WEIGHTS]
    x1_ref, q_ref, kv_ref, f_ref, gate_ref = refs[N_LATER_WEIGHTS:N_LATER_WEIGHTS + 5]
    later_bf16 = refs[N_LATER_WEIGHTS + 5:2 * N_LATER_WEIGHTS + 5]
    act_ref, wup_ref, wdn_ref, win_ref = refs[2 * N_LATER_WEIGHTS + 5:]

    @pl.when(pl.program_id(0) == 0)
    def _stage_own_weights():
        _stage_weight(wup_hbm, wup_ref)
        _stage_weight(wdn_hbm, wdn_ref)
        _stage_weight(win_hbm, win_ref)

    for src, dst in zip(later_f32, later_bf16):
        dst[...] = src[...].astype(BF16)

    def project(k, rows, x1):
        x1_ref[rows, :] = x1
        h2 = _rms(x1, gmix_ref[...]).astype(BF16)
        q_ref[rows, :] = (_dot(h2, win_ref[:, 0:O_K]) * (HD ** -0.5 * LOG2E)).astype(BF16)
        kv_ref[rows, :] = _dot(h2, win_ref[:, O_K:O_F]).astype(BF16)
        f_ref[rows, :] = _dot(h2, win_ref[:, O_F:O_G]).astype(BF16)
        for c in range(2):
            lo = c * D_MODEL
            z = _dot(h2, win_ref[:, O_G + lo:O_G + lo + D_MODEL]) + bg_ref[:, lo:lo + D_MODEL]
            gate_ref[rows, lo:lo + D_MODEL] = _sigmoid(z).astype(BF16)

    _skewed_ffn(x_ref, g1_ref, wup_ref, wdn_ref, act_ref, project)


def _ffn_final_kernel(x_ref, g2_ref, wup_ref, wdn_ref, gfin_ref, out_ref, act_ref):
    def final_norm(k, rows, x3):
        out_ref[rows, :] = _rms(x3, gfin_ref[...])

    _skewed_ffn(x_ref, g2_ref, wup_ref, wdn_ref, act_ref, final_norm)


def _mixer_kernel(relb_ref, sink_ref, x1_ref, q_ref, kv_ref, f_ref, gate_ref, bucket_ref,
                  dftc_ref, dfts_ref, wa_ref, wb_ref, wout_ref, out_ref,
                  bias_ref, kpad_ref, vt_ref, frev_ref, uw_ref, o_ref, s_ref, p_ref,
                  fmix_ref, yb_ref):
    b = pl.program_id(0)
    t = pl.program_id(1)

    @pl.when((b == 0) & (t == 0))
    def _build_bias():
        bucket = bucket_ref[...]
        kj = lax.broadcasted_iota(jnp.int32, (3 * BLK, BLK), 0)
        qi = lax.broadcasted_iota(jnp.int32, (3 * BLK, BLK), 1)
        in_window = jnp.abs(kj - BLK - qi) <= WINDOW
        valid = (in_window, in_window & (kj >= BLK), in_window & (kj < 2 * BLK))
        for h in range(N_HEADS):
            def pick(bk, acc, h=h):
                return jnp.where(bucket == bk, relb_ref[bk, h] * LOG2E, acc)
            base = lax.fori_loop(0, N_BUCKETS, pick, jnp.zeros((3 * BLK, BLK), F32))
            for v in range(3):
                bias_ref[v, h] = jnp.where(valid[v], base, -jnp.inf)

    @pl.when(t == 0)
    def _per_sequence():
        kpad_ref[0:BLK, :] = jnp.zeros((BLK, KV_WIDTH), BF16)
        kpad_ref[BLK:BLK + SEQ, :] = kv_ref[:, :KV_WIDTH]
        kpad_ref[BLK + SEQ:, :] = jnp.zeros((BLK, KV_WIDTH), BF16)
        vt_ref[:, 0:BLK] = jnp.zeros((N_KV * VT_ROWS, BLK), BF16)
        vt_ref[:, BLK + SEQ:] = jnp.zeros((N_KV * VT_ROWS, BLK), BF16)
        for c in range(SEQ // BLK):
            v_t = kv_ref[c * BLK:(c + 1) * BLK, KV_WIDTH:].T
            for kh in range(N_KV):
                vt_ref[kh * VT_ROWS:kh * VT_ROWS + HD, (c + 1) * BLK:(c + 2) * BLK] = (
                    v_t[kh * HD:(kh + 1) * HD])
        for kh in range(N_KV):
            vt_ref[kh * VT_ROWS + HD:(kh + 1) * VT_ROWS, :] = jnp.ones(
                (VT_ROWS - HD, SEQ + 2 * BLK), BF16)
        jj = lax.broadcasted_iota(jnp.int32, (BLK, 2 * BLK), 0)
        cc = lax.broadcasted_iota(jnp.int32, (BLK, 2 * BLK), 1)
        flip = jnp.where(jj + cc == BLK, 1.0, 0.0).astype(BF16)
        for blk in range(HALF_SEQ // BLK):
            src = SEQ - (blk + 1) * BLK
            if blk == 0:
                rev = _dot(flip[:, :BLK], f_ref[src:src + BLK, :])
            else:
                rev = _dot(flip, f_ref[src:src + 2 * BLK, :])
            frev_ref[blk * BLK:(blk + 1) * BLK, :] = rev.astype(BF16)
        first_row = lax.broadcasted_iota(jnp.int32, (V7X_BF16_SUBLANES, FG_DIM), 0) == 0
        for g in range(N_FGROUPS):
            lo = g * FG_DIM
            both = jnp.concatenate([f_ref[0:HALF_SEQ, lo:lo + FG_DIM],
                                    frev_ref[:, lo:lo + FG_DIM]], axis=1)
            r = _dot(both, dftc_ref[...])
            uw_ref[0:HALF_SEQ, lo:lo + FG_DIM] = r[:, :FG_DIM].astype(BF16)
            uw_ref[HALF_SEQ:, lo:lo + FG_DIM] = r[:, FG_DIM:].astype(BF16)
            mid = _dot(f_ref[HALF_SEQ:HALF_SEQ + V7X_BF16_SUBLANES, lo:lo + FG_DIM],
                       dftc_ref[0:FG_DIM, 0:FG_DIM])
            head = uw_ref[HALF_SEQ:HALF_SEQ + V7X_BF16_SUBLANES, lo:lo + FG_DIM]
            uw_ref[HALF_SEQ:HALF_SEQ + V7X_BF16_SUBLANES, lo:lo + FG_DIM] = jnp.where(
                first_row, mid, head.astype(F32)).astype(BF16)

    def block_keys(i):
        n = t * BLOCKS_PER_TILE + i
        variant = jnp.where(n == 0, 1, jnp.where(n == SEQ // BLK - 1, 2, 0))
        return pl.multiple_of(n * BLK, BLK), variant

    def scores(i):
        key0, _ = block_keys(i)
        q_t = q_ref[i * BLK:(i + 1) * BLK, :].T
        zero = jnp.zeros((HD, BLK), BF16)
        cols = []
        for h in range(N_HEADS):
            q_h = q_t[h * HD:(h + 1) * HD]
            cols.append(jnp.concatenate([q_h, zero] if h < GQ else [zero, q_h], axis=0))
        s_t = _dot(kpad_ref[pl.ds(key0, 3 * BLK), :], jnp.concatenate(cols, axis=1))
        for h in range(N_HEADS):
            s_ref[i % 2, h] = s_t[:, h * BLK:(h + 1) * BLK]

    def softmax(i):
        _, variant = block_keys(i)
        buf = i % 2
        sink_terms = []
        for h in range(N_HEADS):
            sink = sink_ref[h] * LOG2E
            top = s_ref[buf, h, 0:KEY_CHUNK, :] + bias_ref[variant, h, 0:KEY_CHUNK, :]
            for k0 in range(KEY_CHUNK, 3 * BLK, KEY_CHUNK):
                top = jnp.maximum(top, s_ref[buf, h, k0:k0 + KEY_CHUNK, :]
                                  + bias_ref[variant, h, k0:k0 + KEY_CHUNK, :])
            m = jnp.maximum(jnp.max(top, axis=0, keepdims=True), sink)
            for k0 in range(0, 3 * BLK, KEY_CHUNK):
                z = (s_ref[buf, h, k0:k0 + KEY_CHUNK, :] - m) + bias_ref[variant, h, k0:k0 + KEY_CHUNK, :]
                p_ref[buf, h, k0:k0 + KEY_CHUNK, :] = jnp.exp2(z).astype(BF16)
            sink_terms.append(jnp.exp2(sink - m))
        return sink_terms

    def weighted_values(i, sink_terms):
        key0, _ = block_keys(i)
        for kh in range(N_KV):
            p_group = jnp.concatenate([p_ref[i % 2, kh * GQ + g] for g in range(GQ)], axis=1)
            o_t = _dot(vt_ref[kh * VT_ROWS:(kh + 1) * VT_ROWS, pl.ds(key0, 3 * BLK)], p_group)
            den = o_t[HD:HD + 1] + jnp.concatenate(sink_terms[kh * GQ:(kh + 1) * GQ], axis=1)
            o_n = o_t[:HD] / den
            for pr in range(GQ // 2):
                pair_t = jnp.concatenate([o_n[:, 2 * pr * BLK:(2 * pr + 1) * BLK],
                                          o_n[:, (2 * pr + 1) * BLK:(2 * pr + 2) * BLK]], axis=0)
                lane0 = (kh * GQ + 2 * pr) * HD
                o_ref[i * BLK:(i + 1) * BLK, lane0:lane0 + 2 * HD] = pair_t.T.astype(BF16)

    half_tile = ROW_TILE // 2
    scores(0)
    for i in range(BLOCKS_PER_TILE):
        if i + 1 < BLOCKS_PER_TILE:
            scores(i + 1)
        if i < 2:
            rows = slice(i * half_tile, (i + 1) * half_tile)
            freq0 = pl.multiple_of(t * ROW_TILE + i * half_tile, half_tile)
            fmix_ref[rows, :] = _dot(dfts_ref[pl.ds(freq0, half_tile), :], uw_ref[...]).astype(BF16)
        elif i == 2:
            yb_ref[...] = _dot(fmix_ref[...], wb_ref[...])
        weighted_values(i, softmax(i))

    y_a = _dot(o_ref[...], wa_ref[...])
    mix = (gate_ref[:, :D_MODEL].astype(F32) * y_a + gate_ref[:, D_MODEL:].astype(F32) * yb_ref[...])
    out_ref[...] = x1_ref[...] + _dot(mix.astype(BF16), wout_ref[...])


def _resident(shape):
    return pl.BlockSpec(shape, lambda *_: (0,) * len(shape), pipeline_mode=pl.Buffered(1))


def _rows(width):
    return pl.BlockSpec((ROW_TILE, width), lambda i: (i, 0))


def _slab_spec(shape):
    rows, cols = shape
    steps = TOKENS // ROW_TILE
    slab = next(r for r in range(V7X_BF16_SUBLANES, rows + 1, V7X_BF16_SUBLANES)
                if rows % r == 0 and r * steps >= rows)
    return pl.BlockSpec((slab, cols), lambda i: (jnp.minimum(i, rows // slab - 1), 0))


def _ffn_proj(x, g1, wup, wdn, gmix, win, bg, later_weights):
    assert len(later_weights) == N_LATER_WEIGHTS
    slabs = [_slab_spec(w.shape) for w in later_weights]
    hbm = pl.BlockSpec(memory_space=pl.ANY)
    return pl.pallas_call(
        _ffn_proj_kernel,
        grid=(TOKENS // ROW_TILE,),
        in_specs=[_rows(D_MODEL), _resident((1, D_MODEL)), hbm, hbm, _resident((1, D_MODEL)),
                  hbm, _resident((1, 2 * D_MODEL))] + slabs,
        out_specs=[_rows(D_MODEL), _rows(Q_WIDTH), _rows(2 * KV_WIDTH), _rows(F_WIDTH),
                   _rows(2 * D_MODEL)] + slabs,
        out_shape=[jax.ShapeDtypeStruct((TOKENS, D_MODEL), F32),
                   jax.ShapeDtypeStruct((TOKENS, Q_WIDTH), BF16),
                   jax.ShapeDtypeStruct((TOKENS, 2 * KV_WIDTH), BF16),
                   jax.ShapeDtypeStruct((TOKENS, F_WIDTH), BF16),
                   jax.ShapeDtypeStruct((TOKENS, 2 * D_MODEL), BF16)]
        + [jax.ShapeDtypeStruct(w.shape, BF16) for w in later_weights],
        scratch_shapes=[pltpu.VMEM((ROW_TILE, D_FF), BF16),
                        pltpu.VMEM((D_MODEL, 2 * D_FF), BF16),
                        pltpu.VMEM((D_FF, D_MODEL), BF16),
                        pltpu.VMEM((D_MODEL, IN_WIDTH), BF16)],
        compiler_params=pltpu.CompilerParams(dimension_semantics=("arbitrary",),
                                             vmem_limit_bytes=V7X_VMEM_LIMIT_BYTES),
        name="ffn_proj",
    )(x, g1, wup, wdn, gmix, win, bg, *later_weights)


def _ffn_final(x, g2, wup, wdn, gfin):
    return pl.pallas_call(
        _ffn_final_kernel,
        grid=(TOKENS // ROW_TILE,),
        in_specs=[_rows(D_MODEL), _resident((1, D_MODEL)), _resident((D_MODEL, 2 * D_FF)),
                  _resident((D_FF, D_MODEL)), _resident((1, D_MODEL))],
        out_specs=_rows(D_MODEL),
        out_shape=jax.ShapeDtypeStruct((TOKENS, D_MODEL), F32),
        scratch_shapes=[pltpu.VMEM((ROW_TILE, D_FF), BF16)],
        compiler_params=pltpu.CompilerParams(dimension_semantics=("arbitrary",),
                                             vmem_limit_bytes=V7X_VMEM_LIMIT_BYTES),
        name="ffn_final",
    )(x, g2, wup, wdn, gfin)


def _mixer(rel_bias, sink, x1, q, kv, f, gates, bucket, dftc, dfts, wa, wb, wout):
    def tile_rows(width):
        return pl.BlockSpec((ROW_TILE, width), lambda b, t: (b * TILES_PER_SEQ + t, 0))

    def seq_rows(width):
        return pl.BlockSpec((SEQ, width), lambda b, t: (b, 0))

    smem = pl.BlockSpec(memory_space=pltpu.SMEM)
    return pl.pallas_call(
        _mixer_kernel,
        grid=(BATCH, TILES_PER_SEQ),
        in_specs=[smem, smem, tile_rows(D_MODEL), tile_rows(Q_WIDTH), seq_rows(2 * KV_WIDTH),
                  seq_rows(F_WIDTH), tile_rows(2 * D_MODEL), _resident((3 * BLK, BLK)),
                  _resident((2 * FG_DIM, 2 * FG_DIM)),
                  _resident((SEQ, SEQ)),
                  _resident((Q_WIDTH, D_MODEL)), _resident((F_WIDTH, D_MODEL)),
                  _resident((D_MODEL, D_MODEL))],
        out_specs=tile_rows(D_MODEL),
        out_shape=jax.ShapeDtypeStruct((TOKENS, D_MODEL), F32),
        scratch_shapes=[pltpu.VMEM((3, N_HEADS, 3 * BLK, BLK), F32),
                        pltpu.VMEM((SEQ + 2 * BLK, KV_WIDTH), BF16),
                        pltpu.VMEM((N_KV * VT_ROWS, SEQ + 2 * BLK), BF16),
                        pltpu.VMEM((HALF_SEQ, F_WIDTH), BF16),
                        pltpu.VMEM((SEQ, F_WIDTH), BF16),
                        pltpu.VMEM((ROW_TILE, Q_WIDTH), BF16),
                        pltpu.VMEM((2, N_HEADS, 3 * BLK, BLK), F32),
                        pltpu.VMEM((2, N_HEADS, 3 * BLK, BLK), BF16),
                        pltpu.VMEM((ROW_TILE, F_WIDTH), BF16),
                        pltpu.VMEM((ROW_TILE, D_MODEL), F32)],
        compiler_params=pltpu.CompilerParams(dimension_semantics=("arbitrary", "arbitrary"),
                                             vmem_limit_bytes=V7X_VMEM_LIMIT_BYTES),
        name="mixer",
    )(rel_bias, sink, x1, q, kv, f, gates, bucket, dftc, dfts, wa, wb, wout)


def kernel(x, g_ffn1, w_up1, w_down1, g_mix, w_in, b_gate, sink, rel_bias, w_branch_a, w_branch_b, w_out, g_ffn2, w_up2, w_down2, g_final):
    assert x.shape == (BATCH, SEQ, D_MODEL) and w_up1.shape[0] == DEPTH == 1
    chan, pos = _dft_tables()
    bucket = jnp.asarray(_t5_bucket_table().T)
    dftc = jnp.asarray(chan).astype(BF16)
    dfts = jnp.asarray(pos).astype(BF16)
    row = lambda v: v.reshape(1, -1)
    x0 = x.reshape(TOKENS, D_MODEL)
    x1, q, kv, f, gates, wa, wb, wout, wup2, wdn2 = _ffn_proj(
        x0, row(g_ffn1[0]), w_up1[0], w_down1[0], row(g_mix[0]), w_in[0], row(b_gate[0]),
        [w_branch_a[0], w_branch_b[0], w_out[0], w_up2[0], w_down2[0]])
    x2 = _mixer(rel_bias, sink[0], x1, q, kv, f, gates, bucket, dftc, dfts, wa, wb, wout)
    out = _ffn_final(x2, row(g_ffn2[0]), wup2, wdn2, row(g_final))
    return out.reshape(BATCH, SEQ, D_MODEL)
```

```python
import functools

import numpy as np
import jax
import jax.numpy as jnp
from jax import lax
from jax.experimental import pallas as pl
from jax.experimental.pallas import tpu as pltpu

D_MODEL = 1024
BATCH = 8
SEQ = 2048
DEPTH = 1
N_HEADS = 8
N_KV = 2
GQ = N_HEADS // N_KV
HD = 64
Q_WIDTH = N_HEADS * HD
KV_WIDTH = N_KV * HD
WINDOW = 128
BLK = 128
N_FGROUPS = 4
FG_DIM = 128
F_WIDTH = N_FGROUPS * FG_DIM
N_BUCKETS = 32
MAX_DIST = 128
D_FF = 2816
EPS = 1e-6
O_K = Q_WIDTH
O_V = O_K + KV_WIDTH
O_F = O_V + KV_WIDTH
O_G = O_F + F_WIDTH
IN_WIDTH = O_G + 2 * D_MODEL

F32 = jnp.float32
BF16 = jnp.bfloat16

V7X_MXU_DIM = 256
V7X_BF16_SUBLANES = 16
V7X_VMEM_LIMIT_BYTES = 56 * 1024 * 1024

TOKENS = BATCH * SEQ
HALF_SEQ = SEQ // 2
ROW_TILE = 512
FINAL_ROW_TILE = 1024
TILES_PER_SEQ = SEQ // ROW_TILE
BLOCKS_PER_TILE = ROW_TILE // BLK
FF_CHUNK = V7X_MXU_DIM
FFN_SUB_ROWS = 256
VT_ROWS = HD + V7X_BF16_SUBLANES
N_LATER_WEIGHTS = 5
STAGE_SLOTS = 4
STAGE_BYTES = 3 * 512 * 1024
KEY_CHUNK = 64
LOG2E = float(np.log2(np.e))


def _t5_bucket_table():
    rel = (np.arange(3 * BLK)[None, :] - BLK) - np.arange(BLK)[:, None]
    half = N_BUCKETS // 2
    max_exact = half // 2
    ret = (rel > 0).astype(np.int32) * half
    n = np.abs(rel)
    n_safe = np.maximum(n, 1).astype(np.float32)
    large = max_exact + (np.log(n_safe / max_exact) / np.log(MAX_DIST / max_exact)
                         * (half - max_exact)).astype(np.int32)
    large = np.minimum(large, half - 1)
    return (ret + np.where(n < max_exact, n, large)).astype(np.int32)


@functools.lru_cache(maxsize=None)
def _dft_tables():
    def cos_sin(n):
        idx = np.arange(n)
        ang = 2.0 * np.pi * ((idx[:, None] * idx[None, :]) % n).astype(np.float64) / n
        return np.cos(ang) / np.sqrt(n), np.sin(ang) / np.sqrt(n)
    cc, sc = cos_sin(FG_DIM)
    cs, ss = cos_sin(SEQ)
    chan = np.block([[cc, sc], [cc, -sc]]).astype(np.float32)
    neg_sin = -ss[:, :HALF_SEQ]
    neg_sin[:, 0] = np.where(np.arange(SEQ) % 2 == 0, 1.0, -1.0) / np.sqrt(SEQ)
    pos = np.concatenate([cs[:, :HALF_SEQ], neg_sin], axis=1).astype(np.float32)
    return chan, pos


def _rms(x, g):
    return x * lax.rsqrt(jnp.mean(x * x, axis=-1, keepdims=True) + EPS) * g


def _sigmoid(z):
    return 1.0 / (1.0 + jnp.exp2(z * -LOG2E))


def _dot(a, b):
    return jnp.dot(a, b, preferred_element_type=F32)


def _skewed_ffn(x_ref, g_ref, wup_ref, wdn_ref, act_ref, finish):
    n_sub = x_ref.shape[0] // FFN_SUB_ROWS
    rows = [slice(k * FFN_SUB_ROWS, (k + 1) * FFN_SUB_ROWS) for k in range(n_sub)]
    normed = [None] * n_sub

    def prologue(k):
        normed[k] = _rms(x_ref[rows[k], :], g_ref[...]).astype(BF16)

    def up_chunk(k, c):
        lo = c * FF_CHUNK
        gate = _dot(normed[k], wup_ref[:, lo:lo + FF_CHUNK])
        up = _dot(normed[k], wup_ref[:, D_FF + lo:D_FF + lo + FF_CHUNK])
        act_ref[rows[k], lo:lo + FF_CHUNK] = (gate * _sigmoid(gate) * up).astype(BF16)

    def epilogue(k):
        finish(k, rows[k], x_ref[rows[k], :] + 0.5 * _dot(act_ref[rows[k], :], wdn_ref[...]))

    prologue(0)
    for k in range(n_sub):
        for c in range(D_FF // FF_CHUNK):
            up_chunk(k, c)
            if c == 0 and k + 1 < n_sub:
                prologue(k + 1)
            if c == 1 and k > 0:
                epilogue(k - 1)
    epilogue(n_sub - 1)


def _stage_weight(src_hbm, dst_ref):
    rows, cols = dst_ref.shape
    slab = max(r for r in range(V7X_BF16_SUBLANES, rows + 1, V7X_BF16_SUBLANES)
               if rows % r == 0 and r * cols * 4 <= STAGE_BYTES)
    n_slabs = rows // slab

    def staged(stage_ref, sem):
        def slab_copy(c, slot):
            return pltpu.make_async_copy(src_hbm.at[pl.ds(c * slab, slab), :],
                                         stage_ref.at[slot], sem.at[slot])

        for c in range(min(STAGE_SLOTS - 1, n_slabs)):
            slab_copy(c, c).start()

        def step(c, carry):
            ahead = c + STAGE_SLOTS - 1

            @pl.when(ahead < n_slabs)
            def _start_ahead():
                slab_copy(ahead, ahead % STAGE_SLOTS).start()

            slot = c % STAGE_SLOTS
            slab_copy(c, slot).wait()
            dst_ref[pl.ds(pl.multiple_of(c * slab, slab), slab), :] = stage_ref[slot].astype(BF16)
            return carry

        lax.fori_loop(0, n_slabs, step, 0)

    pl.run_scoped(staged, pltpu.VMEM((STAGE_SLOTS, slab, cols), F32),
                  pltpu.SemaphoreType.DMA((STAGE_SLOTS,)))


def _ffn_proj_kernel(x_ref, g1_ref, wup_hbm, wdn_hbm, gmix_ref, win_hbm, bg_ref, *refs):
    later_f32 = refs[:N_LATER_WEIGHTS]
    x1_ref, q_ref, kv_ref, f_ref, gate_ref = refs[N_LATER_WEIGHTS:N_LATER_WEIGHTS + 5]
    later_bf16 = refs[N_LATER_WEIGHTS + 5:2 * N_LATER_WEIGHTS + 5]
    act_ref, wup_ref, wdn_ref, win_ref = refs[2 * N_LATER_WEIGHTS + 5:]

    @pl.when(pl.program_id(0) == 0)
    def _stage_own_weights():
        _stage_weight(wup_hbm, wup_ref)
        _stage_weight(wdn_hbm, wdn_ref)
        _stage_weight(win_hbm, win_ref)

    for src, dst in zip(later_f32, later_bf16):
        dst[...] = src[...].astype(BF16)

    def project(k, rows, x1):
        x1_ref[rows, :] = x1
        h2 = _rms(x1, gmix_ref[...]).astype(BF16)
        q_ref[rows, :] = (_dot(h2, win_ref[:, 0:O_K]) * (HD ** -0.5 * LOG2E)).astype(BF16)
        kv_ref[rows, :] = _dot(h2, win_ref[:, O_K:O_F]).astype(BF16)
        f_ref[rows, :] = _dot(h2, win_ref[:, O_F:O_G]).astype(BF16)
        for c in range(2):
            lo = c * D_MODEL
            z = _dot(h2, win_ref[:, O_G + lo:O_G + lo + D_MODEL]) + bg_ref[:, lo:lo + D_MODEL]
            gate_ref[rows, lo:lo + D_MODEL] = _sigmoid(z).astype(BF16)

    _skewed_ffn(x_ref, g1_ref, wup_ref, wdn_ref, act_ref, project)


def _ffn_final_kernel(x_ref, g2_ref, wup_ref, wdn_ref, gfin_ref, out_ref, act_ref):
    def final_norm(k, rows, x3):
        out_ref[rows, :] = _rms(x3, gfin_ref[...])

    _skewed_ffn(x_ref, g2_ref, wup_ref, wdn_ref, act_ref, final_norm)


def _mixer_kernel(relb_ref, sink_ref, x1_ref, q_ref, kv_ref, f_ref, gate_ref, bucket_ref,
                  dftc_ref, dfts_ref, wa_ref, wb_ref, wout_ref, out_ref,
                  bias_ref, kpad_ref, vt_ref, frev_ref, uw_ref, o_ref, s_ref, p_ref,
                  fmix_ref, yb_ref):
    b = pl.program_id(0)
    t = pl.program_id(1)

    @pl.when((b == 0) & (t == 0))
    def _build_bias():
        bucket = bucket_ref[...]
        kj = lax.broadcasted_iota(jnp.int32, (3 * BLK, BLK), 0)
        qi = lax.broadcasted_iota(jnp.int32, (3 * BLK, BLK), 1)
        in_window = jnp.abs(kj - BLK - qi) <= WINDOW
        valid = (in_window, in_window & (kj >= BLK), in_window & (kj < 2 * BLK))
        for h in range(N_HEADS):
            def pick(bk, acc, h=h):
                return jnp.where(bucket == bk, relb_ref[bk, h] * LOG2E, acc)
            base = lax.fori_loop(0, N_BUCKETS, pick, jnp.zeros((3 * BLK, BLK), F32))
            for v in range(3):
                bias_ref[v, h] = jnp.where(valid[v], base, -jnp.inf)

    @pl.when(t == 0)
    def _per_sequence():
        kpad_ref[0:BLK, :] = jnp.zeros((BLK, KV_WIDTH), BF16)
        kpad_ref[BLK:BLK + SEQ, :] = kv_ref[:, :KV_WIDTH]
        kpad_ref[BLK + SEQ:, :] = jnp.zeros((BLK, KV_WIDTH), BF16)
        vt_ref[:, 0:BLK] = jnp.zeros((N_KV * VT_ROWS, BLK), BF16)
        vt_ref[:, BLK + SEQ:] = jnp.zeros((N_KV * VT_ROWS, BLK), BF16)
        for c in range(SEQ // BLK):
            v_t = kv_ref[c * BLK:(c + 1) * BLK, KV_WIDTH:].T
            for kh in range(N_KV):
                vt_ref[kh * VT_ROWS:kh * VT_ROWS + HD, (c + 1) * BLK:(c + 2) * BLK] = (
                    v_t[kh * HD:(kh + 1) * HD])
        for kh in range(N_KV):
            vt_ref[kh * VT_ROWS + HD:(kh + 1) * VT_ROWS, :] = jnp.ones(
                (VT_ROWS - HD, SEQ + 2 * BLK), BF16)
        jj = lax.broadcasted_iota(jnp.int32, (BLK, 2 * BLK), 0)
        cc = lax.broadcasted_iota(jnp.int32, (BLK, 2 * BLK), 1)
        flip = jnp.where(jj + cc == BLK, 1.0, 0.0).astype(BF16)
        for blk in range(HALF_SEQ // BLK):
            src = SEQ - (blk + 1) * BLK
            if blk == 0:
                rev = _dot(flip[:, :BLK], f_ref[src:src + BLK, :])
            else:
                rev = _dot(flip, f_ref[src:src + 2 * BLK, :])
            frev_ref[blk * BLK:(blk + 1) * BLK, :] = rev.astype(BF16)
        first_row = lax.broadcasted_iota(jnp.int32, (V7X_BF16_SUBLANES, FG_DIM), 0) == 0
        for g in range(N_FGROUPS):
            lo = g * FG_DIM
            both = jnp.concatenate([f_ref[0:HALF_SEQ, lo:lo + FG_DIM],
                                    frev_ref[:, lo:lo + FG_DIM]], axis=1)
            r = _dot(both, dftc_ref[...])
            uw_ref[0:HALF_SEQ, lo:lo + FG_DIM] = r[:, :FG_DIM].astype(BF16)
            uw_ref[HALF_SEQ:, lo:lo + FG_DIM] = r[:, FG_DIM:].astype(BF16)
            mid = _dot(f_ref[HALF_SEQ:HALF_SEQ + V7X_BF16_SUBLANES, lo:lo + FG_DIM],
                       dftc_ref[0:FG_DIM, 0:FG_DIM])
            head = uw_ref[HALF_SEQ:HALF_SEQ + V7X_BF16_SUBLANES, lo:lo + FG_DIM]
            uw_ref[HALF_SEQ:HALF_SEQ + V7X_BF16_SUBLANES, lo:lo + FG_DIM] = jnp.where(
                first_row, mid, head.astype(F32)).astype(BF16)

    def block_keys(i):
        n = t * BLOCKS_PER_TILE + i
        variant = jnp.where(n == 0, 1, jnp.where(n == SEQ // BLK - 1, 2, 0))
        return pl.multiple_of(n * BLK, BLK), variant

    def scores(i):
        key0, _ = block_keys(i)
        q_t = q_ref[i * BLK:(i + 1) * BLK, :].T
        zero = jnp.zeros((HD, BLK), BF16)
        cols = []
        for h in range(N_HEADS):
            q_h = q_t[h * HD:(h + 1) * HD]
            cols.append(jnp.concatenate([q_h, zero] if h < GQ else [zero, q_h], axis=0))
        s_t = _dot(kpad_ref[pl.ds(key0, 3 * BLK), :], jnp.concatenate(cols, axis=1))
        for h in range(N_HEADS):
            s_ref[i % 2, h] = s_t[:, h * BLK:(h + 1) * BLK]

    def softmax(i):
        _, variant = block_keys(i)
        buf = i % 2
        sink_terms = []
        for h in range(N_HEADS):
            sink = sink_ref[h] * LOG2E
            top = s_ref[buf, h, 0:KEY_CHUNK, :] + bias_ref[variant, h, 0:KEY_CHUNK, :]
            for k0 in range(KEY_CHUNK, 3 * BLK, KEY_CHUNK):
                top = jnp.maximum(top, s_ref[buf, h, k0:k0 + KEY_CHUNK, :]
                                  + bias_ref[variant, h, k0:k0 + KEY_CHUNK, :])
            m = jnp.maximum(jnp.max(top, axis=0, keepdims=True), sink)
            for k0 in range(0, 3 * BLK, KEY_CHUNK):
                z = (s_ref[buf, h, k0:k0 + KEY_CHUNK, :] - m) + bias_ref[variant, h, k0:k0 + KEY_CHUNK, :]
                p_ref[buf, h, k0:k0 + KEY_CHUNK, :] = jnp.exp2(z).astype(BF16)
            sink_terms.append(jnp.exp2(sink - m))
        return sink_terms

    def weighted_values(i, sink_terms):
        key0, _ = block_keys(i)
        for kh in range(N_KV):
            p_group = jnp.concatenate([p_ref[i % 2, kh * GQ + g] for g in range(GQ)], axis=1)
            o_t = _dot(vt_ref[kh * VT_ROWS:(kh + 1) * VT_ROWS, pl.ds(key0, 3 * BLK)], p_group)
            den = o_t[HD:HD + 1] + jnp.concatenate(sink_terms[kh * GQ:(kh + 1) * GQ], axis=1)
            o_n = o_t[:HD] / den
            for pr in range(GQ // 2):
                pair_t = jnp.concatenate([o_n[:, 2 * pr * BLK:(2 * pr + 1) * BLK],
                                          o_n[:, (2 * pr + 1) * BLK:(2 * pr + 2) * BLK]], axis=0)
                lane0 = (kh * GQ + 2 * pr) * HD
                o_ref[i * BLK:(i + 1) * BLK, lane0:lane0 + 2 * HD] = pair_t.T.astype(BF16)

    half_tile = ROW_TILE // 2
    scores(0)
    for i in range(BLOCKS_PER_TILE):
        if i + 1 < BLOCKS_PER_TILE:
            scores(i + 1)
        if i < 2:
            rows = slice(i * half_tile, (i + 1) * half_tile)
            freq0 = pl.multiple_of(t * ROW_TILE + i * half_tile, half_tile)
            fmix_ref[rows, :] = _dot(dfts_ref[pl.ds(freq0, half_tile), :], uw_ref[...]).astype(BF16)
        elif i == 2:
            yb_ref[...] = _dot(fmix_ref[...], wb_ref[...])
        weighted_values(i, softmax(i))

    y_a = _dot(o_ref[...], wa_ref[...])
    mix = (gate_ref[:, :D_MODEL].astype(F32) * y_a + gate_ref[:, D_MODEL:].astype(F32) * yb_ref[...])
    out_ref[...] = x1_ref[...] + _dot(mix.astype(BF16), wout_ref[...])


def _resident(shape):
    return pl.BlockSpec(shape, lambda *_: (0,) * len(shape), pipeline_mode=pl.Buffered(1))


def _rows(width, tile=ROW_TILE):
    return pl.BlockSpec((tile, width), lambda i: (i, 0))


def _slab_spec(shape):
    rows, cols = shape
    steps = TOKENS // ROW_TILE
    slab = next(r for r in range(V7X_BF16_SUBLANES, rows + 1, V7X_BF16_SUBLANES)
                if rows % r == 0 and r * steps >= rows)
    return pl.BlockSpec((slab, cols), lambda i: (jnp.minimum(i, rows // slab - 1), 0))


def _ffn_proj(x, g1, wup, wdn, gmix, win, bg, later_weights):
    assert len(later_weights) == N_LATER_WEIGHTS
    slabs = [_slab_spec(w.shape) for w in later_weights]
    hbm = pl.BlockSpec(memory_space=pl.ANY)
    return pl.pallas_call(
        _ffn_proj_kernel,
        grid=(TOKENS // ROW_TILE,),
        in_specs=[_rows(D_MODEL), _resident((1, D_MODEL)), hbm, hbm, _resident((1, D_MODEL)),
                  hbm, _resident((1, 2 * D_MODEL))] + slabs,
        out_specs=[_rows(D_MODEL), _rows(Q_WIDTH), _rows(2 * KV_WIDTH), _rows(F_WIDTH),
                   _rows(2 * D_MODEL)] + slabs,
        out_shape=[jax.ShapeDtypeStruct((TOKENS, D_MODEL), F32),
                   jax.ShapeDtypeStruct((TOKENS, Q_WIDTH), BF16),
                   jax.ShapeDtypeStruct((TOKENS, 2 * KV_WIDTH), BF16),
                   jax.ShapeDtypeStruct((TOKENS, F_WIDTH), BF16),
                   jax.ShapeDtypeStruct((TOKENS, 2 * D_MODEL), BF16)]
        + [jax.ShapeDtypeStruct(w.shape, BF16) for w in later_weights],
        scratch_shapes=[pltpu.VMEM((ROW_TILE, D_FF), BF16),
                        pltpu.VMEM((D_MODEL, 2 * D_FF), BF16),
                        pltpu.VMEM((D_FF, D_MODEL), BF16),
                        pltpu.VMEM((D_MODEL, IN_WIDTH), BF16)],
        compiler_params=pltpu.CompilerParams(dimension_semantics=("arbitrary",),
                                             vmem_limit_bytes=V7X_VMEM_LIMIT_BYTES),
        name="ffn_proj",
    )(x, g1, wup, wdn, gmix, win, bg, *later_weights)


def _ffn_final(x, g2, wup, wdn, gfin):
    return pl.pallas_call(
        _ffn_final_kernel,
        grid=(TOKENS // FINAL_ROW_TILE,),
        in_specs=[_rows(D_MODEL, FINAL_ROW_TILE), _resident((1, D_MODEL)),
                  _resident((D_MODEL, 2 * D_FF)), _resident((D_FF, D_MODEL)),
                  _resident((1, D_MODEL))],
        out_specs=_rows(D_MODEL, FINAL_ROW_TILE),
        out_shape=jax.ShapeDtypeStruct((TOKENS, D_MODEL), F32),
        scratch_shapes=[pltpu.VMEM((FINAL_ROW_TILE, D_FF), BF16)],
        compiler_params=pltpu.CompilerParams(dimension_semantics=("arbitrary",),
                                             vmem_limit_bytes=V7X_VMEM_LIMIT_BYTES),
        name="ffn_final",
    )(x, g2, wup, wdn, gfin)


def _mixer(rel_bias, sink, x1, q, kv, f, gates, bucket, dftc, dfts, wa, wb, wout):
    def tile_rows(width):
        return pl.BlockSpec((ROW_TILE, width), lambda b, t: (b * TILES_PER_SEQ + t, 0))

    def seq_rows(width):
        return pl.BlockSpec((SEQ, width), lambda b, t: (b, 0))

    smem = pl.BlockSpec(memory_space=pltpu.SMEM)
    return pl.pallas_call(
        _mixer_kernel,
        grid=(BATCH, TILES_PER_SEQ),
        in_specs=[smem, smem, tile_rows(D_MODEL), tile_rows(Q_WIDTH), seq_rows(2 * KV_WIDTH),
                  seq_rows(F_WIDTH), tile_rows(2 * D_MODEL), _resident((3 * BLK, BLK)),
                  _resident((2 * FG_DIM, 2 * FG_DIM)),
                  _resident((SEQ, SEQ)),
                  _resident((Q_WIDTH, D_MODEL)), _resident((F_WIDTH, D_MODEL)),
                  _resident((D_MODEL, D_MODEL))],
        out_specs=tile_rows(D_MODEL),
        out_shape=jax.ShapeDtypeStruct((TOKENS, D_MODEL), F32),
        scratch_shapes=[pltpu.VMEM((3, N_HEADS, 3 * BLK, BLK), F32),
                        pltpu.VMEM((SEQ + 2 * BLK, KV_WIDTH), BF16),
                        pltpu.VMEM((N_KV * VT_ROWS, SEQ + 2 * BLK), BF16),
                        pltpu.VMEM((HALF_SEQ, F_WIDTH), BF16),
                        pltpu.VMEM((SEQ, F_WIDTH), BF16),
                        pltpu.VMEM((ROW_TILE, Q_WIDTH), BF16),
                        pltpu.VMEM((2, N_HEADS, 3 * BLK, BLK), F32),
                        pltpu.VMEM((2, N_HEADS, 3 * BLK, BLK), BF16),
                        pltpu.VMEM((ROW_TILE, F_WIDTH), BF16),
                        pltpu.VMEM((ROW_TILE, D_MODEL), F32)],
        compiler_params=pltpu.CompilerParams(dimension_semantics=("arbitrary", "arbitrary"),
                                             vmem_limit_bytes=V7X_VMEM_LIMIT_BYTES),
        name="mixer",
    )(rel_bias, sink, x1, q, kv, f, gates, bucket, dftc, dfts, wa, wb, wout)


def kernel(x, g_ffn1, w_up1, w_down1, g_mix, w_in, b_gate, sink, rel_bias, w_branch_a, w_branch_b, w_out, g_ffn2, w_up2, w_down2, g_final):
    assert x.shape == (BATCH, SEQ, D_MODEL) and w_up1.shape[0] == DEPTH == 1
    chan, pos = _dft_tables()
    bucket = jnp.asarray(_t5_bucket_table().T)
    dftc = jnp.asarray(chan).astype(BF16)
    dfts = jnp.asarray(pos).astype(BF16)
    row = lambda v: v.reshape(1, -1)
    x0 = x.reshape(TOKENS, D_MODEL)
    x1, q, kv, f, gates, wa, wb, wout, wup2, wdn2 = _ffn_proj(
        x0, row(g_ffn1[0]), w_up1[0], w_down1[0], row(g_mix[0]), w_in[0], row(b_gate[0]),
        [w_branch_a[0], w_branch_b[0], w_out[0], w_up2[0], w_down2[0]])
    x2 = _mixer(rel_bias, sink[0], x1, q, kv, f, gates, bucket, dftc, dfts, wa, wb, wout)
    out = _ffn_final(x2, row(g_ffn2[0]), wup2, wdn2, row(g_final))
    return out.reshape(BATCH, SEQ, D_MODEL)
```

```python
import functools

import numpy as np
import jax
import jax.numpy as jnp
from jax import lax
from jax.experimental import pallas as pl
from jax.experimental.pallas import tpu as pltpu

D_MODEL = 1024
BATCH = 8
SEQ = 2048
DEPTH = 1
N_HEADS = 8
N_KV = 2
GQ = N_HEADS // N_KV
HD = 64
Q_WIDTH = N_HEADS * HD
KV_WIDTH = N_KV * HD
WINDOW = 128
BLK = 128
N_FGROUPS = 4
FG_DIM = 128
F_WIDTH = N_FGROUPS * FG_DIM
N_BUCKETS = 32
MAX_DIST = 128
D_FF = 2816
EPS = 1e-6
O_K = Q_WIDTH
O_V = O_K + KV_WIDTH
O_F = O_V + KV_WIDTH
O_G = O_F + F_WIDTH
IN_WIDTH = O_G + 2 * D_MODEL

F32 = jnp.float32
BF16 = jnp.bfloat16

V7X_MXU_DIM = 256
V7X_BF16_SUBLANES = 16
V7X_VMEM_LIMIT_BYTES = 56 * 1024 * 1024

TOKENS = BATCH * SEQ
HALF_SEQ = SEQ // 2
ROW_TILE = 512
FINAL_ROW_TILE = 1024
TILES_PER_SEQ = SEQ // ROW_TILE
BLOCKS_PER_TILE = ROW_TILE // BLK
FF_CHUNK = V7X_MXU_DIM
FFN_SUB_ROWS = 256
VT_ROWS = HD + V7X_BF16_SUBLANES
N_LATER_WEIGHTS = 5
STAGE_SLOTS = 4
STAGE_BYTES = 3 * 512 * 1024
KEY_CHUNK = 64
LOG2E = float(np.log2(np.e))


def _t5_bucket_table():
    rel = (np.arange(3 * BLK)[None, :] - BLK) - np.arange(BLK)[:, None]
    half = N_BUCKETS // 2
    max_exact = half // 2
    ret = (rel > 0).astype(np.int32) * half
    n = np.abs(rel)
    n_safe = np.maximum(n, 1).astype(np.float32)
    large = max_exact + (np.log(n_safe / max_exact) / np.log(MAX_DIST / max_exact)
                         * (half - max_exact)).astype(np.int32)
    large = np.minimum(large, half - 1)
    return (ret + np.where(n < max_exact, n, large)).astype(np.int32)


@functools.lru_cache(maxsize=None)
def _dft_tables():
    def cos_sin(n):
        idx = np.arange(n)
        ang = 2.0 * np.pi * ((idx[:, None] * idx[None, :]) % n).astype(np.float64) / n
        return np.cos(ang) / np.sqrt(n), np.sin(ang) / np.sqrt(n)
    cc, sc = cos_sin(FG_DIM)
    cs, ss = cos_sin(SEQ)
    chan = np.block([[cc, sc], [cc, -sc]]).astype(np.float32)
    neg_sin = -ss[:, :HALF_SEQ]
    neg_sin[:, 0] = np.where(np.arange(SEQ) % 2 == 0, 1.0, -1.0) / np.sqrt(SEQ)
    pos = np.concatenate([cs[:, :HALF_SEQ], neg_sin], axis=1).astype(np.float32)
    return chan, pos


def _rms(x, g):
    return x * lax.rsqrt(jnp.mean(x * x, axis=-1, keepdims=True) + EPS) * g


def _sigmoid(z):
    return 1.0 / (1.0 + jnp.exp2(z * -LOG2E))


def _dot(a, b):
    return jnp.dot(a, b, preferred_element_type=F32)


def _skewed_ffn(x_ref, g_ref, wup_ref, wdn_ref, act_ref, finish):
    n_sub = x_ref.shape[0] // FFN_SUB_ROWS
    rows = [slice(k * FFN_SUB_ROWS, (k + 1) * FFN_SUB_ROWS) for k in range(n_sub)]
    normed = [None] * n_sub

    def prologue(k):
        normed[k] = _rms(x_ref[rows[k], :], g_ref[...]).astype(BF16)

    def up_chunk(k, c):
        lo = c * FF_CHUNK
        gate = _dot(normed[k], wup_ref[:, lo:lo + FF_CHUNK])
        up = _dot(normed[k], wup_ref[:, D_FF + lo:D_FF + lo + FF_CHUNK])
        act_ref[rows[k], lo:lo + FF_CHUNK] = (gate * _sigmoid(gate) * up).astype(BF16)

    def epilogue(k):
        finish(k, rows[k], x_ref[rows[k], :] + 0.5 * _dot(act_ref[rows[k], :], wdn_ref[...]))

    prologue(0)
    for k in range(n_sub):
        for c in range(D_FF // FF_CHUNK):
            up_chunk(k, c)
            if c == 0 and k + 1 < n_sub:
                prologue(k + 1)
            if c == 1 and k > 0:
                epilogue(k - 1)
    epilogue(n_sub - 1)


def _stage_weight(src_hbm, dst_ref):
    rows, cols = dst_ref.shape
    slab = max(r for r in range(V7X_BF16_SUBLANES, rows + 1, V7X_BF16_SUBLANES)
               if rows % r == 0 and r * cols * 4 <= STAGE_BYTES)
    n_slabs = rows // slab

    def staged(stage_ref, sem):
        def slab_copy(c, slot):
            return pltpu.make_async_copy(src_hbm.at[pl.ds(c * slab, slab), :],
                                         stage_ref.at[slot], sem.at[slot])

        for c in range(min(STAGE_SLOTS - 1, n_slabs)):
            slab_copy(c, c).start()

        def step(c, carry):
            ahead = c + STAGE_SLOTS - 1

            @pl.when(ahead < n_slabs)
            def _start_ahead():
                slab_copy(ahead, ahead % STAGE_SLOTS).start()

            slot = c % STAGE_SLOTS
            slab_copy(c, slot).wait()
            dst_ref[pl.ds(pl.multiple_of(c * slab, slab), slab), :] = stage_ref[slot].astype(BF16)
            return carry

        lax.fori_loop(0, n_slabs, step, 0)

    pl.run_scoped(staged, pltpu.VMEM((STAGE_SLOTS, slab, cols), F32),
                  pltpu.SemaphoreType.DMA((STAGE_SLOTS,)))


def _ffn_proj_kernel(x_ref, g1_ref, wup_hbm, wdn_hbm, gmix_ref, win_hbm, bg_ref, *refs):
    later_f32 = refs[:N_LATER_WEIGHTS]
    x1_ref, q_ref, kv_ref, f_ref, gate_ref = refs[N_LATER_WEIGHTS:N_LATER_WEIGHTS + 5]
    later_bf16 = refs[N_LATER_WEIGHTS + 5:2 * N_LATER_WEIGHTS + 5]
    act_ref, wup_ref, wdn_ref, win_ref = refs[2 * N_LATER_WEIGHTS + 5:]

    @pl.when(pl.program_id(0) == 0)
    def _stage_own_weights():
        _stage_weight(wup_hbm, wup_ref)
        _stage_weight(wdn_hbm, wdn_ref)
        _stage_weight(win_hbm, win_ref)

    for src, dst in zip(later_f32, later_bf16):
        dst[...] = src[...].astype(BF16)

    def project(k, rows, x1):
        x1_ref[rows, :] = x1
        h2 = _rms(x1, gmix_ref[...]).astype(BF16)
        q_ref[rows, :] = (_dot(h2, win_ref[:, 0:O_K]) * (HD ** -0.5 * LOG2E)).astype(BF16)
        kv_ref[rows, :] = _dot(h2, win_ref[:, O_K:O_F]).astype(BF16)
        f_ref[rows, :] = _dot(h2, win_ref[:, O_F:O_G]).astype(BF16)
        for c in range(2):
            lo = c * D_MODEL
            z = _dot(h2, win_ref[:, O_G + lo:O_G + lo + D_MODEL]) + bg_ref[:, lo:lo + D_MODEL]
            gate_ref[rows, lo:lo + D_MODEL] = _sigmoid(z).astype(BF16)

    _skewed_ffn(x_ref, g1_ref, wup_ref, wdn_ref, act_ref, project)


def _ffn_final_kernel(x_ref, g2_ref, wup_ref, wdn_ref, gfin_ref, out_ref, act_ref):
    def final_norm(k, rows, x3):
        out_ref[rows, :] = _rms(x3, gfin_ref[...])

    _skewed_ffn(x_ref, g2_ref, wup_ref, wdn_ref, act_ref, final_norm)


def _mixer_kernel(relb_ref, sink_ref, x1_ref, q_ref, kv_ref, f_ref, gate_ref, bucket_ref,
                  dftc_ref, dfts_ref, wa_ref, wb_ref, wout_ref, out_ref,
                  bias_ref, kpad_ref, vt_ref, frev_ref, uw_ref, o_ref, s_ref, p_ref,
                  fmix_ref, yb_ref):
    b = pl.program_id(0)
    t = pl.program_id(1)

    @pl.when((b == 0) & (t == 0))
    def _build_bias():
        bucket = bucket_ref[...]
        kj = lax.broadcasted_iota(jnp.int32, (3 * BLK, BLK), 0)
        qi = lax.broadcasted_iota(jnp.int32, (3 * BLK, BLK), 1)
        in_window = jnp.abs(kj - BLK - qi) <= WINDOW
        valid = (in_window, in_window & (kj >= BLK), in_window & (kj < 2 * BLK))
        for h in range(N_HEADS):
            def pick(bk, acc, h=h):
                return jnp.where(bucket == bk, relb_ref[bk, h] * LOG2E, acc)
            base = lax.fori_loop(0, N_BUCKETS, pick, jnp.zeros((3 * BLK, BLK), F32))
            for v in range(3):
                bias_ref[v, h] = jnp.where(valid[v], base, -jnp.inf)

    @pl.when(t == 0)
    def _per_sequence():
        kpad_ref[0:BLK, :] = jnp.zeros((BLK, KV_WIDTH), BF16)
        kpad_ref[BLK:BLK + SEQ, :] = kv_ref[:, :KV_WIDTH]
        kpad_ref[BLK + SEQ:, :] = jnp.zeros((BLK, KV_WIDTH), BF16)
        vt_ref[:, 0:BLK] = jnp.zeros((N_KV * VT_ROWS, BLK), BF16)
        vt_ref[:, BLK + SEQ:] = jnp.zeros((N_KV * VT_ROWS, BLK), BF16)
        for c in range(SEQ // BLK):
            v_t = kv_ref[c * BLK:(c + 1) * BLK, KV_WIDTH:].T
            for kh in range(N_KV):
                vt_ref[kh * VT_ROWS:kh * VT_ROWS + HD, (c + 1) * BLK:(c + 2) * BLK] = (
                    v_t[kh * HD:(kh + 1) * HD])
        for kh in range(N_KV):
            vt_ref[kh * VT_ROWS + HD:(kh + 1) * VT_ROWS, :] = jnp.ones(
                (VT_ROWS - HD, SEQ + 2 * BLK), BF16)
        jj = lax.broadcasted_iota(jnp.int32, (BLK, 2 * BLK), 0)
        cc = lax.broadcasted_iota(jnp.int32, (BLK, 2 * BLK), 1)
        flip = jnp.where(jj + cc == BLK, 1.0, 0.0).astype(BF16)
        for blk in range(HALF_SEQ // BLK):
            src = SEQ - (blk + 1) * BLK
            if blk == 0:
                rev = _dot(flip[:, :BLK], f_ref[src:src + BLK, :])
            else:
                rev = _dot(flip, f_ref[src:src + 2 * BLK, :])
            frev_ref[blk * BLK:(blk + 1) * BLK, :] = rev.astype(BF16)
        first_row = lax.broadcasted_iota(jnp.int32, (V7X_BF16_SUBLANES, FG_DIM), 0) == 0
        for g in range(N_FGROUPS):
            lo = g * FG_DIM
            both = jnp.concatenate([f_ref[0:HALF_SEQ, lo:lo + FG_DIM],
                                    frev_ref[:, lo:lo + FG_DIM]], axis=1)
            r = _dot(both, dftc_ref[...])
            uw_ref[0:HALF_SEQ, lo:lo + FG_DIM] = r[:, :FG_DIM].astype(BF16)
            uw_ref[HALF_SEQ:, lo:lo + FG_DIM] = r[:, FG_DIM:].astype(BF16)
            mid = _dot(f_ref[HALF_SEQ:HALF_SEQ + V7X_BF16_SUBLANES, lo:lo + FG_DIM],
                       dftc_ref[0:FG_DIM, 0:FG_DIM])
            head = uw_ref[HALF_SEQ:HALF_SEQ + V7X_BF16_SUBLANES, lo:lo + FG_DIM]
            uw_ref[HALF_SEQ:HALF_SEQ + V7X_BF16_SUBLANES, lo:lo + FG_DIM] = jnp.where(
                first_row, mid, head.astype(F32)).astype(BF16)

    def block_keys(i):
        n = t * BLOCKS_PER_TILE + i
        variant = jnp.where(n == 0, 1, jnp.where(n == SEQ // BLK - 1, 2, 0))
        return pl.multiple_of(n * BLK, BLK), variant

    def scores(i):
        key0, variant = block_keys(i)
        q_t = q_ref[i * BLK:(i + 1) * BLK, :].T
        zero = jnp.zeros((HD, BLK), BF16)
        cols = []
        for h in range(N_HEADS):
            q_h = q_t[h * HD:(h + 1) * HD]
            cols.append(jnp.concatenate([q_h, zero] if h < GQ else [zero, q_h], axis=0))
        s_t = _dot(kpad_ref[pl.ds(key0, 3 * BLK), :], jnp.concatenate(cols, axis=1))
        tops = []
        for h in range(N_HEADS):
            s_h = s_t[:, h * BLK:(h + 1) * BLK]
            s_ref[i % 2, h] = s_h
            top = s_h[0:KEY_CHUNK] + bias_ref[variant, h, 0:KEY_CHUNK, :]
            for k0 in range(KEY_CHUNK, 3 * BLK, KEY_CHUNK):
                top = jnp.maximum(top, s_h[k0:k0 + KEY_CHUNK] + bias_ref[variant, h, k0:k0 + KEY_CHUNK, :])
            tops.append(jnp.max(top, axis=0, keepdims=True))
        return tops

    def softmax(i, tops):
        _, variant = block_keys(i)
        buf = i % 2
        sink_terms = []
        for h in range(N_HEADS):
            sink = sink_ref[h] * LOG2E
            m = jnp.maximum(tops[h], sink)
            for k0 in range(0, 3 * BLK, KEY_CHUNK):
                z = (s_ref[buf, h, k0:k0 + KEY_CHUNK, :] - m) + bias_ref[variant, h, k0:k0 + KEY_CHUNK, :]
                p_ref[buf, h, k0:k0 + KEY_CHUNK, :] = jnp.exp2(z).astype(BF16)
            sink_terms.append(jnp.exp2(sink - m))
        return sink_terms

    def weighted_values(i, sink_terms):
        key0, _ = block_keys(i)
        for kh in range(N_KV):
            p_group = jnp.concatenate([p_ref[i % 2, kh * GQ + g] for g in range(GQ)], axis=1)
            o_t = _dot(vt_ref[kh * VT_ROWS:(kh + 1) * VT_ROWS, pl.ds(key0, 3 * BLK)], p_group)
            den = o_t[HD:HD + 1] + jnp.concatenate(sink_terms[kh * GQ:(kh + 1) * GQ], axis=1)
            o_n = o_t[:HD] / den
            for pr in range(GQ // 2):
                pair_t = jnp.concatenate([o_n[:, 2 * pr * BLK:(2 * pr + 1) * BLK],
                                          o_n[:, (2 * pr + 1) * BLK:(2 * pr + 2) * BLK]], axis=0)
                lane0 = (kh * GQ + 2 * pr) * HD
                o_ref[i * BLK:(i + 1) * BLK, lane0:lane0 + 2 * HD] = pair_t.T.astype(BF16)

    half_tile = ROW_TILE // 2
    tops = [scores(0)] + [None] * (BLOCKS_PER_TILE - 1)
    for i in range(BLOCKS_PER_TILE):
        if i + 1 < BLOCKS_PER_TILE:
            tops[i + 1] = scores(i + 1)
        if i < 2:
            rows = slice(i * half_tile, (i + 1) * half_tile)
            freq0 = pl.multiple_of(t * ROW_TILE + i * half_tile, half_tile)
            fmix_ref[rows, :] = _dot(dfts_ref[pl.ds(freq0, half_tile), :], uw_ref[...]).astype(BF16)
        elif i == 2:
            yb_ref[...] = _dot(fmix_ref[...], wb_ref[...])
        weighted_values(i, softmax(i, tops[i]))

    y_a = _dot(o_ref[...], wa_ref[...])
    mix = (gate_ref[:, :D_MODEL].astype(F32) * y_a + gate_ref[:, D_MODEL:].astype(F32) * yb_ref[...])
    out_ref[...] = x1_ref[...] + _dot(mix.astype(BF16), wout_ref[...])


def _resident(shape):
    return pl.BlockSpec(shape, lambda *_: (0,) * len(shape), pipeline_mode=pl.Buffered(1))


def _rows(width, tile=ROW_TILE):
    return pl.BlockSpec((tile, width), lambda i: (i, 0))


def _slab_spec(shape):
    rows, cols = shape
    steps = TOKENS // ROW_TILE
    slab = next(r for r in range(V7X_BF16_SUBLANES, rows + 1, V7X_BF16_SUBLANES)
                if rows % r == 0 and r * steps >= rows)
    return pl.BlockSpec((slab, cols), lambda i: (jnp.minimum(i, rows // slab - 1), 0))


def _ffn_proj(x, g1, wup, wdn, gmix, win, bg, later_weights):
    assert len(later_weights) == N_LATER_WEIGHTS
    slabs = [_slab_spec(w.shape) for w in later_weights]
    hbm = pl.BlockSpec(memory_space=pl.ANY)
    return pl.pallas_call(
        _ffn_proj_kernel,
        grid=(TOKENS // ROW_TILE,),
        in_specs=[_rows(D_MODEL), _resident((1, D_MODEL)), hbm, hbm, _resident((1, D_MODEL)),
                  hbm, _resident((1, 2 * D_MODEL))] + slabs,
        out_specs=[_rows(D_MODEL), _rows(Q_WIDTH), _rows(2 * KV_WIDTH), _rows(F_WIDTH),
                   _rows(2 * D_MODEL)] + slabs,
        out_shape=[jax.ShapeDtypeStruct((TOKENS, D_MODEL), F32),
                   jax.ShapeDtypeStruct((TOKENS, Q_WIDTH), BF16),
                   jax.ShapeDtypeStruct((TOKENS, 2 * KV_WIDTH), BF16),
                   jax.ShapeDtypeStruct((TOKENS, F_WIDTH), BF16),
                   jax.ShapeDtypeStruct((TOKENS, 2 * D_MODEL), BF16)]
        + [jax.ShapeDtypeStruct(w.shape, BF16) for w in later_weights],
        scratch_shapes=[pltpu.VMEM((ROW_TILE, D_FF), BF16),
                        pltpu.VMEM((D_MODEL, 2 * D_FF), BF16),
                        pltpu.VMEM((D_FF, D_MODEL), BF16),
                        pltpu.VMEM((D_MODEL, IN_WIDTH), BF16)],
        compiler_params=pltpu.CompilerParams(dimension_semantics=("arbitrary",),
                                             vmem_limit_bytes=V7X_VMEM_LIMIT_BYTES),
        name="ffn_proj",
    )(x, g1, wup, wdn, gmix, win, bg, *later_weights)


def _ffn_final(x, g2, wup, wdn, gfin):
    return pl.pallas_call(
        _ffn_final_kernel,
        grid=(TOKENS // FINAL_ROW_TILE,),
        in_specs=[_rows(D_MODEL, FINAL_ROW_TILE), _resident((1, D_MODEL)),
                  _resident((D_MODEL, 2 * D_FF)), _resident((D_FF, D_MODEL)),
                  _resident((1, D_MODEL))],
        out_specs=_rows(D_MODEL, FINAL_ROW_TILE),
        out_shape=jax.ShapeDtypeStruct((TOKENS, D_MODEL), F32),
        scratch_shapes=[pltpu.VMEM((FINAL_ROW_TILE, D_FF), BF16)],
        compiler_params=pltpu.CompilerParams(dimension_semantics=("arbitrary",),
                                             vmem_limit_bytes=V7X_VMEM_LIMIT_BYTES),
        name="ffn_final",
    )(x, g2, wup, wdn, gfin)


def _mixer(rel_bias, sink, x1, q, kv, f, gates, bucket, dftc, dfts, wa, wb, wout):
    def tile_rows(width):
        return pl.BlockSpec((ROW_TILE, width), lambda b, t: (b * TILES_PER_SEQ + t, 0))

    def seq_rows(width):
        return pl.BlockSpec((SEQ, width), lambda b, t: (b, 0))

    smem = pl.BlockSpec(memory_space=pltpu.SMEM)
    return pl.pallas_call(
        _mixer_kernel,
        grid=(BATCH, TILES_PER_SEQ),
        in_specs=[smem, smem, tile_rows(D_MODEL), tile_rows(Q_WIDTH), seq_rows(2 * KV_WIDTH),
                  seq_rows(F_WIDTH), tile_rows(2 * D_MODEL), _resident((3 * BLK, BLK)),
                  _resident((2 * FG_DIM, 2 * FG_DIM)),
                  _resident((SEQ, SEQ)),
                  _resident((Q_WIDTH, D_MODEL)), _resident((F_WIDTH, D_MODEL)),
                  _resident((D_MODEL, D_MODEL))],
        out_specs=tile_rows(D_MODEL),
        out_shape=jax.ShapeDtypeStruct((TOKENS, D_MODEL), F32),
        scratch_shapes=[pltpu.VMEM((3, N_HEADS, 3 * BLK, BLK), F32),
                        pltpu.VMEM((SEQ + 2 * BLK, KV_WIDTH), BF16),
                        pltpu.VMEM((N_KV * VT_ROWS, SEQ + 2 * BLK), BF16),
                        pltpu.VMEM((HALF_SEQ, F_WIDTH), BF16),
                        pltpu.VMEM((SEQ, F_WIDTH), BF16),
                        pltpu.VMEM((ROW_TILE, Q_WIDTH), BF16),
                        pltpu.VMEM((2, N_HEADS, 3 * BLK, BLK), F32),
                        pltpu.VMEM((2, N_HEADS, 3 * BLK, BLK), BF16),
                        pltpu.VMEM((ROW_TILE, F_WIDTH), BF16),
                        pltpu.VMEM((ROW_TILE, D_MODEL), F32)],
        compiler_params=pltpu.CompilerParams(dimension_semantics=("arbitrary", "arbitrary"),
                                             vmem_limit_bytes=V7X_VMEM_LIMIT_BYTES),
        name="mixer",
    )(rel_bias, sink, x1, q, kv, f, gates, bucket, dftc, dfts, wa, wb, wout)


def kernel(x, g_ffn1, w_up1, w_down1, g_mix, w_in, b_gate, sink, rel_bias, w_branch_a, w_branch_b, w_out, g_ffn2, w_up2, w_down2, g_final):
    assert x.shape == (BATCH, SEQ, D_MODEL) and w_up1.shape[0] == DEPTH == 1
    chan, pos = _dft_tables()
    bucket = jnp.asarray(_t5_bucket_table().T)
    dftc = jnp.asarray(chan).astype(BF16)
    dfts = jnp.asarray(pos).astype(BF16)
    row = lambda v: v.reshape(1, -1)
    x0 = x.reshape(TOKENS, D_MODEL)
    x1, q, kv, f, gates, wa, wb, wout, wup2, wdn2 = _ffn_proj(
        x0, row(g_ffn1[0]), w_up1[0], w_down1[0], row(g_mix[0]), w_in[0], row(b_gate[0]),
        [w_branch_a[0], w_branch_b[0], w_out[0], w_up2[0], w_down2[0]])
    x2 = _mixer(rel_bias, sink[0], x1, q, kv, f, gates, bucket, dftc, dfts, wa, wb, wout)
    out = _ffn_final(x2, row(g_ffn2[0]), wup2, wdn2, row(g_final))
    return out.reshape(BATCH, SEQ, D_MODEL)
```

```python
import functools

import numpy as np
import jax
import jax.numpy as jnp
from jax import lax
from jax.experimental import pallas as pl
from jax.experimental.pallas import tpu as pltpu

D_MODEL = 1024
BATCH = 8
SEQ = 2048
DEPTH = 1
N_HEADS = 8
N_KV = 2
GQ = N_HEADS // N_KV
HD = 64
Q_WIDTH = N_HEADS * HD
KV_WIDTH = N_KV * HD
WINDOW = 128
BLK = 128
N_FGROUPS = 4
FG_DIM = 128
F_WIDTH = N_FGROUPS * FG_DIM
N_BUCKETS = 32
MAX_DIST = 128
D_FF = 2816
EPS = 1e-6
O_K = Q_WIDTH
O_V = O_K + KV_WIDTH
O_F = O_V + KV_WIDTH
O_G = O_F + F_WIDTH
IN_WIDTH = O_G + 2 * D_MODEL

F32 = jnp.float32
BF16 = jnp.bfloat16

V7X_MXU_DIM = 256
V7X_BF16_SUBLANES = 16
V7X_VMEM_LIMIT_BYTES = 56 * 1024 * 1024

TOKENS = BATCH * SEQ
HALF_SEQ = SEQ // 2
ROW_TILE = 512
MIX_ROW_TILE = 1024
FINAL_ROW_TILE = 512
MIX_TILES_PER_SEQ = SEQ // MIX_ROW_TILE
MIX_BLOCKS = MIX_ROW_TILE // BLK
FF_CHUNK = V7X_MXU_DIM
FFN_SUB_ROWS = 256
VT_ROWS = HD + V7X_BF16_SUBLANES
N_LATER_WEIGHTS = 5
STAGE_SLOTS = 4
STAGE_BYTES = 3 * 512 * 1024
KEY_CHUNK = 64
LOG2E = float(np.log2(np.e))


def _t5_bucket_table():
    rel = (np.arange(3 * BLK)[None, :] - BLK) - np.arange(BLK)[:, None]
    half = N_BUCKETS // 2
    max_exact = half // 2
    ret = (rel > 0).astype(np.int32) * half
    n = np.abs(rel)
    n_safe = np.maximum(n, 1).astype(np.float32)
    large = max_exact + (np.log(n_safe / max_exact) / np.log(MAX_DIST / max_exact)
                         * (half - max_exact)).astype(np.int32)
    large = np.minimum(large, half - 1)
    return (ret + np.where(n < max_exact, n, large)).astype(np.int32)


@functools.lru_cache(maxsize=None)
def _dft_tables():
    def cos_sin(n):
        idx = np.arange(n)
        ang = 2.0 * np.pi * ((idx[:, None] * idx[None, :]) % n).astype(np.float64) / n
        return np.cos(ang) / np.sqrt(n), np.sin(ang) / np.sqrt(n)
    cc, sc = cos_sin(FG_DIM)
    cs, ss = cos_sin(SEQ)
    chan = np.block([[cc, sc], [cc, -sc]]).astype(np.float32)
    neg_sin = -ss[:, :HALF_SEQ]
    neg_sin[:, 0] = np.where(np.arange(SEQ) % 2 == 0, 1.0, -1.0) / np.sqrt(SEQ)
    pos = np.concatenate([cs[:, :HALF_SEQ], neg_sin], axis=1).astype(np.float32)
    return chan, pos


def _rms(x, g):
    return x * lax.rsqrt(jnp.mean(x * x, axis=-1, keepdims=True) + EPS) * g


def _sigmoid(z):
    return 1.0 / (1.0 + jnp.exp2(z * -LOG2E))


def _dot(a, b):
    return jnp.dot(a, b, preferred_element_type=F32)


def _skewed_ffn(x_ref, g_ref, wup_ref, wdn_ref, act_ref, finish, prepare=None):
    n_sub = x_ref.shape[0] // FFN_SUB_ROWS
    rows = [slice(k * FFN_SUB_ROWS, (k + 1) * FFN_SUB_ROWS) for k in range(n_sub)]
    normed = [None] * n_sub

    def prologue(k):
        if prepare is not None:
            prepare(rows[k])
        normed[k] = _rms(x_ref[rows[k], :], g_ref[...]).astype(BF16)

    def up_chunk(k, c):
        lo = c * FF_CHUNK
        gate = _dot(normed[k], wup_ref[:, lo:lo + FF_CHUNK])
        up = _dot(normed[k], wup_ref[:, D_FF + lo:D_FF + lo + FF_CHUNK])
        act_ref[rows[k], lo:lo + FF_CHUNK] = (gate * _sigmoid(gate) * up).astype(BF16)

    def epilogue(k):
        finish(k, rows[k], x_ref[rows[k], :] + 0.5 * _dot(act_ref[rows[k], :], wdn_ref[...]))

    prologue(0)
    for k in range(n_sub):
        for c in range(D_FF // FF_CHUNK):
            up_chunk(k, c)
            if c == 0 and k + 1 < n_sub:
                prologue(k + 1)
            if c == 1 and k > 0:
                epilogue(k - 1)
    epilogue(n_sub - 1)


def _stage_weight(src_hbm, dst_ref):
    rows, cols = dst_ref.shape
    slab = max(r for r in range(V7X_BF16_SUBLANES, rows + 1, V7X_BF16_SUBLANES)
               if rows % r == 0 and r * cols * 4 <= STAGE_BYTES)
    n_slabs = rows // slab

    def staged(stage_ref, sem):
        def slab_copy(c, slot):
            return pltpu.make_async_copy(src_hbm.at[pl.ds(c * slab, slab), :],
                                         stage_ref.at[slot], sem.at[slot])

        for c in range(min(STAGE_SLOTS - 1, n_slabs)):
            slab_copy(c, c).start()

        def step(c, carry):
            ahead = c + STAGE_SLOTS - 1

            @pl.when(ahead < n_slabs)
            def _start_ahead():
                slab_copy(ahead, ahead % STAGE_SLOTS).start()

            slot = c % STAGE_SLOTS
            slab_copy(c, slot).wait()
            dst_ref[pl.ds(pl.multiple_of(c * slab, slab), slab), :] = stage_ref[slot].astype(BF16)
            return carry

        lax.fori_loop(0, n_slabs, step, 0)

    pl.run_scoped(staged, pltpu.VMEM((STAGE_SLOTS, slab, cols), F32),
                  pltpu.SemaphoreType.DMA((STAGE_SLOTS,)))


def _ffn_proj_kernel(x_ref, g1_ref, wup_hbm, wdn_hbm, gmix_ref, win_hbm, bg_ref, *refs):
    later_f32 = refs[:N_LATER_WEIGHTS]
    x1_ref, q_ref, kv_ref, f_ref, gate_ref = refs[N_LATER_WEIGHTS:N_LATER_WEIGHTS + 5]
    later_bf16 = refs[N_LATER_WEIGHTS + 5:2 * N_LATER_WEIGHTS + 5]
    act_ref, wup_ref, wdn_ref, win_ref = refs[2 * N_LATER_WEIGHTS + 5:]

    @pl.when(pl.program_id(0) == 0)
    def _stage_own_weights():
        _stage_weight(wup_hbm, wup_ref)
        _stage_weight(wdn_hbm, wdn_ref)
        _stage_weight(win_hbm, win_ref)

    for src, dst in zip(later_f32, later_bf16):
        dst[...] = src[...].astype(BF16)

    def project(k, rows, x1):
        x1_ref[rows, :] = x1
        h2 = _rms(x1, gmix_ref[...]).astype(BF16)
        q_ref[rows, :] = (_dot(h2, win_ref[:, 0:O_K]) * (HD ** -0.5 * LOG2E)).astype(BF16)
        kv_ref[rows, :] = _dot(h2, win_ref[:, O_K:O_F]).astype(BF16)
        f_ref[rows, :] = _dot(h2, win_ref[:, O_F:O_G]).astype(BF16)
        for c in range(2):
            lo = c * D_MODEL
            z = _dot(h2, win_ref[:, O_G + lo:O_G + lo + D_MODEL]) + bg_ref[:, lo:lo + D_MODEL]
            gate_ref[rows, lo:lo + D_MODEL] = _sigmoid(z).astype(BF16)

    _skewed_ffn(x_ref, g1_ref, wup_ref, wdn_ref, act_ref, project)


def _ffn_final_kernel(x1_ref, o_ref, fmix_ref, gate_ref, wa_ref, wb_ref, wout_ref, g2_ref,
                      wup_ref, wdn_ref, gfin_ref, out_ref, act_ref, x2_ref):
    def mix_branches(rows):
        y_a = _dot(o_ref[rows, :], wa_ref[...])
        y_b = _dot(fmix_ref[rows, :], wb_ref[...])
        mix = (gate_ref[rows, :D_MODEL].astype(F32) * y_a
               + gate_ref[rows, D_MODEL:].astype(F32) * y_b)
        x2_ref[rows, :] = x1_ref[rows, :] + _dot(mix.astype(BF16), wout_ref[...])

    def final_norm(k, rows, x3):
        out_ref[rows, :] = _rms(x3, gfin_ref[...])

    _skewed_ffn(x2_ref, g2_ref, wup_ref, wdn_ref, act_ref, final_norm, prepare=mix_branches)


def _mixer_kernel(relb_ref, sink_ref, q_ref, kv_ref, f_ref, bucket_ref, dftc_ref, dfts_ref,
                  o_ref, fmix_ref, bias_ref, kpad_ref, vt_ref, frev_ref, uw_ref, s_ref, p_ref):
    b = pl.program_id(0)
    t = pl.program_id(1)

    @pl.when((b == 0) & (t == 0))
    def _build_bias():
        bucket = bucket_ref[...]
        kj = lax.broadcasted_iota(jnp.int32, (3 * BLK, BLK), 0)
        qi = lax.broadcasted_iota(jnp.int32, (3 * BLK, BLK), 1)
        in_window = jnp.abs(kj - BLK - qi) <= WINDOW
        valid = (in_window, in_window & (kj >= BLK), in_window & (kj < 2 * BLK))
        for h in range(N_HEADS):
            def pick(bk, acc, h=h):
                return jnp.where(bucket == bk, relb_ref[bk, h] * LOG2E, acc)
            base = lax.fori_loop(0, N_BUCKETS, pick, jnp.zeros((3 * BLK, BLK), F32))
            for v in range(3):
                bias_ref[v, h] = jnp.where(valid[v], base, -jnp.inf)

    @pl.when(t == 0)
    def _per_sequence():
        kpad_ref[0:BLK, :] = jnp.zeros((BLK, KV_WIDTH), BF16)
        kpad_ref[BLK:BLK + SEQ, :] = kv_ref[:, :KV_WIDTH]
        kpad_ref[BLK + SEQ:, :] = jnp.zeros((BLK, KV_WIDTH), BF16)
        vt_ref[:, 0:BLK] = jnp.zeros((N_KV * VT_ROWS, BLK), BF16)
        vt_ref[:, BLK + SEQ:] = jnp.zeros((N_KV * VT_ROWS, BLK), BF16)
        for c in range(SEQ // BLK):
            v_t = kv_ref[c * BLK:(c + 1) * BLK, KV_WIDTH:].T
            for kh in range(N_KV):
                vt_ref[kh * VT_ROWS:kh * VT_ROWS + HD, (c + 1) * BLK:(c + 2) * BLK] = (
                    v_t[kh * HD:(kh + 1) * HD])
        for kh in range(N_KV):
            vt_ref[kh * VT_ROWS + HD:(kh + 1) * VT_ROWS, :] = jnp.ones(
                (VT_ROWS - HD, SEQ + 2 * BLK), BF16)
        jj = lax.broadcasted_iota(jnp.int32, (BLK, 2 * BLK), 0)
        cc = lax.broadcasted_iota(jnp.int32, (BLK, 2 * BLK), 1)
        flip = jnp.where(jj + cc == BLK, 1.0, 0.0).astype(BF16)
        for blk in range(HALF_SEQ // BLK):
            src = SEQ - (blk + 1) * BLK
            if blk == 0:
                rev = _dot(flip[:, :BLK], f_ref[src:src + BLK, :])
            else:
                rev = _dot(flip, f_ref[src:src + 2 * BLK, :])
            frev_ref[blk * BLK:(blk + 1) * BLK, :] = rev.astype(BF16)
        first_row = lax.broadcasted_iota(jnp.int32, (V7X_BF16_SUBLANES, FG_DIM), 0) == 0
        for g in range(N_FGROUPS):
            lo = g * FG_DIM
            both = jnp.concatenate([f_ref[0:HALF_SEQ, lo:lo + FG_DIM],
                                    frev_ref[:, lo:lo + FG_DIM]], axis=1)
            r = _dot(both, dftc_ref[...])
            uw_ref[0:HALF_SEQ, lo:lo + FG_DIM] = r[:, :FG_DIM].astype(BF16)
            uw_ref[HALF_SEQ:, lo:lo + FG_DIM] = r[:, FG_DIM:].astype(BF16)
            mid = _dot(f_ref[HALF_SEQ:HALF_SEQ + V7X_BF16_SUBLANES, lo:lo + FG_DIM],
                       dftc_ref[0:FG_DIM, 0:FG_DIM])
            head = uw_ref[HALF_SEQ:HALF_SEQ + V7X_BF16_SUBLANES, lo:lo + FG_DIM]
            uw_ref[HALF_SEQ:HALF_SEQ + V7X_BF16_SUBLANES, lo:lo + FG_DIM] = jnp.where(
                first_row, mid, head.astype(F32)).astype(BF16)

    def block_keys(i):
        n = t * MIX_BLOCKS + i
        variant = jnp.where(n == 0, 1, jnp.where(n == SEQ // BLK - 1, 2, 0))
        return pl.multiple_of(n * BLK, BLK), variant

    def scores(i):
        key0, variant = block_keys(i)
        q_t = q_ref[i * BLK:(i + 1) * BLK, :].T
        zero = jnp.zeros((HD, BLK), BF16)
        cols = []
        for h in range(N_HEADS):
            q_h = q_t[h * HD:(h + 1) * HD]
            cols.append(jnp.concatenate([q_h, zero] if h < GQ else [zero, q_h], axis=0))
        s_t = _dot(kpad_ref[pl.ds(key0, 3 * BLK), :], jnp.concatenate(cols, axis=1))
        tops = []
        for h in range(N_HEADS):
            s_h = s_t[:, h * BLK:(h + 1) * BLK]
            s_ref[i % 2, h] = s_h
            top = s_h[0:KEY_CHUNK] + bias_ref[variant, h, 0:KEY_CHUNK, :]
            for k0 in range(KEY_CHUNK, 3 * BLK, KEY_CHUNK):
                top = jnp.maximum(top, s_h[k0:k0 + KEY_CHUNK] + bias_ref[variant, h, k0:k0 + KEY_CHUNK, :])
            tops.append(jnp.max(top, axis=0, keepdims=True))
        return tops

    def softmax(i, tops):
        _, variant = block_keys(i)
        buf = i % 2
        sink_terms = []
        for h in range(N_HEADS):
            sink = sink_ref[h] * LOG2E
            m = jnp.maximum(tops[h], sink)
            for k0 in range(0, 3 * BLK, KEY_CHUNK):
                z = (s_ref[buf, h, k0:k0 + KEY_CHUNK, :] - m) + bias_ref[variant, h, k0:k0 + KEY_CHUNK, :]
                p_ref[buf, h, k0:k0 + KEY_CHUNK, :] = jnp.exp2(z).astype(BF16)
            sink_terms.append(jnp.exp2(sink - m))
        return sink_terms

    def weighted_values(i, sink_terms):
        key0, _ = block_keys(i)
        for kh in range(N_KV):
            p_group = jnp.concatenate([p_ref[i % 2, kh * GQ + g] for g in range(GQ)], axis=1)
            o_t = _dot(vt_ref[kh * VT_ROWS:(kh + 1) * VT_ROWS, pl.ds(key0, 3 * BLK)], p_group)
            den = o_t[HD:HD + 1] + jnp.concatenate(sink_terms[kh * GQ:(kh + 1) * GQ], axis=1)
            o_n = o_t[:HD] / den
            for pr in range(GQ // 2):
                pair_t = jnp.concatenate([o_n[:, 2 * pr * BLK:(2 * pr + 1) * BLK],
                                          o_n[:, (2 * pr + 1) * BLK:(2 * pr + 2) * BLK]], axis=0)
                lane0 = (kh * GQ + 2 * pr) * HD
                o_ref[i * BLK:(i + 1) * BLK, lane0:lane0 + 2 * HD] = pair_t.T.astype(BF16)

    tops = [scores(0)] + [None] * (MIX_BLOCKS - 1)
    for i in range(MIX_BLOCKS):
        if i + 1 < MIX_BLOCKS:
            tops[i + 1] = scores(i + 1)
        if i % 2 == 0:
            rows = slice(i * BLK, (i + 2) * BLK)
            freq0 = pl.multiple_of(t * MIX_ROW_TILE + i * BLK, 2 * BLK)
            fmix_ref[rows, :] = _dot(dfts_ref[pl.ds(freq0, 2 * BLK), :], uw_ref[...]).astype(BF16)
        weighted_values(i, softmax(i, tops[i]))


def _resident(shape):
    return pl.BlockSpec(shape, lambda *_: (0,) * len(shape), pipeline_mode=pl.Buffered(1))


def _rows(width, tile=ROW_TILE):
    return pl.BlockSpec((tile, width), lambda i: (i, 0))


def _slab_spec(shape):
    rows, cols = shape
    steps = TOKENS // ROW_TILE
    slab = next(r for r in range(V7X_BF16_SUBLANES, rows + 1, V7X_BF16_SUBLANES)
                if rows % r == 0 and r * steps >= rows)
    return pl.BlockSpec((slab, cols), lambda i: (jnp.minimum(i, rows // slab - 1), 0))


def _ffn_proj(x, g1, wup, wdn, gmix, win, bg, later_weights):
    assert len(later_weights) == N_LATER_WEIGHTS
    slabs = [_slab_spec(w.shape) for w in later_weights]
    hbm = pl.BlockSpec(memory_space=pl.ANY)
    return pl.pallas_call(
        _ffn_proj_kernel,
        grid=(TOKENS // ROW_TILE,),
        in_specs=[_rows(D_MODEL), _resident((1, D_MODEL)), hbm, hbm, _resident((1, D_MODEL)),
                  hbm, _resident((1, 2 * D_MODEL))] + slabs,
        out_specs=[_rows(D_MODEL), _rows(Q_WIDTH), _rows(2 * KV_WIDTH), _rows(F_WIDTH),
                   _rows(2 * D_MODEL)] + slabs,
        out_shape=[jax.ShapeDtypeStruct((TOKENS, D_MODEL), F32),
                   jax.ShapeDtypeStruct((TOKENS, Q_WIDTH), BF16),
                   jax.ShapeDtypeStruct((TOKENS, 2 * KV_WIDTH), BF16),
                   jax.ShapeDtypeStruct((TOKENS, F_WIDTH), BF16),
                   jax.ShapeDtypeStruct((TOKENS, 2 * D_MODEL), BF16)]
        + [jax.ShapeDtypeStruct(w.shape, BF16) for w in later_weights],
        scratch_shapes=[pltpu.VMEM((ROW_TILE, D_FF), BF16),
                        pltpu.VMEM((D_MODEL, 2 * D_FF), BF16),
                        pltpu.VMEM((D_FF, D_MODEL), BF16),
                        pltpu.VMEM((D_MODEL, IN_WIDTH), BF16)],
        compiler_params=pltpu.CompilerParams(dimension_semantics=("arbitrary",),
                                             vmem_limit_bytes=V7X_VMEM_LIMIT_BYTES),
        name="ffn_proj",
    )(x, g1, wup, wdn, gmix, win, bg, *later_weights)


def _ffn_final(x1, o, fmix, gates, wa, wb, wout, g2, wup, wdn, gfin):
    tile = functools.partial(_rows, tile=FINAL_ROW_TILE)
    return pl.pallas_call(
        _ffn_final_kernel,
        grid=(TOKENS // FINAL_ROW_TILE,),
        in_specs=[tile(D_MODEL), tile(Q_WIDTH), tile(F_WIDTH), tile(2 * D_MODEL),
                  _resident((Q_WIDTH, D_MODEL)), _resident((F_WIDTH, D_MODEL)),
                  _resident((D_MODEL, D_MODEL)), _resident((1, D_MODEL)),
                  _resident((D_MODEL, 2 * D_FF)), _resident((D_FF, D_MODEL)),
                  _resident((1, D_MODEL))],
        out_specs=tile(D_MODEL),
        out_shape=jax.ShapeDtypeStruct((TOKENS, D_MODEL), F32),
        scratch_shapes=[pltpu.VMEM((FINAL_ROW_TILE, D_FF), BF16),
                        pltpu.VMEM((FINAL_ROW_TILE, D_MODEL), F32)],
        compiler_params=pltpu.CompilerParams(dimension_semantics=("arbitrary",),
                                             vmem_limit_bytes=V7X_VMEM_LIMIT_BYTES),
        name="ffn_final",
    )(x1, o, fmix, gates, wa, wb, wout, g2, wup, wdn, gfin)


def _mixer(rel_bias, sink, q, kv, f, bucket, dftc, dfts):
    def tile_rows(width):
        return pl.BlockSpec((MIX_ROW_TILE, width), lambda b, t: (b * MIX_TILES_PER_SEQ + t, 0))

    def seq_rows(width):
        return pl.BlockSpec((SEQ, width), lambda b, t: (b, 0))

    smem = pl.BlockSpec(memory_space=pltpu.SMEM)
    return pl.pallas_call(
        _mixer_kernel,
        grid=(BATCH, MIX_TILES_PER_SEQ),
        in_specs=[smem, smem, tile_rows(Q_WIDTH), seq_rows(2 * KV_WIDTH), seq_rows(F_WIDTH),
                  _resident((3 * BLK, BLK)), _resident((2 * FG_DIM, 2 * FG_DIM)),
                  _resident((SEQ, SEQ))],
        out_specs=[tile_rows(Q_WIDTH), tile_rows(F_WIDTH)],
        out_shape=[jax.ShapeDtypeStruct((TOKENS, Q_WIDTH), BF16),
                   jax.ShapeDtypeStruct((TOKENS, F_WIDTH), BF16)],
        scratch_shapes=[pltpu.VMEM((3, N_HEADS, 3 * BLK, BLK), F32),
                        pltpu.VMEM((SEQ + 2 * BLK, KV_WIDTH), BF16),
                        pltpu.VMEM((N_KV * VT_ROWS, SEQ + 2 * BLK), BF16),
                        pltpu.VMEM((HALF_SEQ, F_WIDTH), BF16),
                        pltpu.VMEM((SEQ, F_WIDTH), BF16),
                        pltpu.VMEM((2, N_HEADS, 3 * BLK, BLK), F32),
                        pltpu.VMEM((2, N_HEADS, 3 * BLK, BLK), BF16)],
        compiler_params=pltpu.CompilerParams(dimension_semantics=("arbitrary", "arbitrary"),
                                             vmem_limit_bytes=V7X_VMEM_LIMIT_BYTES),
        name="mixer",
    )(rel_bias, sink, q, kv, f, bucket, dftc, dfts)


def kernel(x, g_ffn1, w_up1, w_down1, g_mix, w_in, b_gate, sink, rel_bias, w_branch_a, w_branch_b, w_out, g_ffn2, w_up2, w_down2, g_final):
    assert x.shape == (BATCH, SEQ, D_MODEL) and w_up1.shape[0] == DEPTH == 1
    chan, pos = _dft_tables()
    bucket = jnp.asarray(_t5_bucket_table().T)
    dftc = jnp.asarray(chan).astype(BF16)
    dfts = jnp.asarray(pos).astype(BF16)
    row = lambda v: v.reshape(1, -1)
    x0 = x.reshape(TOKENS, D_MODEL)
    x1, q, kv, f, gates, wa, wb, wout, wup2, wdn2 = _ffn_proj(
        x0, row(g_ffn1[0]), w_up1[0], w_down1[0], row(g_mix[0]), w_in[0], row(b_gate[0]),
        [w_branch_a[0], w_branch_b[0], w_out[0], w_up2[0], w_down2[0]])
    o, fmix = _mixer(rel_bias, sink[0], q, kv, f, bucket, dftc, dfts)
    out = _ffn_final(x1, o, fmix, gates, wa, wb, wout, row(g_ffn2[0]), wup2, wdn2, row(g_final))
    return out.reshape(BATCH, SEQ, D_MODEL)
```

```python
import functools

import numpy as np
import jax
import jax.numpy as jnp
from jax import lax
from jax.experimental import pallas as pl
from jax.experimental.pallas import tpu as pltpu

D_MODEL = 1024
BATCH = 8
SEQ = 2048
DEPTH = 1
N_HEADS = 8
N_KV = 2
GQ = N_HEADS // N_KV
HD = 64
Q_WIDTH = N_HEADS * HD
KV_WIDTH = N_KV * HD
WINDOW = 128
BLK = 128
N_FGROUPS = 4
FG_DIM = 128
F_WIDTH = N_FGROUPS * FG_DIM
N_BUCKETS = 32
MAX_DIST = 128
D_FF = 2816
EPS = 1e-6
O_K = Q_WIDTH
O_V = O_K + KV_WIDTH
O_F = O_V + KV_WIDTH
O_G = O_F + F_WIDTH
IN_WIDTH = O_G + 2 * D_MODEL

F32 = jnp.float32
BF16 = jnp.bfloat16

V7X_MXU_DIM = 256
V7X_BF16_SUBLANES = 16
V7X_VMEM_LIMIT_BYTES = 56 * 1024 * 1024

TOKENS = BATCH * SEQ
HALF_SEQ = SEQ // 2
ROW_TILE = 512
FINAL_ROW_TILE = 1024
TILES_PER_SEQ = SEQ // ROW_TILE
BLOCKS_PER_TILE = ROW_TILE // BLK
FF_CHUNK = V7X_MXU_DIM
FFN_SUB_ROWS = 256
VT_ROWS = HD + V7X_BF16_SUBLANES
N_LATER_WEIGHTS = 5
STAGE_SLOTS = 6
STAGE_BYTES = 1024 * 1024
KEY_CHUNK = 64
LOG2E = float(np.log2(np.e))


def _t5_bucket_table():
    rel = (np.arange(3 * BLK)[None, :] - BLK) - np.arange(BLK)[:, None]
    half = N_BUCKETS // 2
    max_exact = half // 2
    ret = (rel > 0).astype(np.int32) * half
    n = np.abs(rel)
    n_safe = np.maximum(n, 1).astype(np.float32)
    large = max_exact + (np.log(n_safe / max_exact) / np.log(MAX_DIST / max_exact)
                         * (half - max_exact)).astype(np.int32)
    large = np.minimum(large, half - 1)
    return (ret + np.where(n < max_exact, n, large)).astype(np.int32)


@functools.lru_cache(maxsize=None)
def _dft_tables():
    def cos_sin(n):
        idx = np.arange(n)
        ang = 2.0 * np.pi * ((idx[:, None] * idx[None, :]) % n).astype(np.float64) / n
        return np.cos(ang) / np.sqrt(n), np.sin(ang) / np.sqrt(n)
    cc, sc = cos_sin(FG_DIM)
    cs, ss = cos_sin(SEQ)
    chan = np.block([[cc, sc], [cc, -sc]]).astype(np.float32)
    neg_sin = -ss[:, :HALF_SEQ]
    neg_sin[:, 0] = np.where(np.arange(SEQ) % 2 == 0, 1.0, -1.0) / np.sqrt(SEQ)
    pos = np.concatenate([cs[:, :HALF_SEQ], neg_sin], axis=1).astype(np.float32)
    return chan, pos


def _rms(x, g):
    return x * lax.rsqrt(jnp.mean(x * x, axis=-1, keepdims=True) + EPS) * g


def _sigmoid(z):
    return 1.0 / (1.0 + jnp.exp2(z * -LOG2E))


def _dot(a, b):
    return jnp.dot(a, b, preferred_element_type=F32)


def _skewed_ffn(x_ref, g_ref, wup_ref, wdn_ref, act_ref, finish):
    n_sub = x_ref.shape[0] // FFN_SUB_ROWS
    rows = [slice(k * FFN_SUB_ROWS, (k + 1) * FFN_SUB_ROWS) for k in range(n_sub)]
    normed = [None] * n_sub

    def prologue(k):
        normed[k] = _rms(x_ref[rows[k], :], g_ref[...]).astype(BF16)

    def up_chunk(k, c):
        lo = c * FF_CHUNK
        gate = _dot(normed[k], wup_ref[:, lo:lo + FF_CHUNK])
        up = _dot(normed[k], wup_ref[:, D_FF + lo:D_FF + lo + FF_CHUNK])
        act_ref[rows[k], lo:lo + FF_CHUNK] = (gate * _sigmoid(gate) * up).astype(BF16)

    def epilogue(k):
        finish(k, rows[k], x_ref[rows[k], :] + 0.5 * _dot(act_ref[rows[k], :], wdn_ref[...]))

    prologue(0)
    for k in range(n_sub):
        for c in range(D_FF // FF_CHUNK):
            up_chunk(k, c)
            if c == 0 and k + 1 < n_sub:
                prologue(k + 1)
            if c == 1 and k > 0:
                epilogue(k - 1)
    epilogue(n_sub - 1)


def _stage_weight(src_hbm, dst_ref):
    rows, cols = dst_ref.shape
    slab = max(r for r in range(V7X_BF16_SUBLANES, rows + 1, V7X_BF16_SUBLANES)
               if rows % r == 0 and r * cols * 4 <= STAGE_BYTES)
    n_slabs = rows // slab

    def staged(stage_ref, sem):
        def slab_copy(c, slot):
            return pltpu.make_async_copy(src_hbm.at[pl.ds(c * slab, slab), :],
                                         stage_ref.at[slot], sem.at[slot])

        for c in range(min(STAGE_SLOTS - 1, n_slabs)):
            slab_copy(c, c).start()

        def step(c, carry):
            ahead = c + STAGE_SLOTS - 1

            @pl.when(ahead < n_slabs)
            def _start_ahead():
                slab_copy(ahead, ahead % STAGE_SLOTS).start()

            slot = c % STAGE_SLOTS
            slab_copy(c, slot).wait()
            dst_ref[pl.ds(pl.multiple_of(c * slab, slab), slab), :] = stage_ref[slot].astype(BF16)
            return carry

        lax.fori_loop(0, n_slabs, step, 0)

    pl.run_scoped(staged, pltpu.VMEM((STAGE_SLOTS, slab, cols), F32),
                  pltpu.SemaphoreType.DMA((STAGE_SLOTS,)))


def _ffn_proj_kernel(x_ref, g1_ref, wup_hbm, wdn_hbm, gmix_ref, win_hbm, bg_ref, *refs):
    later_f32 = refs[:N_LATER_WEIGHTS]
    x1_ref, q_ref, kv_ref, f_ref, gate_ref = refs[N_LATER_WEIGHTS:N_LATER_WEIGHTS + 5]
    later_bf16 = refs[N_LATER_WEIGHTS + 5:2 * N_LATER_WEIGHTS + 5]
    act_ref, wup_ref, wdn_ref, win_ref = refs[2 * N_LATER_WEIGHTS + 5:]

    @pl.when(pl.program_id(0) == 0)
    def _stage_own_weights():
        _stage_weight(wup_hbm, wup_ref)
        _stage_weight(wdn_hbm, wdn_ref)
        _stage_weight(win_hbm, win_ref)

    for src, dst in zip(later_f32, later_bf16):
        dst[...] = src[...].astype(BF16)

    def project(k, rows, x1):
        x1_ref[rows, :] = x1
        h2 = _rms(x1, gmix_ref[...]).astype(BF16)
        q_ref[rows, :] = (_dot(h2, win_ref[:, 0:O_K]) * (HD ** -0.5 * LOG2E)).astype(BF16)
        kv_ref[rows, :] = _dot(h2, win_ref[:, O_K:O_F]).astype(BF16)
        f_ref[rows, :] = _dot(h2, win_ref[:, O_F:O_G]).astype(BF16)
        for c in range(2):
            lo = c * D_MODEL
            z = _dot(h2, win_ref[:, O_G + lo:O_G + lo + D_MODEL]) + bg_ref[:, lo:lo + D_MODEL]
            gate_ref[rows, lo:lo + D_MODEL] = _sigmoid(z).astype(BF16)

    _skewed_ffn(x_ref, g1_ref, wup_ref, wdn_ref, act_ref, project)


def _ffn_final_kernel(x_ref, g2_ref, wup_ref, wdn_ref, gfin_ref, out_ref, act_ref):
    def final_norm(k, rows, x3):
        out_ref[rows, :] = _rms(x3, gfin_ref[...])

    _skewed_ffn(x_ref, g2_ref, wup_ref, wdn_ref, act_ref, final_norm)


def _mixer_kernel(relb_ref, sink_ref, x1_ref, q_ref, kv_ref, f_ref, gate_ref, bucket_ref,
                  dftc_ref, dfts_ref, wa_ref, wb_ref, wout_ref, out_ref,
                  bias_ref, kpad_ref, vt_ref, frev_ref, uw_ref, o_ref, s_ref, p_ref,
                  fmix_ref, yb_ref):
    b = pl.program_id(0)
    t = pl.program_id(1)

    @pl.when((b == 0) & (t == 0))
    def _build_bias():
        bucket = bucket_ref[...]
        kj = lax.broadcasted_iota(jnp.int32, (3 * BLK, BLK), 0)
        qi = lax.broadcasted_iota(jnp.int32, (3 * BLK, BLK), 1)
        in_window = jnp.abs(kj - BLK - qi) <= WINDOW
        valid = (in_window, in_window & (kj >= BLK), in_window & (kj < 2 * BLK))
        for h in range(N_HEADS):
            def pick(bk, acc, h=h):
                return jnp.where(bucket == bk, relb_ref[bk, h] * LOG2E, acc)
            base = lax.fori_loop(0, N_BUCKETS, pick, jnp.zeros((3 * BLK, BLK), F32))
            for v in range(3):
                bias_ref[v, h] = jnp.where(valid[v], base, -jnp.inf)

    @pl.when(t == 0)
    def _per_sequence():
        kpad_ref[0:BLK, :] = jnp.zeros((BLK, KV_WIDTH), BF16)
        kpad_ref[BLK:BLK + SEQ, :] = kv_ref[:, :KV_WIDTH]
        kpad_ref[BLK + SEQ:, :] = jnp.zeros((BLK, KV_WIDTH), BF16)
        vt_ref[:, 0:BLK] = jnp.zeros((N_KV * VT_ROWS, BLK), BF16)
        vt_ref[:, BLK + SEQ:] = jnp.zeros((N_KV * VT_ROWS, BLK), BF16)
        for c in range(SEQ // BLK):
            v_t = kv_ref[c * BLK:(c + 1) * BLK, KV_WIDTH:].T
            for kh in range(N_KV):
                vt_ref[kh * VT_ROWS:kh * VT_ROWS + HD, (c + 1) * BLK:(c + 2) * BLK] = (
                    v_t[kh * HD:(kh + 1) * HD])
        for kh in range(N_KV):
            vt_ref[kh * VT_ROWS + HD:(kh + 1) * VT_ROWS, :] = jnp.ones(
                (VT_ROWS - HD, SEQ + 2 * BLK), BF16)
        jj = lax.broadcasted_iota(jnp.int32, (BLK, 2 * BLK), 0)
        cc = lax.broadcasted_iota(jnp.int32, (BLK, 2 * BLK), 1)
        flip = jnp.where(jj + cc == BLK, 1.0, 0.0).astype(BF16)
        for blk in range(HALF_SEQ // BLK):
            src = SEQ - (blk + 1) * BLK
            if blk == 0:
                rev = _dot(flip[:, :BLK], f_ref[src:src + BLK, :])
            else:
                rev = _dot(flip, f_ref[src:src + 2 * BLK, :])
            frev_ref[blk * BLK:(blk + 1) * BLK, :] = rev.astype(BF16)
        first_row = lax.broadcasted_iota(jnp.int32, (V7X_BF16_SUBLANES, FG_DIM), 0) == 0
        for g in range(N_FGROUPS):
            lo = g * FG_DIM
            both = jnp.concatenate([f_ref[0:HALF_SEQ, lo:lo + FG_DIM],
                                    frev_ref[:, lo:lo + FG_DIM]], axis=1)
            r = _dot(both, dftc_ref[...])
            uw_ref[0:HALF_SEQ, lo:lo + FG_DIM] = r[:, :FG_DIM].astype(BF16)
            uw_ref[HALF_SEQ:, lo:lo + FG_DIM] = r[:, FG_DIM:].astype(BF16)
            mid = _dot(f_ref[HALF_SEQ:HALF_SEQ + V7X_BF16_SUBLANES, lo:lo + FG_DIM],
                       dftc_ref[0:FG_DIM, 0:FG_DIM])
            head = uw_ref[HALF_SEQ:HALF_SEQ + V7X_BF16_SUBLANES, lo:lo + FG_DIM]
            uw_ref[HALF_SEQ:HALF_SEQ + V7X_BF16_SUBLANES, lo:lo + FG_DIM] = jnp.where(
                first_row, mid, head.astype(F32)).astype(BF16)

    def block_keys(i):
        n = t * BLOCKS_PER_TILE + i
        variant = jnp.where(n == 0, 1, jnp.where(n == SEQ // BLK - 1, 2, 0))
        return pl.multiple_of(n * BLK, BLK), variant

    def scores(i):
        key0, variant = block_keys(i)
        q_t = q_ref[i * BLK:(i + 1) * BLK, :].T
        zero = jnp.zeros((HD, BLK), BF16)
        cols = []
        for h in range(N_HEADS):
            q_h = q_t[h * HD:(h + 1) * HD]
            cols.append(jnp.concatenate([q_h, zero] if h < GQ else [zero, q_h], axis=0))
        s_t = _dot(kpad_ref[pl.ds(key0, 3 * BLK), :], jnp.concatenate(cols, axis=1))
        tops = []
        for h in range(N_HEADS):
            s_h = s_t[:, h * BLK:(h + 1) * BLK]
            s_ref[i % 2, h] = s_h
            top = s_h[0:KEY_CHUNK] + bias_ref[variant, h, 0:KEY_CHUNK, :]
            for k0 in range(KEY_CHUNK, 3 * BLK, KEY_CHUNK):
                top = jnp.maximum(top, s_h[k0:k0 + KEY_CHUNK] + bias_ref[variant, h, k0:k0 + KEY_CHUNK, :])
            tops.append(jnp.max(top, axis=0, keepdims=True))
        return tops

    def softmax(i, tops):
        _, variant = block_keys(i)
        buf = i % 2
        sink_terms = []
        for h in range(N_HEADS):
            sink = sink_ref[h] * LOG2E
            m = jnp.maximum(tops[h], sink)
            for k0 in range(0, 3 * BLK, KEY_CHUNK):
                z = (s_ref[buf, h, k0:k0 + KEY_CHUNK, :] - m) + bias_ref[variant, h, k0:k0 + KEY_CHUNK, :]
                p_ref[buf, h, k0:k0 + KEY_CHUNK, :] = jnp.exp2(z).astype(BF16)
            sink_terms.append(jnp.exp2(sink - m))
        return sink_terms

    def weighted_values(i, sink_terms):
        key0, _ = block_keys(i)
        for kh in range(N_KV):
            p_group = jnp.concatenate([p_ref[i % 2, kh * GQ + g] for g in range(GQ)], axis=1)
            o_t = _dot(vt_ref[kh * VT_ROWS:(kh + 1) * VT_ROWS, pl.ds(key0, 3 * BLK)], p_group)
            den = o_t[HD:HD + 1] + jnp.concatenate(sink_terms[kh * GQ:(kh + 1) * GQ], axis=1)
            o_n = o_t[:HD] / den
            for pr in range(GQ // 2):
                pair_t = jnp.concatenate([o_n[:, 2 * pr * BLK:(2 * pr + 1) * BLK],
                                          o_n[:, (2 * pr + 1) * BLK:(2 * pr + 2) * BLK]], axis=0)
                lane0 = (kh * GQ + 2 * pr) * HD
                o_ref[i * BLK:(i + 1) * BLK, lane0:lane0 + 2 * HD] = pair_t.T.astype(BF16)

    half_tile = ROW_TILE // 2
    tops = [scores(0)] + [None] * (BLOCKS_PER_TILE - 1)
    for i in range(BLOCKS_PER_TILE):
        if i + 1 < BLOCKS_PER_TILE:
            tops[i + 1] = scores(i + 1)
        if i < 2:
            rows = slice(i * half_tile, (i + 1) * half_tile)
            freq0 = pl.multiple_of(t * ROW_TILE + i * half_tile, half_tile)
            fmix_ref[rows, :] = _dot(dfts_ref[pl.ds(freq0, half_tile), :], uw_ref[...]).astype(BF16)
        elif i == 2:
            yb_ref[...] = _dot(fmix_ref[...], wb_ref[...])
        weighted_values(i, softmax(i, tops[i]))

    y_a = _dot(o_ref[...], wa_ref[...])
    mix = (gate_ref[:, :D_MODEL].astype(F32) * y_a + gate_ref[:, D_MODEL:].astype(F32) * yb_ref[...])
    out_ref[...] = x1_ref[...] + _dot(mix.astype(BF16), wout_ref[...])


def _resident(shape):
    return pl.BlockSpec(shape, lambda *_: (0,) * len(shape), pipeline_mode=pl.Buffered(1))


def _rows(width, tile=ROW_TILE):
    return pl.BlockSpec((tile, width), lambda i: (i, 0))


def _slab_spec(shape):
    rows, cols = shape
    steps = TOKENS // ROW_TILE
    slab = next(r for r in range(V7X_BF16_SUBLANES, rows + 1, V7X_BF16_SUBLANES)
                if rows % r == 0 and r * steps >= rows)
    return pl.BlockSpec((slab, cols), lambda i: (jnp.minimum(i, rows // slab - 1), 0))


def _ffn_proj(x, g1, wup, wdn, gmix, win, bg, later_weights):
    assert len(later_weights) == N_LATER_WEIGHTS
    slabs = [_slab_spec(w.shape) for w in later_weights]
    hbm = pl.BlockSpec(memory_space=pl.ANY)
    return pl.pallas_call(
        _ffn_proj_kernel,
        grid=(TOKENS // ROW_TILE,),
        in_specs=[_rows(D_MODEL), _resident((1, D_MODEL)), hbm, hbm, _resident((1, D_MODEL)),
                  hbm, _resident((1, 2 * D_MODEL))] + slabs,
        out_specs=[_rows(D_MODEL), _rows(Q_WIDTH), _rows(2 * KV_WIDTH), _rows(F_WIDTH),
                   _rows(2 * D_MODEL)] + slabs,
        out_shape=[jax.ShapeDtypeStruct((TOKENS, D_MODEL), F32),
                   jax.ShapeDtypeStruct((TOKENS, Q_WIDTH), BF16),
                   jax.ShapeDtypeStruct((TOKENS, 2 * KV_WIDTH), BF16),
                   jax.ShapeDtypeStruct((TOKENS, F_WIDTH), BF16),
                   jax.ShapeDtypeStruct((TOKENS, 2 * D_MODEL), BF16)]
        + [jax.ShapeDtypeStruct(w.shape, BF16) for w in later_weights],
        scratch_shapes=[pltpu.VMEM((ROW_TILE, D_FF), BF16),
                        pltpu.VMEM((D_MODEL, 2 * D_FF), BF16),
                        pltpu.VMEM((D_FF, D_MODEL), BF16),
                        pltpu.VMEM((D_MODEL, IN_WIDTH), BF16)],
        compiler_params=pltpu.CompilerParams(dimension_semantics=("arbitrary",),
                                             vmem_limit_bytes=V7X_VMEM_LIMIT_BYTES),
        name="ffn_proj",
    )(x, g1, wup, wdn, gmix, win, bg, *later_weights)


def _ffn_final(x, g2, wup, wdn, gfin):
    return pl.pallas_call(
        _ffn_final_kernel,
        grid=(TOKENS // FINAL_ROW_TILE,),
        in_specs=[_rows(D_MODEL, FINAL_ROW_TILE), _resident((1, D_MODEL)),
                  _resident((D_MODEL, 2 * D_FF)), _resident((D_FF, D_MODEL)),
                  _resident((1, D_MODEL))],
        out_specs=_rows(D_MODEL, FINAL_ROW_TILE),
        out_shape=jax.ShapeDtypeStruct((TOKENS, D_MODEL), F32),
        scratch_shapes=[pltpu.VMEM((FINAL_ROW_TILE, D_FF), BF16)],
        compiler_params=pltpu.CompilerParams(dimension_semantics=("arbitrary",),
                                             vmem_limit_bytes=V7X_VMEM_LIMIT_BYTES),
        name="ffn_final",
    )(x, g2, wup, wdn, gfin)


def _mixer(rel_bias, sink, x1, q, kv, f, gates, bucket, dftc, dfts, wa, wb, wout):
    def tile_rows(width):
        return pl.BlockSpec((ROW_TILE, width), lambda b, t: (b * TILES_PER_SEQ + t, 0))

    def seq_rows(width):
        return pl.BlockSpec((SEQ, width), lambda b, t: (b, 0))

    smem = pl.BlockSpec(memory_space=pltpu.SMEM)
    return pl.pallas_call(
        _mixer_kernel,
        grid=(BATCH, TILES_PER_SEQ),
        in_specs=[smem, smem, tile_rows(D_MODEL), tile_rows(Q_WIDTH), seq_rows(2 * KV_WIDTH),
                  seq_rows(F_WIDTH), tile_rows(2 * D_MODEL), _resident((3 * BLK, BLK)),
                  _resident((2 * FG_DIM, 2 * FG_DIM)),
                  _resident((SEQ, SEQ)),
                  _resident((Q_WIDTH, D_MODEL)), _resident((F_WIDTH, D_MODEL)),
                  _resident((D_MODEL, D_MODEL))],
        out_specs=tile_rows(D_MODEL),
        out_shape=jax.ShapeDtypeStruct((TOKENS, D_MODEL), F32),
        scratch_shapes=[pltpu.VMEM((3, N_HEADS, 3 * BLK, BLK), F32),
                        pltpu.VMEM((SEQ + 2 * BLK, KV_WIDTH), BF16),
                        pltpu.VMEM((N_KV * VT_ROWS, SEQ + 2 * BLK), BF16),
                        pltpu.VMEM((HALF_SEQ, F_WIDTH), BF16),
                        pltpu.VMEM((SEQ, F_WIDTH), BF16),
                        pltpu.VMEM((ROW_TILE, Q_WIDTH), BF16),
                        pltpu.VMEM((2, N_HEADS, 3 * BLK, BLK), F32),
                        pltpu.VMEM((2, N_HEADS, 3 * BLK, BLK), BF16),
                        pltpu.VMEM((ROW_TILE, F_WIDTH), BF16),
                        pltpu.VMEM((ROW_TILE, D_MODEL), F32)],
        compiler_params=pltpu.CompilerParams(dimension_semantics=("arbitrary", "arbitrary"),
                                             vmem_limit_bytes=V7X_VMEM_LIMIT_BYTES),
        name="mixer",
    )(rel_bias, sink, x1, q, kv, f, gates, bucket, dftc, dfts, wa, wb, wout)


def kernel(x, g_ffn1, w_up1, w_down1, g_mix, w_in, b_gate, sink, rel_bias, w_branch_a, w_branch_b, w_out, g_ffn2, w_up2, w_down2, g_final):
    assert x.shape == (BATCH, SEQ, D_MODEL) and w_up1.shape[0] == DEPTH == 1
    chan, pos = _dft_tables()
    bucket = jnp.asarray(_t5_bucket_table().T)
    dftc = jnp.asarray(chan).astype(BF16)
    dfts = jnp.asarray(pos).astype(BF16)
    row = lambda v: v.reshape(1, -1)
    x0 = x.reshape(TOKENS, D_MODEL)
    x1, q, kv, f, gates, wa, wb, wout, wup2, wdn2 = _ffn_proj(
        x0, row(g_ffn1[0]), w_up1[0], w_down1[0], row(g_mix[0]), w_in[0], row(b_gate[0]),
        [w_branch_a[0], w_branch_b[0], w_out[0], w_up2[0], w_down2[0]])
    x2 = _mixer(rel_bias, sink[0], x1, q, kv, f, gates, bucket, dftc, dfts, wa, wb, wout)
    out = _ffn_final(x2, row(g_ffn2[0]), wup2, wdn2, row(g_final))
    return out.reshape(BATCH, SEQ, D_MODEL)
```

```python
import functools

import numpy as np
import jax
import jax.numpy as jnp
from jax import lax
from jax.experimental import pallas as pl
from jax.experimental.pallas import tpu as pltpu

D_MODEL = 1024
BATCH = 8
SEQ = 2048
DEPTH = 1
N_HEADS = 8
N_KV = 2
GQ = N_HEADS // N_KV
HD = 64
Q_WIDTH = N_HEADS * HD
KV_WIDTH = N_KV * HD
WINDOW = 128
BLK = 128
N_FGROUPS = 4
FG_DIM = 128
F_WIDTH = N_FGROUPS * FG_DIM
N_BUCKETS = 32
MAX_DIST = 128
D_FF = 2816
EPS = 1e-6
O_K = Q_WIDTH
O_V = O_K + KV_WIDTH
O_F = O_V + KV_WIDTH
O_G = O_F + F_WIDTH
IN_WIDTH = O_G + 2 * D_MODEL

F32 = jnp.float32
BF16 = jnp.bfloat16

V7X_MXU_DIM = 256
V7X_BF16_SUBLANES = 16
V7X_VMEM_LIMIT_BYTES = 56 * 1024 * 1024

TOKENS = BATCH * SEQ
HALF_SEQ = SEQ // 2
ROW_TILE = 512
FINAL_ROW_TILE = 1024
TILES_PER_SEQ = SEQ // ROW_TILE
BLOCKS_PER_TILE = ROW_TILE // BLK
FF_CHUNK = V7X_MXU_DIM
FFN_SUB_ROWS = 256
VT_ROWS = HD + V7X_BF16_SUBLANES
N_LATER_WEIGHTS = 5
STAGE_SLOTS = 4
STAGE_BYTES = 3 * 512 * 1024
KEY_CHUNK = 64
LOG2E = float(np.log2(np.e))


def _t5_bucket_table():
    rel = (np.arange(3 * BLK)[None, :] - BLK) - np.arange(BLK)[:, None]
    half = N_BUCKETS // 2
    max_exact = half // 2
    ret = (rel > 0).astype(np.int32) * half
    n = np.abs(rel)
    n_safe = np.maximum(n, 1).astype(np.float32)
    large = max_exact + (np.log(n_safe / max_exact) / np.log(MAX_DIST / max_exact)
                         * (half - max_exact)).astype(np.int32)
    large = np.minimum(large, half - 1)
    return (ret + np.where(n < max_exact, n, large)).astype(np.int32)


@functools.lru_cache(maxsize=None)
def _dft_tables():
    def cos_sin(n):
        idx = np.arange(n)
        ang = 2.0 * np.pi * ((idx[:, None] * idx[None, :]) % n).astype(np.float64) / n
        return np.cos(ang) / np.sqrt(n), np.sin(ang) / np.sqrt(n)
    cc, sc = cos_sin(FG_DIM)
    cs, ss = cos_sin(SEQ)
    chan = np.block([[cc, sc], [cc, -sc]]).astype(np.float32)
    neg_sin = -ss[:, :HALF_SEQ]
    neg_sin[:, 0] = np.where(np.arange(SEQ) % 2 == 0, 1.0, -1.0) / np.sqrt(SEQ)
    pos = np.concatenate([cs[:, :HALF_SEQ], neg_sin], axis=1).astype(np.float32)
    return chan, pos


def _rms(x, g):
    return x * lax.rsqrt(jnp.mean(x * x, axis=-1, keepdims=True) + EPS) * g


def _sigmoid(z):
    return 1.0 / (1.0 + jnp.exp2(z * -LOG2E))


def _dot(a, b):
    return jnp.dot(a, b, preferred_element_type=F32)


def _skewed_ffn(x_ref, g_ref, wup_ref, wdn_ref, act_ref, finish, finish_late=None):
    n_sub = x_ref.shape[0] // FFN_SUB_ROWS
    rows = [slice(k * FFN_SUB_ROWS, (k + 1) * FFN_SUB_ROWS) for k in range(n_sub)]
    normed = [None] * n_sub

    def prologue(k):
        normed[k] = _rms(x_ref[rows[k], :], g_ref[...]).astype(BF16)

    def up_chunk(k, c):
        lo = c * FF_CHUNK
        gate = _dot(normed[k], wup_ref[:, lo:lo + FF_CHUNK])
        up = _dot(normed[k], wup_ref[:, D_FF + lo:D_FF + lo + FF_CHUNK])
        act_ref[rows[k], lo:lo + FF_CHUNK] = (gate * _sigmoid(gate) * up).astype(BF16)

    def epilogue(k):
        finish(k, rows[k], x_ref[rows[k], :] + 0.5 * _dot(act_ref[rows[k], :], wdn_ref[...]))

    prologue(0)
    for k in range(n_sub):
        for c in range(D_FF // FF_CHUNK):
            up_chunk(k, c)
            if c == 0 and k + 1 < n_sub:
                prologue(k + 1)
            if c == 1 and k > 0:
                epilogue(k - 1)
            if c == 3 and k > 0 and finish_late is not None:
                finish_late(k - 1, rows[k - 1])
    epilogue(n_sub - 1)
    if finish_late is not None:
        finish_late(n_sub - 1, rows[n_sub - 1])


def _stage_weight(src_hbm, dst_ref):
    rows, cols = dst_ref.shape
    slab = max(r for r in range(V7X_BF16_SUBLANES, rows + 1, V7X_BF16_SUBLANES)
               if rows % r == 0 and r * cols * 4 <= STAGE_BYTES)
    n_slabs = rows // slab

    def staged(stage_ref, sem):
        def slab_copy(c, slot):
            return pltpu.make_async_copy(src_hbm.at[pl.ds(c * slab, slab), :],
                                         stage_ref.at[slot], sem.at[slot])

        for c in range(min(STAGE_SLOTS - 1, n_slabs)):
            slab_copy(c, c).start()

        def step(c, carry):
            ahead = c + STAGE_SLOTS - 1

            @pl.when(ahead < n_slabs)
            def _start_ahead():
                slab_copy(ahead, ahead % STAGE_SLOTS).start()

            slot = c % STAGE_SLOTS
            slab_copy(c, slot).wait()
            dst_ref[pl.ds(pl.multiple_of(c * slab, slab), slab), :] = stage_ref[slot].astype(BF16)
            return carry

        lax.fori_loop(0, n_slabs, step, 0)

    pl.run_scoped(staged, pltpu.VMEM((STAGE_SLOTS, slab, cols), F32),
                  pltpu.SemaphoreType.DMA((STAGE_SLOTS,)))


def _ffn_proj_kernel(x_ref, g1_ref, wup_hbm, wdn_hbm, gmix_ref, win_hbm, bg_ref, *refs):
    later_f32 = refs[:N_LATER_WEIGHTS]
    x1_ref, q_ref, kv_ref, f_ref, gate_ref = refs[N_LATER_WEIGHTS:N_LATER_WEIGHTS + 5]
    later_bf16 = refs[N_LATER_WEIGHTS + 5:2 * N_LATER_WEIGHTS + 5]
    act_ref, h2_ref, wup_ref, wdn_ref, win_ref = refs[2 * N_LATER_WEIGHTS + 5:]

    @pl.when(pl.program_id(0) == 0)
    def _stage_own_weights():
        _stage_weight(wup_hbm, wup_ref)
        _stage_weight(wdn_hbm, wdn_ref)
        _stage_weight(win_hbm, win_ref)

    for src, dst in zip(later_f32, later_bf16):
        dst[...] = src[...].astype(BF16)

    def second_norm(k, rows, x1):
        x1_ref[rows, :] = x1
        h2_ref[rows, :] = _rms(x1, gmix_ref[...]).astype(BF16)

    def project(k, rows):
        h2 = h2_ref[rows, :]
        q_ref[rows, :] = (_dot(h2, win_ref[:, 0:O_K]) * (HD ** -0.5 * LOG2E)).astype(BF16)
        kv_ref[rows, :] = _dot(h2, win_ref[:, O_K:O_F]).astype(BF16)
        f_ref[rows, :] = _dot(h2, win_ref[:, O_F:O_G]).astype(BF16)
        for c in range(2):
            lo = c * D_MODEL
            z = _dot(h2, win_ref[:, O_G + lo:O_G + lo + D_MODEL]) + bg_ref[:, lo:lo + D_MODEL]
            gate_ref[rows, lo:lo + D_MODEL] = _sigmoid(z).astype(BF16)

    _skewed_ffn(x_ref, g1_ref, wup_ref, wdn_ref, act_ref, second_norm, finish_late=project)


def _ffn_final_kernel(x_ref, g2_ref, wup_ref, wdn_ref, gfin_ref, out_ref, act_ref):
    def final_norm(k, rows, x3):
        out_ref[rows, :] = _rms(x3, gfin_ref[...])

    _skewed_ffn(x_ref, g2_ref, wup_ref, wdn_ref, act_ref, final_norm)


def _mixer_kernel(relb_ref, sink_ref, x1_ref, q_ref, kv_ref, f_ref, gate_ref, bucket_ref,
                  dftc_ref, dfts_ref, wa_ref, wb_ref, wout_ref, out_ref,
                  bias_ref, kpad_ref, vt_ref, frev_ref, uw_ref, o_ref, s_ref, p_ref,
                  fmix_ref, yb_ref):
    b = pl.program_id(0)
    t = pl.program_id(1)

    @pl.when((b == 0) & (t == 0))
    def _build_bias():
        bucket = bucket_ref[...]
        kj = lax.broadcasted_iota(jnp.int32, (3 * BLK, BLK), 0)
        qi = lax.broadcasted_iota(jnp.int32, (3 * BLK, BLK), 1)
        in_window = jnp.abs(kj - BLK - qi) <= WINDOW
        valid = (in_window, in_window & (kj >= BLK), in_window & (kj < 2 * BLK))
        for h in range(N_HEADS):
            def pick(bk, acc, h=h):
                return jnp.where(bucket == bk, relb_ref[bk, h] * LOG2E, acc)
            base = lax.fori_loop(0, N_BUCKETS, pick, jnp.zeros((3 * BLK, BLK), F32))
            for v in range(3):
                bias_ref[v, h] = jnp.where(valid[v], base, -jnp.inf)

    @pl.when(t == 0)
    def _per_sequence():
        kpad_ref[0:BLK, :] = jnp.zeros((BLK, KV_WIDTH), BF16)
        kpad_ref[BLK:BLK + SEQ, :] = kv_ref[:, :KV_WIDTH]
        kpad_ref[BLK + SEQ:, :] = jnp.zeros((BLK, KV_WIDTH), BF16)
        vt_ref[:, 0:BLK] = jnp.zeros((N_KV * VT_ROWS, BLK), BF16)
        vt_ref[:, BLK + SEQ:] = jnp.zeros((N_KV * VT_ROWS, BLK), BF16)
        for c in range(SEQ // BLK):
            v_t = kv_ref[c * BLK:(c + 1) * BLK, KV_WIDTH:].T
            for kh in range(N_KV):
                vt_ref[kh * VT_ROWS:kh * VT_ROWS + HD, (c + 1) * BLK:(c + 2) * BLK] = (
                    v_t[kh * HD:(kh + 1) * HD])
        for kh in range(N_KV):
            vt_ref[kh * VT_ROWS + HD:(kh + 1) * VT_ROWS, :] = jnp.ones(
                (VT_ROWS - HD, SEQ + 2 * BLK), BF16)
        jj = lax.broadcasted_iota(jnp.int32, (BLK, 2 * BLK), 0)
        cc = lax.broadcasted_iota(jnp.int32, (BLK, 2 * BLK), 1)
        flip = jnp.where(jj + cc == BLK, 1.0, 0.0).astype(BF16)
        for blk in range(HALF_SEQ // BLK):
            src = SEQ - (blk + 1) * BLK
            if blk == 0:
                rev = _dot(flip[:, :BLK], f_ref[src:src + BLK, :])
            else:
                rev = _dot(flip, f_ref[src:src + 2 * BLK, :])
            frev_ref[blk * BLK:(blk + 1) * BLK, :] = rev.astype(BF16)
        first_row = lax.broadcasted_iota(jnp.int32, (V7X_BF16_SUBLANES, FG_DIM), 0) == 0
        for g in range(N_FGROUPS):
            lo = g * FG_DIM
            both = jnp.concatenate([f_ref[0:HALF_SEQ, lo:lo + FG_DIM],
                                    frev_ref[:, lo:lo + FG_DIM]], axis=1)
            r = _dot(both, dftc_ref[...])
            uw_ref[0:HALF_SEQ, lo:lo + FG_DIM] = r[:, :FG_DIM].astype(BF16)
            uw_ref[HALF_SEQ:, lo:lo + FG_DIM] = r[:, FG_DIM:].astype(BF16)
            mid = _dot(f_ref[HALF_SEQ:HALF_SEQ + V7X_BF16_SUBLANES, lo:lo + FG_DIM],
                       dftc_ref[0:FG_DIM, 0:FG_DIM])
            head = uw_ref[HALF_SEQ:HALF_SEQ + V7X_BF16_SUBLANES, lo:lo + FG_DIM]
            uw_ref[HALF_SEQ:HALF_SEQ + V7X_BF16_SUBLANES, lo:lo + FG_DIM] = jnp.where(
                first_row, mid, head.astype(F32)).astype(BF16)

    def block_keys(i):
        n = t * BLOCKS_PER_TILE + i
        variant = jnp.where(n == 0, 1, jnp.where(n == SEQ // BLK - 1, 2, 0))
        return pl.multiple_of(n * BLK, BLK), variant

    def scores(i):
        key0, variant = block_keys(i)
        q_t = q_ref[i * BLK:(i + 1) * BLK, :].T
        zero = jnp.zeros((HD, BLK), BF16)
        cols = []
        for h in range(N_HEADS):
            q_h = q_t[h * HD:(h + 1) * HD]
            cols.append(jnp.concatenate([q_h, zero] if h < GQ else [zero, q_h], axis=0))
        s_t = _dot(kpad_ref[pl.ds(key0, 3 * BLK), :], jnp.concatenate(cols, axis=1))
        tops = []
        for h in range(N_HEADS):
            s_h = s_t[:, h * BLK:(h + 1) * BLK]
            s_ref[i % 2, h] = s_h
            top = s_h[0:KEY_CHUNK] + bias_ref[variant, h, 0:KEY_CHUNK, :]
            for k0 in range(KEY_CHUNK, 3 * BLK, KEY_CHUNK):
                top = jnp.maximum(top, s_h[k0:k0 + KEY_CHUNK] + bias_ref[variant, h, k0:k0 + KEY_CHUNK, :])
            tops.append(jnp.max(top, axis=0, keepdims=True))
        return tops

    def softmax(i, tops):
        _, variant = block_keys(i)
        buf = i % 2
        sink_terms = []
        for h in range(N_HEADS):
            sink = sink_ref[h] * LOG2E
            m = jnp.maximum(tops[h], sink)
            for k0 in range(0, 3 * BLK, KEY_CHUNK):
                z = (s_ref[buf, h, k0:k0 + KEY_CHUNK, :] - m) + bias_ref[variant, h, k0:k0 + KEY_CHUNK, :]
                p_ref[buf, h, k0:k0 + KEY_CHUNK, :] = jnp.exp2(z).astype(BF16)
            sink_terms.append(jnp.exp2(sink - m))
        return sink_terms

    def weighted_values(i, sink_terms):
        key0, _ = block_keys(i)
        for kh in range(N_KV):
            p_group = jnp.concatenate([p_ref[i % 2, kh * GQ + g] for g in range(GQ)], axis=1)
            o_t = _dot(vt_ref[kh * VT_ROWS:(kh + 1) * VT_ROWS, pl.ds(key0, 3 * BLK)], p_group)
            den = o_t[HD:HD + 1] + jnp.concatenate(sink_terms[kh * GQ:(kh + 1) * GQ], axis=1)
            o_n = o_t[:HD] / den
            for pr in range(GQ // 2):
                pair_t = jnp.concatenate([o_n[:, 2 * pr * BLK:(2 * pr + 1) * BLK],
                                          o_n[:, (2 * pr + 1) * BLK:(2 * pr + 2) * BLK]], axis=0)
                lane0 = (kh * GQ + 2 * pr) * HD
                o_ref[i * BLK:(i + 1) * BLK, lane0:lane0 + 2 * HD] = pair_t.T.astype(BF16)

    half_tile = ROW_TILE // 2
    tops = [scores(0)] + [None] * (BLOCKS_PER_TILE - 1)
    for i in range(BLOCKS_PER_TILE):
        if i + 1 < BLOCKS_PER_TILE:
            tops[i + 1] = scores(i + 1)
        if i < 2:
            rows = slice(i * half_tile, (i + 1) * half_tile)
            freq0 = pl.multiple_of(t * ROW_TILE + i * half_tile, half_tile)
            fmix_ref[rows, :] = _dot(dfts_ref[pl.ds(freq0, half_tile), :], uw_ref[...]).astype(BF16)
        elif i == 2:
            yb_ref[...] = _dot(fmix_ref[...], wb_ref[...])
        weighted_values(i, softmax(i, tops[i]))

    y_a = _dot(o_ref[...], wa_ref[...])
    mix = (gate_ref[:, :D_MODEL].astype(F32) * y_a + gate_ref[:, D_MODEL:].astype(F32) * yb_ref[...])
    out_ref[...] = x1_ref[...] + _dot(mix.astype(BF16), wout_ref[...])


def _resident(shape):
    return pl.BlockSpec(shape, lambda *_: (0,) * len(shape), pipeline_mode=pl.Buffered(1))


def _rows(width, tile=ROW_TILE):
    return pl.BlockSpec((tile, width), lambda i: (i, 0))


def _slab_spec(shape):
    rows, cols = shape
    steps = TOKENS // ROW_TILE
    slab = next(r for r in range(V7X_BF16_SUBLANES, rows + 1, V7X_BF16_SUBLANES)
                if rows % r == 0 and r * steps >= rows)
    return pl.BlockSpec((slab, cols), lambda i: (jnp.minimum(i, rows // slab - 1), 0))


def _ffn_proj(x, g1, wup, wdn, gmix, win, bg, later_weights):
    assert len(later_weights) == N_LATER_WEIGHTS
    slabs = [_slab_spec(w.shape) for w in later_weights]
    hbm = pl.BlockSpec(memory_space=pl.ANY)
    return pl.pallas_call(
        _ffn_proj_kernel,
        grid=(TOKENS // ROW_TILE,),
        in_specs=[_rows(D_MODEL), _resident((1, D_MODEL)), hbm, hbm, _resident((1, D_MODEL)),
                  hbm, _resident((1, 2 * D_MODEL))] + slabs,
        out_specs=[_rows(D_MODEL), _rows(Q_WIDTH), _rows(2 * KV_WIDTH), _rows(F_WIDTH),
                   _rows(2 * D_MODEL)] + slabs,
        out_shape=[jax.ShapeDtypeStruct((TOKENS, D_MODEL), F32),
                   jax.ShapeDtypeStruct((TOKENS, Q_WIDTH), BF16),
                   jax.ShapeDtypeStruct((TOKENS, 2 * KV_WIDTH), BF16),
                   jax.ShapeDtypeStruct((TOKENS, F_WIDTH), BF16),
                   jax.ShapeDtypeStruct((TOKENS, 2 * D_MODEL), BF16)]
        + [jax.ShapeDtypeStruct(w.shape, BF16) for w in later_weights],
        scratch_shapes=[pltpu.VMEM((ROW_TILE, D_FF), BF16),
                        pltpu.VMEM((ROW_TILE, D_MODEL), BF16),
                        pltpu.VMEM((D_MODEL, 2 * D_FF), BF16),
                        pltpu.VMEM((D_FF, D_MODEL), BF16),
                        pltpu.VMEM((D_MODEL, IN_WIDTH), BF16)],
        compiler_params=pltpu.CompilerParams(dimension_semantics=("arbitrary",),
                                             vmem_limit_bytes=V7X_VMEM_LIMIT_BYTES),
        name="ffn_proj",
    )(x, g1, wup, wdn, gmix, win, bg, *later_weights)


def _ffn_final(x, g2, wup, wdn, gfin):
    return pl.pallas_call(
        _ffn_final_kernel,
        grid=(TOKENS // FINAL_ROW_TILE,),
        in_specs=[_rows(D_MODEL, FINAL_ROW_TILE), _resident((1, D_MODEL)),
                  _resident((D_MODEL, 2 * D_FF)), _resident((D_FF, D_MODEL)),
                  _resident((1, D_MODEL))],
        out_specs=_rows(D_MODEL, FINAL_ROW_TILE),
        out_shape=jax.ShapeDtypeStruct((TOKENS, D_MODEL), F32),
        scratch_shapes=[pltpu.VMEM((FINAL_ROW_TILE, D_FF), BF16)],
        compiler_params=pltpu.CompilerParams(dimension_semantics=("arbitrary",),
                                             vmem_limit_bytes=V7X_VMEM_LIMIT_BYTES),
        name="ffn_final",
    )(x, g2, wup, wdn, gfin)


def _mixer(rel_bias, sink, x1, q, kv, f, gates, bucket, dftc, dfts, wa, wb, wout):
    def tile_rows(width):
        return pl.BlockSpec((ROW_TILE, width), lambda b, t: (b * TILES_PER_SEQ + t, 0))

    def seq_rows(width):
        return pl.BlockSpec((SEQ, width), lambda b, t: (b, 0))

    smem = pl.BlockSpec(memory_space=pltpu.SMEM)
    return pl.pallas_call(
        _mixer_kernel,
        grid=(BATCH, TILES_PER_SEQ),
        in_specs=[smem, smem, tile_rows(D_MODEL), tile_rows(Q_WIDTH), seq_rows(2 * KV_WIDTH),
                  seq_rows(F_WIDTH), tile_rows(2 * D_MODEL), _resident((3 * BLK, BLK)),
                  _resident((2 * FG_DIM, 2 * FG_DIM)),
                  _resident((SEQ, SEQ)),
                  _resident((Q_WIDTH, D_MODEL)), _resident((F_WIDTH, D_MODEL)),
                  _resident((D_MODEL, D_MODEL))],
        out_specs=tile_rows(D_MODEL),
        out_shape=jax.ShapeDtypeStruct((TOKENS, D_MODEL), F32),
        scratch_shapes=[pltpu.VMEM((3, N_HEADS, 3 * BLK, BLK), F32),
                        pltpu.VMEM((SEQ + 2 * BLK, KV_WIDTH), BF16),
                        pltpu.VMEM((N_KV * VT_ROWS, SEQ + 2 * BLK), BF16),
                        pltpu.VMEM((HALF_SEQ, F_WIDTH), BF16),
                        pltpu.VMEM((SEQ, F_WIDTH), BF16),
                        pltpu.VMEM((ROW_TILE, Q_WIDTH), BF16),
                        pltpu.VMEM((2, N_HEADS, 3 * BLK, BLK), F32),
                        pltpu.VMEM((2, N_HEADS, 3 * BLK, BLK), BF16),
                        pltpu.VMEM((ROW_TILE, F_WIDTH), BF16),
                        pltpu.VMEM((ROW_TILE, D_MODEL), F32)],
        compiler_params=pltpu.CompilerParams(dimension_semantics=("arbitrary", "arbitrary"),
                                             vmem_limit_bytes=V7X_VMEM_LIMIT_BYTES),
        name="mixer",
    )(rel_bias, sink, x1, q, kv, f, gates, bucket, dftc, dfts, wa, wb, wout)


def kernel(x, g_ffn1, w_up1, w_down1, g_mix, w_in, b_gate, sink, rel_bias, w_branch_a, w_branch_b, w_out, g_ffn2, w_up2, w_down2, g_final):
    assert x.shape == (BATCH, SEQ, D_MODEL) and w_up1.shape[0] == DEPTH == 1
    chan, pos = _dft_tables()
    bucket = jnp.asarray(_t5_bucket_table().T)
    dftc = jnp.asarray(chan).astype(BF16)
    dfts = jnp.asarray(pos).astype(BF16)
    row = lambda v: v.reshape(1, -1)
    x0 = x.reshape(TOKENS, D_MODEL)
    x1, q, kv, f, gates, wa, wb, wout, wup2, wdn2 = _ffn_proj(
        x0, row(g_ffn1[0]), w_up1[0], w_down1[0], row(g_mix[0]), w_in[0], row(b_gate[0]),
        [w_branch_a[0], w_branch_b[0], w_out[0], w_up2[0], w_down2[0]])
    x2 = _mixer(rel_bias, sink[0], x1, q, kv, f, gates, bucket, dftc, dfts, wa, wb, wout)
    out = _ffn_final(x2, row(g_ffn2[0]), wup2, wdn2, row(g_final))
    return out.reshape(BATCH, SEQ, D_MODEL)
```

```python
import functools

import numpy as np
import jax
import jax.numpy as jnp
from jax import lax
from jax.experimental import pallas as pl
from jax.experimental.pallas import tpu as pltpu

D_MODEL = 1024
BATCH = 8
SEQ = 2048
DEPTH = 1
N_HEADS = 8
N_KV = 2
GQ = N_HEADS // N_KV
HD = 64
Q_WIDTH = N_HEADS * HD
KV_WIDTH = N_KV * HD
WINDOW = 128
BLK = 128
N_FGROUPS = 4
FG_DIM = 128
F_WIDTH = N_FGROUPS * FG_DIM
N_BUCKETS = 32
MAX_DIST = 128
D_FF = 2816
EPS = 1e-6
O_K = Q_WIDTH
O_V = O_K + KV_WIDTH
O_F = O_V + KV_WIDTH
O_G = O_F + F_WIDTH
IN_WIDTH = O_G + 2 * D_MODEL

F32 = jnp.float32
BF16 = jnp.bfloat16

V7X_MXU_DIM = 256
V7X_BF16_SUBLANES = 16
V7X_VMEM_LIMIT_BYTES = 56 * 1024 * 1024

TOKENS = BATCH * SEQ
HALF_SEQ = SEQ // 2
ROW_TILE = 512
FINAL_ROW_TILE = 1024
TILES_PER_SEQ = SEQ // ROW_TILE
BLOCKS_PER_TILE = ROW_TILE // BLK
FF_CHUNK = V7X_MXU_DIM
FFN_SUB_ROWS = 256
VT_ROWS = HD + V7X_BF16_SUBLANES
N_LATER_WEIGHTS = 5
STAGE_SLOTS = 4
STAGE_BYTES = 3 * 512 * 1024
KEY_CHUNK = 64
LOG2E = float(np.log2(np.e))


def _t5_bucket_table():
    rel = (np.arange(3 * BLK)[None, :] - BLK) - np.arange(BLK)[:, None]
    half = N_BUCKETS // 2
    max_exact = half // 2
    ret = (rel > 0).astype(np.int32) * half
    n = np.abs(rel)
    n_safe = np.maximum(n, 1).astype(np.float32)
    large = max_exact + (np.log(n_safe / max_exact) / np.log(MAX_DIST / max_exact)
                         * (half - max_exact)).astype(np.int32)
    large = np.minimum(large, half - 1)
    return (ret + np.where(n < max_exact, n, large)).astype(np.int32)


@functools.lru_cache(maxsize=None)
def _dft_tables():
    def cos_sin(n):
        idx = np.arange(n)
        ang = 2.0 * np.pi * ((idx[:, None] * idx[None, :]) % n).astype(np.float64) / n
        return np.cos(ang) / np.sqrt(n), np.sin(ang) / np.sqrt(n)
    cc, sc = cos_sin(FG_DIM)
    cs, ss = cos_sin(SEQ)
    chan = np.block([[cc, sc], [cc, -sc]]).astype(np.float32)
    neg_sin = -ss[:, :HALF_SEQ]
    neg_sin[:, 0] = np.where(np.arange(SEQ) % 2 == 0, 1.0, -1.0) / np.sqrt(SEQ)
    pos = np.concatenate([cs[:, :HALF_SEQ], neg_sin], axis=1).astype(np.float32)
    return chan, pos


def _rms(x, g):
    return x * lax.rsqrt(jnp.mean(x * x, axis=-1, keepdims=True) + EPS) * g


def _sigmoid(z):
    return 1.0 / (1.0 + jnp.exp2(z * -LOG2E))


def _dot(a, b):
    return jnp.dot(a, b, preferred_element_type=F32)


def _skewed_ffn(x_ref, g_ref, wup_ref, wdn_ref, act_ref, finish, finish_late=None):
    n_sub = x_ref.shape[0] // FFN_SUB_ROWS
    rows = [slice(k * FFN_SUB_ROWS, (k + 1) * FFN_SUB_ROWS) for k in range(n_sub)]
    normed = [None] * n_sub

    def prologue(k):
        normed[k] = _rms(x_ref[rows[k], :], g_ref[...]).astype(BF16)

    def up_chunk(k, c):
        lo = c * FF_CHUNK
        gate = _dot(normed[k], wup_ref[:, lo:lo + FF_CHUNK])
        up = _dot(normed[k], wup_ref[:, D_FF + lo:D_FF + lo + FF_CHUNK])
        act_ref[rows[k], lo:lo + FF_CHUNK] = (gate * _sigmoid(gate) * up).astype(BF16)

    def epilogue(k):
        finish(k, rows[k], x_ref[rows[k], :] + 0.5 * _dot(act_ref[rows[k], :], wdn_ref[...]))

    prologue(0)
    for k in range(n_sub):
        for c in range(D_FF // FF_CHUNK):
            up_chunk(k, c)
            if c == 0 and k + 1 < n_sub:
                prologue(k + 1)
            if c == 1 and k > 0:
                epilogue(k - 1)
            if c == 3 and 0 < k < n_sub - 1 and finish_late is not None:
                finish_late(k - 1, rows[k - 1])
    epilogue(n_sub - 1)
    if finish_late is not None:
        for k in range(max(n_sub - 2, 0), n_sub):
            finish_late(k, rows[k])


def _stage_weight(src_hbm, dst_ref):
    rows, cols = dst_ref.shape
    slab = max(r for r in range(V7X_BF16_SUBLANES, rows + 1, V7X_BF16_SUBLANES)
               if rows % r == 0 and r * cols * 4 <= STAGE_BYTES)
    n_slabs = rows // slab

    def staged(stage_ref, sem):
        def slab_copy(c, slot):
            return pltpu.make_async_copy(src_hbm.at[pl.ds(c * slab, slab), :],
                                         stage_ref.at[slot], sem.at[slot])

        for c in range(min(STAGE_SLOTS - 1, n_slabs)):
            slab_copy(c, c).start()

        def step(c, carry):
            ahead = c + STAGE_SLOTS - 1

            @pl.when(ahead < n_slabs)
            def _start_ahead():
                slab_copy(ahead, ahead % STAGE_SLOTS).start()

            slot = c % STAGE_SLOTS
            slab_copy(c, slot).wait()
            dst_ref[pl.ds(pl.multiple_of(c * slab, slab), slab), :] = stage_ref[slot].astype(BF16)
            return carry

        lax.fori_loop(0, n_slabs, step, 0)

    pl.run_scoped(staged, pltpu.VMEM((STAGE_SLOTS, slab, cols), F32),
                  pltpu.SemaphoreType.DMA((STAGE_SLOTS,)))


def _ffn_proj_kernel(x_ref, g1_ref, wup_hbm, wdn_hbm, gmix_ref, win_hbm, bg_ref, *refs):
    later_f32 = refs[:N_LATER_WEIGHTS]
    x1_ref, q_ref, kv_ref, f_ref, gate_ref = refs[N_LATER_WEIGHTS:N_LATER_WEIGHTS + 5]
    later_bf16 = refs[N_LATER_WEIGHTS + 5:2 * N_LATER_WEIGHTS + 5]
    act_ref, h2_ref, wup_ref, wdn_ref, win_ref = refs[2 * N_LATER_WEIGHTS + 5:]

    @pl.when(pl.program_id(0) == 0)
    def _stage_own_weights():
        _stage_weight(wup_hbm, wup_ref)
        _stage_weight(wdn_hbm, wdn_ref)
        _stage_weight(win_hbm, win_ref)

    for src, dst in zip(later_f32, later_bf16):
        dst[...] = src[...].astype(BF16)

    def second_norm(k, rows, x1):
        x1_ref[rows, :] = x1
        h2_ref[rows, :] = _rms(x1, gmix_ref[...]).astype(BF16)

    def project(k, rows):
        h2 = h2_ref[rows, :]
        q_ref[rows, :] = (_dot(h2, win_ref[:, 0:O_K]) * (HD ** -0.5 * LOG2E)).astype(BF16)
        kv_ref[rows, :] = _dot(h2, win_ref[:, O_K:O_F]).astype(BF16)
        f_ref[rows, :] = _dot(h2, win_ref[:, O_F:O_G]).astype(BF16)
        for c in range(2):
            lo = c * D_MODEL
            z = _dot(h2, win_ref[:, O_G + lo:O_G + lo + D_MODEL]) + bg_ref[:, lo:lo + D_MODEL]
            gate_ref[rows, lo:lo + D_MODEL] = _sigmoid(z).astype(BF16)

    _skewed_ffn(x_ref, g1_ref, wup_ref, wdn_ref, act_ref, second_norm, finish_late=project)


def _ffn_final_kernel(x_ref, g2_ref, wup_ref, wdn_ref, gfin_ref, out_ref, act_ref):
    def final_norm(k, rows, x3):
        out_ref[rows, :] = _rms(x3, gfin_ref[...])

    _skewed_ffn(x_ref, g2_ref, wup_ref, wdn_ref, act_ref, final_norm)


def _mixer_kernel(relb_ref, sink_ref, x1_ref, q_ref, kv_ref, f_ref, gate_ref, bucket_ref,
                  dftc_ref, dfts_ref, wa_ref, wb_ref, wout_ref, out_ref,
                  bias_ref, kpad_ref, vt_ref, frev_ref, uw_ref, o_ref, s_ref, p_ref,
                  fmix_ref, yb_ref):
    b = pl.program_id(0)
    t = pl.program_id(1)

    @pl.when((b == 0) & (t == 0))
    def _build_bias():
        bucket = bucket_ref[...]
        kj = lax.broadcasted_iota(jnp.int32, (3 * BLK, BLK), 0)
        qi = lax.broadcasted_iota(jnp.int32, (3 * BLK, BLK), 1)
        in_window = jnp.abs(kj - BLK - qi) <= WINDOW
        valid = (in_window, in_window & (kj >= BLK), in_window & (kj < 2 * BLK))
        for h in range(N_HEADS):
            def pick(bk, acc, h=h):
                return jnp.where(bucket == bk, relb_ref[bk, h] * LOG2E, acc)
            base = lax.fori_loop(0, N_BUCKETS, pick, jnp.zeros((3 * BLK, BLK), F32))
            for v in range(3):
                bias_ref[v, h] = jnp.where(valid[v], base, -jnp.inf)

    @pl.when(t == 0)
    def _per_sequence():
        kpad_ref[0:BLK, :] = jnp.zeros((BLK, KV_WIDTH), BF16)
        kpad_ref[BLK:BLK + SEQ, :] = kv_ref[:, :KV_WIDTH]
        kpad_ref[BLK + SEQ:, :] = jnp.zeros((BLK, KV_WIDTH), BF16)
        vt_ref[:, 0:BLK] = jnp.zeros((N_KV * VT_ROWS, BLK), BF16)
        vt_ref[:, BLK + SEQ:] = jnp.zeros((N_KV * VT_ROWS, BLK), BF16)
        for c in range(SEQ // BLK):
            v_t = kv_ref[c * BLK:(c + 1) * BLK, KV_WIDTH:].T
            for kh in range(N_KV):
                vt_ref[kh * VT_ROWS:kh * VT_ROWS + HD, (c + 1) * BLK:(c + 2) * BLK] = (
                    v_t[kh * HD:(kh + 1) * HD])
        for kh in range(N_KV):
            vt_ref[kh * VT_ROWS + HD:(kh + 1) * VT_ROWS, :] = jnp.ones(
                (VT_ROWS - HD, SEQ + 2 * BLK), BF16)
        jj = lax.broadcasted_iota(jnp.int32, (BLK, 2 * BLK), 0)
        cc = lax.broadcasted_iota(jnp.int32, (BLK, 2 * BLK), 1)
        flip = jnp.where(jj + cc == BLK, 1.0, 0.0).astype(BF16)
        for blk in range(HALF_SEQ // BLK):
            src = SEQ - (blk + 1) * BLK
            if blk == 0:
                rev = _dot(flip[:, :BLK], f_ref[src:src + BLK, :])
            else:
                rev = _dot(flip, f_ref[src:src + 2 * BLK, :])
            frev_ref[blk * BLK:(blk + 1) * BLK, :] = rev.astype(BF16)
        first_row = lax.broadcasted_iota(jnp.int32, (V7X_BF16_SUBLANES, FG_DIM), 0) == 0
        for g in range(N_FGROUPS):
            lo = g * FG_DIM
            both = jnp.concatenate([f_ref[0:HALF_SEQ, lo:lo + FG_DIM],
                                    frev_ref[:, lo:lo + FG_DIM]], axis=1)
            r = _dot(both, dftc_ref[...])
            uw_ref[0:HALF_SEQ, lo:lo + FG_DIM] = r[:, :FG_DIM].astype(BF16)
            uw_ref[HALF_SEQ:, lo:lo + FG_DIM] = r[:, FG_DIM:].astype(BF16)
            mid = _dot(f_ref[HALF_SEQ:HALF_SEQ + V7X_BF16_SUBLANES, lo:lo + FG_DIM],
                       dftc_ref[0:FG_DIM, 0:FG_DIM])
            head = uw_ref[HALF_SEQ:HALF_SEQ + V7X_BF16_SUBLANES, lo:lo + FG_DIM]
            uw_ref[HALF_SEQ:HALF_SEQ + V7X_BF16_SUBLANES, lo:lo + FG_DIM] = jnp.where(
                first_row, mid, head.astype(F32)).astype(BF16)

    def block_keys(i):
        n = t * BLOCKS_PER_TILE + i
        variant = jnp.where(n == 0, 1, jnp.where(n == SEQ // BLK - 1, 2, 0))
        return pl.multiple_of(n * BLK, BLK), variant

    def scores(i):
        key0, variant = block_keys(i)
        q_t = q_ref[i * BLK:(i + 1) * BLK, :].T
        zero = jnp.zeros((HD, BLK), BF16)
        cols = []
        for h in range(N_HEADS):
            q_h = q_t[h * HD:(h + 1) * HD]
            cols.append(jnp.concatenate([q_h, zero] if h < GQ else [zero, q_h], axis=0))
        s_t = _dot(kpad_ref[pl.ds(key0, 3 * BLK), :], jnp.concatenate(cols, axis=1))
        tops = []
        for h in range(N_HEADS):
            s_h = s_t[:, h * BLK:(h + 1) * BLK]
            s_ref[i % 2, h] = s_h
            top = s_h[0:KEY_CHUNK] + bias_ref[variant, h, 0:KEY_CHUNK, :]
            for k0 in range(KEY_CHUNK, 3 * BLK, KEY_CHUNK):
                top = jnp.maximum(top, s_h[k0:k0 + KEY_CHUNK] + bias_ref[variant, h, k0:k0 + KEY_CHUNK, :])
            tops.append(jnp.max(top, axis=0, keepdims=True))
        return tops

    def softmax(i, tops):
        _, variant = block_keys(i)
        buf = i % 2
        sink_terms = []
        for h in range(N_HEADS):
            sink = sink_ref[h] * LOG2E
            m = jnp.maximum(tops[h], sink)
            for k0 in range(0, 3 * BLK, KEY_CHUNK):
                z = (s_ref[buf, h, k0:k0 + KEY_CHUNK, :] - m) + bias_ref[variant, h, k0:k0 + KEY_CHUNK, :]
                p_ref[buf, h, k0:k0 + KEY_CHUNK, :] = jnp.exp2(z).astype(BF16)
            sink_terms.append(jnp.exp2(sink - m))
        return sink_terms

    def weighted_values(i, sink_terms):
        key0, _ = block_keys(i)
        for kh in range(N_KV):
            p_group = jnp.concatenate([p_ref[i % 2, kh * GQ + g] for g in range(GQ)], axis=1)
            o_t = _dot(vt_ref[kh * VT_ROWS:(kh + 1) * VT_ROWS, pl.ds(key0, 3 * BLK)], p_group)
            den = o_t[HD:HD + 1] + jnp.concatenate(sink_terms[kh * GQ:(kh + 1) * GQ], axis=1)
            o_n = o_t[:HD] / den
            for pr in range(GQ // 2):
                pair_t = jnp.concatenate([o_n[:, 2 * pr * BLK:(2 * pr + 1) * BLK],
                                          o_n[:, (2 * pr + 1) * BLK:(2 * pr + 2) * BLK]], axis=0)
                lane0 = (kh * GQ + 2 * pr) * HD
                o_ref[i * BLK:(i + 1) * BLK, lane0:lane0 + 2 * HD] = pair_t.T.astype(BF16)

    half_tile = ROW_TILE // 2
    tops = [scores(0)] + [None] * (BLOCKS_PER_TILE - 1)
    for i in range(BLOCKS_PER_TILE):
        if i + 1 < BLOCKS_PER_TILE:
            tops[i + 1] = scores(i + 1)
        if i < 2:
            rows = slice(i * half_tile, (i + 1) * half_tile)
            freq0 = pl.multiple_of(t * ROW_TILE + i * half_tile, half_tile)
            fmix_ref[rows, :] = _dot(dfts_ref[pl.ds(freq0, half_tile), :], uw_ref[...]).astype(BF16)
        elif i == 2:
            yb_ref[...] = _dot(fmix_ref[...], wb_ref[...])
        weighted_values(i, softmax(i, tops[i]))

    y_a = _dot(o_ref[...], wa_ref[...])
    mix = (gate_ref[:, :D_MODEL].astype(F32) * y_a + gate_ref[:, D_MODEL:].astype(F32) * yb_ref[...])
    out_ref[...] = x1_ref[...] + _dot(mix.astype(BF16), wout_ref[...])


def _resident(shape):
    return pl.BlockSpec(shape, lambda *_: (0,) * len(shape), pipeline_mode=pl.Buffered(1))


def _rows(width, tile=ROW_TILE):
    return pl.BlockSpec((tile, width), lambda i: (i, 0))


def _slab_spec(shape):
    rows, cols = shape
    steps = TOKENS // ROW_TILE
    slab = next(r for r in range(V7X_BF16_SUBLANES, rows + 1, V7X_BF16_SUBLANES)
                if rows % r == 0 and r * steps >= rows)
    return pl.BlockSpec((slab, cols), lambda i: (jnp.minimum(i, rows // slab - 1), 0))


def _ffn_proj(x, g1, wup, wdn, gmix, win, bg, later_weights):
    assert len(later_weights) == N_LATER_WEIGHTS
    slabs = [_slab_spec(w.shape) for w in later_weights]
    hbm = pl.BlockSpec(memory_space=pl.ANY)
    return pl.pallas_call(
        _ffn_proj_kernel,
        grid=(TOKENS // ROW_TILE,),
        in_specs=[_rows(D_MODEL), _resident((1, D_MODEL)), hbm, hbm, _resident((1, D_MODEL)),
                  hbm, _resident((1, 2 * D_MODEL))] + slabs,
        out_specs=[_rows(D_MODEL), _rows(Q_WIDTH), _rows(2 * KV_WIDTH), _rows(F_WIDTH),
                   _rows(2 * D_MODEL)] + slabs,
        out_shape=[jax.ShapeDtypeStruct((TOKENS, D_MODEL), F32),
                   jax.ShapeDtypeStruct((TOKENS, Q_WIDTH), BF16),
                   jax.ShapeDtypeStruct((TOKENS, 2 * KV_WIDTH), BF16),
                   jax.ShapeDtypeStruct((TOKENS, F_WIDTH), BF16),
                   jax.ShapeDtypeStruct((TOKENS, 2 * D_MODEL), BF16)]
        + [jax.ShapeDtypeStruct(w.shape, BF16) for w in later_weights],
        scratch_shapes=[pltpu.VMEM((ROW_TILE, D_FF), BF16),
                        pltpu.VMEM((ROW_TILE, D_MODEL), BF16),
                        pltpu.VMEM((D_MODEL, 2 * D_FF), BF16),
                        pltpu.VMEM((D_FF, D_MODEL), BF16),
                        pltpu.VMEM((D_MODEL, IN_WIDTH), BF16)],
        compiler_params=pltpu.CompilerParams(dimension_semantics=("arbitrary",),
                                             vmem_limit_bytes=V7X_VMEM_LIMIT_BYTES),
        name="ffn_proj",
    )(x, g1, wup, wdn, gmix, win, bg, *later_weights)


def _ffn_final(x, g2, wup, wdn, gfin):
    return pl.pallas_call(
        _ffn_final_kernel,
        grid=(TOKENS // FINAL_ROW_TILE,),
        in_specs=[_rows(D_MODEL, FINAL_ROW_TILE), _resident((1, D_MODEL)),
                  _resident((D_MODEL, 2 * D_FF)), _resident((D_FF, D_MODEL)),
                  _resident((1, D_MODEL))],
        out_specs=_rows(D_MODEL, FINAL_ROW_TILE),
        out_shape=jax.ShapeDtypeStruct((TOKENS, D_MODEL), F32),
        scratch_shapes=[pltpu.VMEM((FINAL_ROW_TILE, D_FF), BF16)],
        compiler_params=pltpu.CompilerParams(dimension_semantics=("arbitrary",),
                                             vmem_limit_bytes=V7X_VMEM_LIMIT_BYTES),
        name="ffn_final",
    )(x, g2, wup, wdn, gfin)


def _mixer(rel_bias, sink, x1, q, kv, f, gates, bucket, dftc, dfts, wa, wb, wout):
    def tile_rows(width):
        return pl.BlockSpec((ROW_TILE, width), lambda b, t: (b * TILES_PER_SEQ + t, 0))

    def seq_rows(width):
        return pl.BlockSpec((SEQ, width), lambda b, t: (b, 0))

    smem = pl.BlockSpec(memory_space=pltpu.SMEM)
    return pl.pallas_call(
        _mixer_kernel,
        grid=(BATCH, TILES_PER_SEQ),
        in_specs=[smem, smem, tile_rows(D_MODEL), tile_rows(Q_WIDTH), seq_rows(2 * KV_WIDTH),
                  seq_rows(F_WIDTH), tile_rows(2 * D_MODEL), _resident((3 * BLK, BLK)),
                  _resident((2 * FG_DIM, 2 * FG_DIM)),
                  _resident((SEQ, SEQ)),
                  _resident((Q_WIDTH, D_MODEL)), _resident((F_WIDTH, D_MODEL)),
                  _resident((D_MODEL, D_MODEL))],
        out_specs=tile_rows(D_MODEL),
        out_shape=jax.ShapeDtypeStruct((TOKENS, D_MODEL), F32),
        scratch_shapes=[pltpu.VMEM((3, N_HEADS, 3 * BLK, BLK), F32),
                        pltpu.VMEM((SEQ + 2 * BLK, KV_WIDTH), BF16),
                        pltpu.VMEM((N_KV * VT_ROWS, SEQ + 2 * BLK), BF16),
                        pltpu.VMEM((HALF_SEQ, F_WIDTH), BF16),
                        pltpu.VMEM((SEQ, F_WIDTH), BF16),
                        pltpu.VMEM((ROW_TILE, Q_WIDTH), BF16),
                        pltpu.VMEM((2, N_HEADS, 3 * BLK, BLK), F32),
                        pltpu.VMEM((2, N_HEADS, 3 * BLK, BLK), BF16),
                        pltpu.VMEM((ROW_TILE, F_WIDTH), BF16),
                        pltpu.VMEM((ROW_TILE, D_MODEL), F32)],
        compiler_params=pltpu.CompilerParams(dimension_semantics=("arbitrary", "arbitrary"),
                                             vmem_limit_bytes=V7X_VMEM_LIMIT_BYTES),
        name="mixer",
    )(rel_bias, sink, x1, q, kv, f, gates, bucket, dftc, dfts, wa, wb, wout)


def kernel(x, g_ffn1, w_up1, w_down1, g_mix, w_in, b_gate, sink, rel_bias, w_branch_a, w_branch_b, w_out, g_ffn2, w_up2, w_down2, g_final):
    assert x.shape == (BATCH, SEQ, D_MODEL) and w_up1.shape[0] == DEPTH == 1
    chan, pos = _dft_tables()
    bucket = jnp.asarray(_t5_bucket_table().T)
    dftc = jnp.asarray(chan).astype(BF16)
    dfts = jnp.asarray(pos).astype(BF16)
    row = lambda v: v.reshape(1, -1)
    x0 = x.reshape(TOKENS, D_MODEL)
    x1, q, kv, f, gates, wa, wb, wout, wup2, wdn2 = _ffn_proj(
        x0, row(g_ffn1[0]), w_up1[0], w_down1[0], row(g_mix[0]), w_in[0], row(b_gate[0]),
        [w_branch_a[0], w_branch_b[0], w_out[0], w_up2[0], w_down2[0]])
    x2 = _mixer(rel_bias, sink[0], x1, q, kv, f, gates, bucket, dftc, dfts, wa, wb, wout)
    out = _ffn_final(x2, row(g_ffn2[0]), wup2, wdn2, row(g_final))
    return out.reshape(BATCH, SEQ, D_MODEL)
```

```python
import functools

import numpy as np
import jax
import jax.numpy as jnp
from jax import lax
from jax.experimental import pallas as pl
from jax.experimental.pallas import tpu as pltpu

D_MODEL = 1024
BATCH = 8
SEQ = 2048
DEPTH = 1
N_HEADS = 8
N_KV = 2
GQ = N_HEADS // N_KV
HD = 64
Q_WIDTH = N_HEADS * HD
KV_WIDTH = N_KV * HD
WINDOW = 128
BLK = 128
N_FGROUPS = 4
FG_DIM = 128
F_WIDTH = N_FGROUPS * FG_DIM
N_BUCKETS = 32
MAX_DIST = 128
D_FF = 2816
EPS = 1e-6
O_K = Q_WIDTH
O_V = O_K + KV_WIDTH
O_F = O_V + KV_WIDTH
O_G = O_F + F_WIDTH
IN_WIDTH = O_G + 2 * D_MODEL

F32 = jnp.float32
BF16 = jnp.bfloat16

V7X_MXU_DIM = 256
V7X_BF16_SUBLANES = 16
V7X_VMEM_LIMIT_BYTES = 56 * 1024 * 1024

TOKENS = BATCH * SEQ
HALF_SEQ = SEQ // 2
ROW_TILE = 512
FINAL_ROW_TILE = 1024
TILES_PER_SEQ = SEQ // ROW_TILE
BLOCKS_PER_TILE = ROW_TILE // BLK
FF_CHUNK = V7X_MXU_DIM
FFN_SUB_ROWS = 256
VT_ROWS = HD + V7X_BF16_SUBLANES
N_LATER_WEIGHTS = 5
STAGE_SLOTS = 4
STAGE_BYTES = 3 * 512 * 1024
KEY_CHUNK = 64
LOG2E = float(np.log2(np.e))


def _t5_bucket_table():
    rel = (np.arange(3 * BLK)[None, :] - BLK) - np.arange(BLK)[:, None]
    half = N_BUCKETS // 2
    max_exact = half // 2
    ret = (rel > 0).astype(np.int32) * half
    n = np.abs(rel)
    n_safe = np.maximum(n, 1).astype(np.float32)
    large = max_exact + (np.log(n_safe / max_exact) / np.log(MAX_DIST / max_exact)
                         * (half - max_exact)).astype(np.int32)
    large = np.minimum(large, half - 1)
    return (ret + np.where(n < max_exact, n, large)).astype(np.int32)


@functools.lru_cache(maxsize=None)
def _dft_tables():
    def cos_sin(n):
        idx = np.arange(n)
        ang = 2.0 * np.pi * ((idx[:, None] * idx[None, :]) % n).astype(np.float64) / n
        return np.cos(ang) / np.sqrt(n), np.sin(ang) / np.sqrt(n)
    cc, sc = cos_sin(FG_DIM)
    cs, ss = cos_sin(SEQ)
    chan = np.block([[cc, sc], [cc, -sc]]).astype(np.float32)
    neg_sin = -ss[:, :HALF_SEQ]
    neg_sin[:, 0] = np.where(np.arange(SEQ) % 2 == 0, 1.0, -1.0) / np.sqrt(SEQ)
    pos = np.concatenate([cs[:, :HALF_SEQ], neg_sin], axis=1).astype(np.float32)
    return chan, pos


def _rms(x, g):
    return x * lax.rsqrt(jnp.mean(x * x, axis=-1, keepdims=True) + EPS) * g


def _sigmoid(z):
    return 1.0 / (1.0 + jnp.exp2(z * -LOG2E))


def _dot(a, b):
    return jnp.dot(a, b, preferred_element_type=F32)


def _skewed_ffn(x_ref, g_ref, wup_ref, wdn_ref, act_ref, finish, finish_late=None):
    n_sub = x_ref.shape[0] // FFN_SUB_ROWS
    rows = [slice(k * FFN_SUB_ROWS, (k + 1) * FFN_SUB_ROWS) for k in range(n_sub)]
    normed = [None] * n_sub

    def prologue(k):
        normed[k] = _rms(x_ref[rows[k], :], g_ref[...]).astype(BF16)

    def up_chunk(k, c):
        lo = c * FF_CHUNK
        gate = _dot(normed[k], wup_ref[:, lo:lo + FF_CHUNK])
        up = _dot(normed[k], wup_ref[:, D_FF + lo:D_FF + lo + FF_CHUNK])
        act_ref[rows[k], lo:lo + FF_CHUNK] = (gate * _sigmoid(gate) * up).astype(BF16)

    def epilogue(k):
        finish(k, rows[k], x_ref[rows[k], :] + 0.5 * _dot(act_ref[rows[k], :], wdn_ref[...]))

    prologue(0)
    for k in range(n_sub):
        for c in range(D_FF // FF_CHUNK):
            up_chunk(k, c)
            if c == 0 and k + 1 < n_sub:
                prologue(k + 1)
            if c == 1 and k > 0:
                epilogue(k - 1)
            if c == 3 and 0 < k < n_sub - 1 and finish_late is not None:
                finish_late(k - 1, rows[k - 1])
    epilogue(n_sub - 1)
    if finish_late is not None:
        for k in range(max(n_sub - 2, 0), n_sub):
            finish_late(k, rows[k])


def _stage_weight(src_hbm, dst_ref):
    rows, cols = dst_ref.shape
    slab = max(r for r in range(V7X_BF16_SUBLANES, rows + 1, V7X_BF16_SUBLANES)
               if rows % r == 0 and r * cols * 4 <= STAGE_BYTES)
    n_slabs = rows // slab

    def staged(stage_ref, sem):
        def slab_copy(c, slot):
            return pltpu.make_async_copy(src_hbm.at[pl.ds(c * slab, slab), :],
                                         stage_ref.at[slot], sem.at[slot])

        for c in range(min(STAGE_SLOTS - 1, n_slabs)):
            slab_copy(c, c).start()

        def step(c, carry):
            ahead = c + STAGE_SLOTS - 1

            @pl.when(ahead < n_slabs)
            def _start_ahead():
                slab_copy(ahead, ahead % STAGE_SLOTS).start()

            slot = c % STAGE_SLOTS
            slab_copy(c, slot).wait()
            dst_ref[pl.ds(pl.multiple_of(c * slab, slab), slab), :] = stage_ref[slot].astype(BF16)
            return carry

        lax.fori_loop(0, n_slabs, step, 0)

    pl.run_scoped(staged, pltpu.VMEM((STAGE_SLOTS, slab, cols), F32),
                  pltpu.SemaphoreType.DMA((STAGE_SLOTS,)))


def _ffn_proj_kernel(x_ref, g1_ref, wup_hbm, wdn_hbm, gmix_ref, win_hbm, bg_ref, *refs):
    later_f32 = refs[:N_LATER_WEIGHTS]
    x1_ref, q_ref, kv_ref, f_ref, gate_ref = refs[N_LATER_WEIGHTS:N_LATER_WEIGHTS + 5]
    later_bf16 = refs[N_LATER_WEIGHTS + 5:2 * N_LATER_WEIGHTS + 5]
    act_ref, h2_ref, wup_ref, wdn_ref, win_ref = refs[2 * N_LATER_WEIGHTS + 5:]

    @pl.when(pl.program_id(0) == 0)
    def _stage_own_weights():
        _stage_weight(wup_hbm, wup_ref)
        _stage_weight(wdn_hbm, wdn_ref)
        _stage_weight(win_hbm, win_ref)

    for src, dst in zip(later_f32, later_bf16):
        dst[...] = src[...].astype(BF16)

    def second_norm(k, rows, x1):
        x1_ref[rows, :] = x1
        h2_ref[rows, :] = _rms(x1, gmix_ref[...]).astype(BF16)

    def project(k, rows):
        h2 = h2_ref[rows, :]
        for c in range(2):
            lo = c * D_MODEL
            z = _dot(h2, win_ref[:, O_G + lo:O_G + lo + D_MODEL]) + bg_ref[:, lo:lo + D_MODEL]
            gate_ref[rows, lo:lo + D_MODEL] = _sigmoid(z).astype(BF16)
        q_ref[rows, :] = (_dot(h2, win_ref[:, 0:O_K]) * (HD ** -0.5 * LOG2E)).astype(BF16)
        kv_ref[rows, :] = _dot(h2, win_ref[:, O_K:O_F]).astype(BF16)
        f_ref[rows, :] = _dot(h2, win_ref[:, O_F:O_G]).astype(BF16)

    _skewed_ffn(x_ref, g1_ref, wup_ref, wdn_ref, act_ref, second_norm, finish_late=project)


def _ffn_final_kernel(x_ref, g2_ref, wup_ref, wdn_ref, gfin_ref, out_ref, act_ref):
    def final_norm(k, rows, x3):
        out_ref[rows, :] = _rms(x3, gfin_ref[...])

    _skewed_ffn(x_ref, g2_ref, wup_ref, wdn_ref, act_ref, final_norm)


def _mixer_kernel(relb_ref, sink_ref, x1_ref, q_ref, kv_ref, f_ref, gate_ref, bucket_ref,
                  dftc_ref, dfts_ref, wa_ref, wb_ref, wout_ref, out_ref,
                  bias_ref, kpad_ref, vt_ref, frev_ref, uw_ref, o_ref, s_ref, p_ref,
                  fmix_ref, yb_ref):
    b = pl.program_id(0)
    t = pl.program_id(1)

    @pl.when((b == 0) & (t == 0))
    def _build_bias():
        bucket = bucket_ref[...]
        kj = lax.broadcasted_iota(jnp.int32, (3 * BLK, BLK), 0)
        qi = lax.broadcasted_iota(jnp.int32, (3 * BLK, BLK), 1)
        in_window = jnp.abs(kj - BLK - qi) <= WINDOW
        valid = (in_window, in_window & (kj >= BLK), in_window & (kj < 2 * BLK))
        for h in range(N_HEADS):
            def pick(bk, acc, h=h):
                return jnp.where(bucket == bk, relb_ref[bk, h] * LOG2E, acc)
            base = lax.fori_loop(0, N_BUCKETS, pick, jnp.zeros((3 * BLK, BLK), F32))
            for v in range(3):
                bias_ref[v, h] = jnp.where(valid[v], base, -jnp.inf)

    @pl.when(t == 0)
    def _per_sequence():
        kpad_ref[0:BLK, :] = jnp.zeros((BLK, KV_WIDTH), BF16)
        kpad_ref[BLK:BLK + SEQ, :] = kv_ref[:, :KV_WIDTH]
        kpad_ref[BLK + SEQ:, :] = jnp.zeros((BLK, KV_WIDTH), BF16)
        vt_ref[:, 0:BLK] = jnp.zeros((N_KV * VT_ROWS, BLK), BF16)
        vt_ref[:, BLK + SEQ:] = jnp.zeros((N_KV * VT_ROWS, BLK), BF16)
        for c in range(SEQ // BLK):
            v_t = kv_ref[c * BLK:(c + 1) * BLK, KV_WIDTH:].T
            for kh in range(N_KV):
                vt_ref[kh * VT_ROWS:kh * VT_ROWS + HD, (c + 1) * BLK:(c + 2) * BLK] = (
                    v_t[kh * HD:(kh + 1) * HD])
        for kh in range(N_KV):
            vt_ref[kh * VT_ROWS + HD:(kh + 1) * VT_ROWS, :] = jnp.ones(
                (VT_ROWS - HD, SEQ + 2 * BLK), BF16)
        jj = lax.broadcasted_iota(jnp.int32, (BLK, 2 * BLK), 0)
        cc = lax.broadcasted_iota(jnp.int32, (BLK, 2 * BLK), 1)
        flip = jnp.where(jj + cc == BLK, 1.0, 0.0).astype(BF16)
        for blk in range(HALF_SEQ // BLK):
            src = SEQ - (blk + 1) * BLK
            if blk == 0:
                rev = _dot(flip[:, :BLK], f_ref[src:src + BLK, :])
            else:
                rev = _dot(flip, f_ref[src:src + 2 * BLK, :])
            frev_ref[blk * BLK:(blk + 1) * BLK, :] = rev.astype(BF16)
        first_row = lax.broadcasted_iota(jnp.int32, (V7X_BF16_SUBLANES, FG_DIM), 0) == 0
        for g in range(N_FGROUPS):
            lo = g * FG_DIM
            both = jnp.concatenate([f_ref[0:HALF_SEQ, lo:lo + FG_DIM],
                                    frev_ref[:, lo:lo + FG_DIM]], axis=1)
            r = _dot(both, dftc_ref[...])
            uw_ref[0:HALF_SEQ, lo:lo + FG_DIM] = r[:, :FG_DIM].astype(BF16)
            uw_ref[HALF_SEQ:, lo:lo + FG_DIM] = r[:, FG_DIM:].astype(BF16)
            mid = _dot(f_ref[HALF_SEQ:HALF_SEQ + V7X_BF16_SUBLANES, lo:lo + FG_DIM],
                       dftc_ref[0:FG_DIM, 0:FG_DIM])
            head = uw_ref[HALF_SEQ:HALF_SEQ + V7X_BF16_SUBLANES, lo:lo + FG_DIM]
            uw_ref[HALF_SEQ:HALF_SEQ + V7X_BF16_SUBLANES, lo:lo + FG_DIM] = jnp.where(
                first_row, mid, head.astype(F32)).astype(BF16)

    def block_keys(i):
        n = t * BLOCKS_PER_TILE + i
        variant = jnp.where(n == 0, 1, jnp.where(n == SEQ // BLK - 1, 2, 0))
        return pl.multiple_of(n * BLK, BLK), variant

    def scores(i):
        key0, variant = block_keys(i)
        q_t = q_ref[i * BLK:(i + 1) * BLK, :].T
        zero = jnp.zeros((HD, BLK), BF16)
        cols = []
        for h in range(N_HEADS):
            q_h = q_t[h * HD:(h + 1) * HD]
            cols.append(jnp.concatenate([q_h, zero] if h < GQ else [zero, q_h], axis=0))
        s_t = _dot(kpad_ref[pl.ds(key0, 3 * BLK), :], jnp.concatenate(cols, axis=1))
        tops = []
        for h in range(N_HEADS):
            s_h = s_t[:, h * BLK:(h + 1) * BLK]
            s_ref[i % 2, h] = s_h
            top = s_h[0:KEY_CHUNK] + bias_ref[variant, h, 0:KEY_CHUNK, :]
            for k0 in range(KEY_CHUNK, 3 * BLK, KEY_CHUNK):
                top = jnp.maximum(top, s_h[k0:k0 + KEY_CHUNK] + bias_ref[variant, h, k0:k0 + KEY_CHUNK, :])
            tops.append(jnp.max(top, axis=0, keepdims=True))
        return tops

    def softmax(i, tops):
        _, variant = block_keys(i)
        buf = i % 2
        sink_terms = []
        for h in range(N_HEADS):
            sink = sink_ref[h] * LOG2E
            m = jnp.maximum(tops[h], sink)
            for k0 in range(0, 3 * BLK, KEY_CHUNK):
                z = (s_ref[buf, h, k0:k0 + KEY_CHUNK, :] - m) + bias_ref[variant, h, k0:k0 + KEY_CHUNK, :]
                p_ref[buf, h, k0:k0 + KEY_CHUNK, :] = jnp.exp2(z).astype(BF16)
            sink_terms.append(jnp.exp2(sink - m))
        return sink_terms

    def weighted_values(i, sink_terms):
        key0, _ = block_keys(i)
        for kh in range(N_KV):
            p_group = jnp.concatenate([p_ref[i % 2, kh * GQ + g] for g in range(GQ)], axis=1)
            o_t = _dot(vt_ref[kh * VT_ROWS:(kh + 1) * VT_ROWS, pl.ds(key0, 3 * BLK)], p_group)
            den = o_t[HD:HD + 1] + jnp.concatenate(sink_terms[kh * GQ:(kh + 1) * GQ], axis=1)
            o_n = o_t[:HD] / den
            for pr in range(GQ // 2):
                pair_t = jnp.concatenate([o_n[:, 2 * pr * BLK:(2 * pr + 1) * BLK],
                                          o_n[:, (2 * pr + 1) * BLK:(2 * pr + 2) * BLK]], axis=0)
                lane0 = (kh * GQ + 2 * pr) * HD
                o_ref[i * BLK:(i + 1) * BLK, lane0:lane0 + 2 * HD] = pair_t.T.astype(BF16)

    half_tile = ROW_TILE // 2
    tops = [scores(0)] + [None] * (BLOCKS_PER_TILE - 1)
    for i in range(BLOCKS_PER_TILE):
        if i + 1 < BLOCKS_PER_TILE:
            tops[i + 1] = scores(i + 1)
        if i < 2:
            rows = slice(i * half_tile, (i + 1) * half_tile)
            freq0 = pl.multiple_of(t * ROW_TILE + i * half_tile, half_tile)
            fmix_ref[rows, :] = _dot(dfts_ref[pl.ds(freq0, half_tile), :], uw_ref[...]).astype(BF16)
        elif i == 2:
            yb_ref[...] = _dot(fmix_ref[...], wb_ref[...])
        weighted_values(i, softmax(i, tops[i]))

    y_a = _dot(o_ref[...], wa_ref[...])
    mix = (gate_ref[:, :D_MODEL].astype(F32) * y_a + gate_ref[:, D_MODEL:].astype(F32) * yb_ref[...])
    out_ref[...] = x1_ref[...] + _dot(mix.astype(BF16), wout_ref[...])


def _resident(shape):
    return pl.BlockSpec(shape, lambda *_: (0,) * len(shape), pipeline_mode=pl.Buffered(1))


def _rows(width, tile=ROW_TILE):
    return pl.BlockSpec((tile, width), lambda i: (i, 0))


def _slab_spec(shape):
    rows, cols = shape
    steps = TOKENS // ROW_TILE
    slab = next(r for r in range(V7X_BF16_SUBLANES, rows + 1, V7X_BF16_SUBLANES)
                if rows % r == 0 and r * steps >= rows)
    return pl.BlockSpec((slab, cols), lambda i: (jnp.minimum(i, rows // slab - 1), 0))


def _ffn_proj(x, g1, wup, wdn, gmix, win, bg, later_weights):
    assert len(later_weights) == N_LATER_WEIGHTS
    slabs = [_slab_spec(w.shape) for w in later_weights]
    hbm = pl.BlockSpec(memory_space=pl.ANY)
    return pl.pallas_call(
        _ffn_proj_kernel,
        grid=(TOKENS // ROW_TILE,),
        in_specs=[_rows(D_MODEL), _resident((1, D_MODEL)), hbm, hbm, _resident((1, D_MODEL)),
                  hbm, _resident((1, 2 * D_MODEL))] + slabs,
        out_specs=[_rows(D_MODEL), _rows(Q_WIDTH), _rows(2 * KV_WIDTH), _rows(F_WIDTH),
                   _rows(2 * D_MODEL)] + slabs,
        out_shape=[jax.ShapeDtypeStruct((TOKENS, D_MODEL), F32),
                   jax.ShapeDtypeStruct((TOKENS, Q_WIDTH), BF16),
                   jax.ShapeDtypeStruct((TOKENS, 2 * KV_WIDTH), BF16),
                   jax.ShapeDtypeStruct((TOKENS, F_WIDTH), BF16),
                   jax.ShapeDtypeStruct((TOKENS, 2 * D_MODEL), BF16)]
        + [jax.ShapeDtypeStruct(w.shape, BF16) for w in later_weights],
        scratch_shapes=[pltpu.VMEM((ROW_TILE, D_FF), BF16),
                        pltpu.VMEM((ROW_TILE, D_MODEL), BF16),
                        pltpu.VMEM((D_MODEL, 2 * D_FF), BF16),
                        pltpu.VMEM((D_FF, D_MODEL), BF16),
                        pltpu.VMEM((D_MODEL, IN_WIDTH), BF16)],
        compiler_params=pltpu.CompilerParams(dimension_semantics=("arbitrary",),
                                             vmem_limit_bytes=V7X_VMEM_LIMIT_BYTES),
        name="ffn_proj",
    )(x, g1, wup, wdn, gmix, win, bg, *later_weights)


def _ffn_final(x, g2, wup, wdn, gfin):
    return pl.pallas_call(
        _ffn_final_kernel,
        grid=(TOKENS // FINAL_ROW_TILE,),
        in_specs=[_rows(D_MODEL, FINAL_ROW_TILE), _resident((1, D_MODEL)),
                  _resident((D_MODEL, 2 * D_FF)), _resident((D_FF, D_MODEL)),
                  _resident((1, D_MODEL))],
        out_specs=_rows(D_MODEL, FINAL_ROW_TILE),
        out_shape=jax.ShapeDtypeStruct((TOKENS, D_MODEL), F32),
        scratch_shapes=[pltpu.VMEM((FINAL_ROW_TILE, D_FF), BF16)],
        compiler_params=pltpu.CompilerParams(dimension_semantics=("arbitrary",),
                                             vmem_limit_bytes=V7X_VMEM_LIMIT_BYTES),
        name="ffn_final",
    )(x, g2, wup, wdn, gfin)


def _mixer(rel_bias, sink, x1, q, kv, f, gates, bucket, dftc, dfts, wa, wb, wout):
    def tile_rows(width):
        return pl.BlockSpec((ROW_TILE, width), lambda b, t: (b * TILES_PER_SEQ + t, 0))

    def seq_rows(width):
        return pl.BlockSpec((SEQ, width), lambda b, t: (b, 0))

    smem = pl.BlockSpec(memory_space=pltpu.SMEM)
    return pl.pallas_call(
        _mixer_kernel,
        grid=(BATCH, TILES_PER_SEQ),
        in_specs=[smem, smem, tile_rows(D_MODEL), tile_rows(Q_WIDTH), seq_rows(2 * KV_WIDTH),
                  seq_rows(F_WIDTH), tile_rows(2 * D_MODEL), _resident((3 * BLK, BLK)),
                  _resident((2 * FG_DIM, 2 * FG_DIM)),
                  _resident((SEQ, SEQ)),
                  _resident((Q_WIDTH, D_MODEL)), _resident((F_WIDTH, D_MODEL)),
                  _resident((D_MODEL, D_MODEL))],
        out_specs=tile_rows(D_MODEL),
        out_shape=jax.ShapeDtypeStruct((TOKENS, D_MODEL), F32),
        scratch_shapes=[pltpu.VMEM((3, N_HEADS, 3 * BLK, BLK), F32),
                        pltpu.VMEM((SEQ + 2 * BLK, KV_WIDTH), BF16),
                        pltpu.VMEM((N_KV * VT_ROWS, SEQ + 2 * BLK), BF16),
                        pltpu.VMEM((HALF_SEQ, F_WIDTH), BF16),
                        pltpu.VMEM((SEQ, F_WIDTH), BF16),
                        pltpu.VMEM((ROW_TILE, Q_WIDTH), BF16),
                        pltpu.VMEM((2, N_HEADS, 3 * BLK, BLK), F32),
                        pltpu.VMEM((2, N_HEADS, 3 * BLK, BLK), BF16),
                        pltpu.VMEM((ROW_TILE, F_WIDTH), BF16),
                        pltpu.VMEM((ROW_TILE, D_MODEL), F32)],
        compiler_params=pltpu.CompilerParams(dimension_semantics=("arbitrary", "arbitrary"),
                                             vmem_limit_bytes=V7X_VMEM_LIMIT_BYTES),
        name="mixer",
    )(rel_bias, sink, x1, q, kv, f, gates, bucket, dftc, dfts, wa, wb, wout)


def kernel(x, g_ffn1, w_up1, w_down1, g_mix, w_in, b_gate, sink, rel_bias, w_branch_a, w_branch_b, w_out, g_ffn2, w_up2, w_down2, g_final):
    assert x.shape == (BATCH, SEQ, D_MODEL) and w_up1.shape[0] == DEPTH == 1
    chan, pos = _dft_tables()
    bucket = jnp.asarray(_t5_bucket_table().T)
    dftc = jnp.asarray(chan).astype(BF16)
    dfts = jnp.asarray(pos).astype(BF16)
    row = lambda v: v.reshape(1, -1)
    x0 = x.reshape(TOKENS, D_MODEL)
    x1, q, kv, f, gates, wa, wb, wout, wup2, wdn2 = _ffn_proj(
        x0, row(g_ffn1[0]), w_up1[0], w_down1[0], row(g_mix[0]), w_in[0], row(b_gate[0]),
        [w_branch_a[0], w_branch_b[0], w_out[0], w_up2[0], w_down2[0]])
    x2 = _mixer(rel_bias, sink[0], x1, q, kv, f, gates, bucket, dftc, dfts, wa, wb, wout)
    out = _ffn_final(x2, row(g_ffn2[0]), wup2, wdn2, row(g_final))
    return out.reshape(BATCH, SEQ, D_MODEL)
```

```python
import functools

import numpy as np
import jax
import jax.numpy as jnp
from jax import lax
from jax.experimental import pallas as pl
from jax.experimental.pallas import tpu as pltpu

D_MODEL = 1024
BATCH = 8
SEQ = 2048
DEPTH = 1
N_HEADS = 8
N_KV = 2
GQ = N_HEADS // N_KV
HD = 64
Q_WIDTH = N_HEADS * HD
KV_WIDTH = N_KV * HD
WINDOW = 128
BLK = 128
N_FGROUPS = 4
FG_DIM = 128
F_WIDTH = N_FGROUPS * FG_DIM
N_BUCKETS = 32
MAX_DIST = 128
D_FF = 2816
EPS = 1e-6
O_K = Q_WIDTH
O_V = O_K + KV_WIDTH
O_F = O_V + KV_WIDTH
O_G = O_F + F_WIDTH
IN_WIDTH = O_G + 2 * D_MODEL

F32 = jnp.float32
BF16 = jnp.bfloat16

V7X_MXU_DIM = 256
V7X_BF16_SUBLANES = 16
V7X_VMEM_LIMIT_BYTES = 56 * 1024 * 1024

TOKENS = BATCH * SEQ
HALF_SEQ = SEQ // 2
ROW_TILE = 512
FINAL_ROW_TILE = 1024
TILES_PER_SEQ = SEQ // ROW_TILE
BLOCKS_PER_TILE = ROW_TILE // BLK
FF_CHUNK = V7X_MXU_DIM
FFN_SUB_ROWS = 256
VT_ROWS = HD + V7X_BF16_SUBLANES
N_LATER_WEIGHTS = 5
STAGE_SLOTS = 4
STAGE_BYTES = 3 * 512 * 1024
KEY_CHUNK = 64
LOG2E = float(np.log2(np.e))


def _t5_bucket_table():
    rel = (np.arange(3 * BLK)[None, :] - BLK) - np.arange(BLK)[:, None]
    half = N_BUCKETS // 2
    max_exact = half // 2
    ret = (rel > 0).astype(np.int32) * half
    n = np.abs(rel)
    n_safe = np.maximum(n, 1).astype(np.float32)
    large = max_exact + (np.log(n_safe / max_exact) / np.log(MAX_DIST / max_exact)
                         * (half - max_exact)).astype(np.int32)
    large = np.minimum(large, half - 1)
    return (ret + np.where(n < max_exact, n, large)).astype(np.int32)


@functools.lru_cache(maxsize=None)
def _dft_tables():
    def cos_sin(n):
        idx = np.arange(n)
        ang = 2.0 * np.pi * ((idx[:, None] * idx[None, :]) % n).astype(np.float64) / n
        return np.cos(ang) / np.sqrt(n), np.sin(ang) / np.sqrt(n)
    cc, sc = cos_sin(FG_DIM)
    cs, ss = cos_sin(SEQ)
    chan = np.block([[cc, sc], [cc, -sc]]).astype(np.float32)
    neg_sin = -ss[:, :HALF_SEQ]
    neg_sin[:, 0] = np.where(np.arange(SEQ) % 2 == 0, 1.0, -1.0) / np.sqrt(SEQ)
    pos = np.concatenate([cs[:, :HALF_SEQ], neg_sin], axis=1).astype(np.float32)
    return chan, pos


def _rms(x, g):
    return x * lax.rsqrt(jnp.mean(x * x, axis=-1, keepdims=True) + EPS) * g


def _sigmoid(z):
    return 1.0 / (1.0 + jnp.exp2(z * -LOG2E))


def _dot(a, b):
    return jnp.dot(a, b, preferred_element_type=F32)


def _skewed_ffn(x_ref, g_ref, wup_ref, wdn_ref, act_ref, finish, finish_late=None):
    n_sub = x_ref.shape[0] // FFN_SUB_ROWS
    rows = [slice(k * FFN_SUB_ROWS, (k + 1) * FFN_SUB_ROWS) for k in range(n_sub)]
    normed = [None] * n_sub

    def prologue(k):
        normed[k] = _rms(x_ref[rows[k], :], g_ref[...]).astype(BF16)

    def up_chunk(k, c):
        lo = c * FF_CHUNK
        gate = _dot(normed[k], wup_ref[:, lo:lo + FF_CHUNK])
        up = _dot(normed[k], wup_ref[:, D_FF + lo:D_FF + lo + FF_CHUNK])
        act_ref[rows[k], lo:lo + FF_CHUNK] = (gate * _sigmoid(gate) * up).astype(BF16)

    def epilogue(k):
        finish(k, rows[k], x_ref[rows[k], :] + 0.5 * _dot(act_ref[rows[k], :], wdn_ref[...]))

    prologue(0)
    for k in range(n_sub):
        for c in range(D_FF // FF_CHUNK):
            up_chunk(k, c)
            if c == 0 and k + 1 < n_sub:
                prologue(k + 1)
            if c == 1 and k > 0:
                epilogue(k - 1)
            if c == 3 and 0 < k < n_sub - 1 and finish_late is not None:
                finish_late(k - 1, rows[k - 1])
    epilogue(n_sub - 1)
    if finish_late is not None:
        for k in range(max(n_sub - 2, 0), n_sub):
            finish_late(k, rows[k])


def _stage_weight(src_hbm, dst_ref):
    rows, cols = dst_ref.shape
    slab = max(r for r in range(V7X_BF16_SUBLANES, rows + 1, V7X_BF16_SUBLANES)
               if rows % r == 0 and r * cols * 4 <= STAGE_BYTES)
    n_slabs = rows // slab

    def staged(stage_ref, sem):
        def slab_copy(c, slot):
            return pltpu.make_async_copy(src_hbm.at[pl.ds(c * slab, slab), :],
                                         stage_ref.at[slot], sem.at[slot])

        for c in range(min(STAGE_SLOTS - 1, n_slabs)):
            slab_copy(c, c).start()

        def step(c, carry):
            ahead = c + STAGE_SLOTS - 1

            @pl.when(ahead < n_slabs)
            def _start_ahead():
                slab_copy(ahead, ahead % STAGE_SLOTS).start()

            slot = c % STAGE_SLOTS
            slab_copy(c, slot).wait()
            dst_ref[pl.ds(pl.multiple_of(c * slab, slab), slab), :] = stage_ref[slot].astype(BF16)
            return carry

        lax.fori_loop(0, n_slabs, step, 0)

    pl.run_scoped(staged, pltpu.VMEM((STAGE_SLOTS, slab, cols), F32),
                  pltpu.SemaphoreType.DMA((STAGE_SLOTS,)))


def _ffn_proj_kernel(x_ref, g1_ref, wup_hbm, wdn_hbm, gmix_ref, win_hbm, bg_ref, *refs):
    later_f32 = refs[:N_LATER_WEIGHTS]
    x1_ref, q_ref, kv_ref, f_ref, gate_ref = refs[N_LATER_WEIGHTS:N_LATER_WEIGHTS + 5]
    later_bf16 = refs[N_LATER_WEIGHTS + 5:2 * N_LATER_WEIGHTS + 5]
    act_ref, h2_ref, wup_ref, wdn_ref, win_ref = refs[2 * N_LATER_WEIGHTS + 5:]

    @pl.when(pl.program_id(0) == 0)
    def _stage_own_weights():
        _stage_weight(wup_hbm, wup_ref)
        _stage_weight(wdn_hbm, wdn_ref)
        _stage_weight(win_hbm, win_ref)

    for src, dst in zip(later_f32, later_bf16):
        dst[...] = src[...].astype(BF16)

    def second_norm(k, rows, x1):
        x1_ref[rows, :] = x1
        h2_ref[rows, :] = _rms(x1, gmix_ref[...]).astype(BF16)

    def project(k, rows):
        h2 = h2_ref[rows, :]
        for c in range(2):
            lo = c * D_MODEL
            z = _dot(h2, win_ref[:, O_G + lo:O_G + lo + D_MODEL]) + bg_ref[:, lo:lo + D_MODEL]
            gate_ref[rows, lo:lo + D_MODEL] = _sigmoid(z).astype(BF16)
        q_ref[rows, :] = (_dot(h2, win_ref[:, 0:O_K]) * (HD ** -0.5 * LOG2E)).astype(BF16)
        kv_ref[rows, :] = _dot(h2, win_ref[:, O_K:O_F]).astype(BF16)
        f_ref[rows, :] = _dot(h2, win_ref[:, O_F:O_G]).astype(BF16)

    _skewed_ffn(x_ref, g1_ref, wup_ref, wdn_ref, act_ref, second_norm, finish_late=project)


def _ffn_final_kernel(x_ref, g2_ref, wup_ref, wdn_ref, gfin_ref, out_ref, act_ref):
    def final_norm(k, rows, x3):
        out_ref[rows, :] = _rms(x3, gfin_ref[...])

    _skewed_ffn(x_ref, g2_ref, wup_ref, wdn_ref, act_ref, final_norm)


def _mixer_kernel(relb_ref, sink_ref, x1_ref, q_ref, kv_ref, f_ref, gate_ref, bucket_ref,
                  dftc_ref, dfts_ref, wa_ref, wb_ref, wout_ref, out_ref,
                  bias_ref, kpad_ref, vt_ref, frev_ref, uw_ref, o_ref, s_ref, p_ref,
                  fmix_ref, yb_ref):
    b = pl.program_id(0)
    t = pl.program_id(1)

    @pl.when((b == 0) & (t == 0))
    def _build_bias():
        bucket = bucket_ref[...]
        kj = lax.broadcasted_iota(jnp.int32, (3 * BLK, BLK), 0)
        qi = lax.broadcasted_iota(jnp.int32, (3 * BLK, BLK), 1)
        in_window = jnp.abs(kj - BLK - qi) <= WINDOW
        valid = (in_window, in_window & (kj >= BLK), in_window & (kj < 2 * BLK))
        for h in range(N_HEADS):
            def pick(bk, acc, h=h):
                return jnp.where(bucket == bk, relb_ref[bk, h] * LOG2E, acc)
            base = lax.fori_loop(0, N_BUCKETS, pick, jnp.zeros((3 * BLK, BLK), F32))
            for v in range(3):
                bias_ref[v, h] = jnp.where(valid[v], base, -jnp.inf)

    @pl.when(t == 0)
    def _per_sequence():
        kpad_ref[0:BLK, :] = jnp.zeros((BLK, KV_WIDTH), BF16)
        kpad_ref[BLK:BLK + SEQ, :] = kv_ref[:, :KV_WIDTH]
        kpad_ref[BLK + SEQ:, :] = jnp.zeros((BLK, KV_WIDTH), BF16)
        vt_ref[:, 0:BLK] = jnp.zeros((N_KV * VT_ROWS, BLK), BF16)
        vt_ref[:, BLK + SEQ:] = jnp.zeros((N_KV * VT_ROWS, BLK), BF16)
        for c in range(SEQ // BLK):
            v_t = kv_ref[c * BLK:(c + 1) * BLK, KV_WIDTH:].T
            for kh in range(N_KV):
                vt_ref[kh * VT_ROWS:kh * VT_ROWS + HD, (c + 1) * BLK:(c + 2) * BLK] = (
                    v_t[kh * HD:(kh + 1) * HD])
        for kh in range(N_KV):
            vt_ref[kh * VT_ROWS + HD:(kh + 1) * VT_ROWS, :] = jnp.ones(
                (VT_ROWS - HD, SEQ + 2 * BLK), BF16)
        jj = lax.broadcasted_iota(jnp.int32, (BLK, 2 * BLK), 0)
        cc = lax.broadcasted_iota(jnp.int32, (BLK, 2 * BLK), 1)
        flip = jnp.where(jj + cc == BLK, 1.0, 0.0).astype(BF16)
        for blk in range(HALF_SEQ // BLK):
            src = SEQ - (blk + 1) * BLK
            if blk == 0:
                rev = _dot(flip[:, :BLK], f_ref[src:src + BLK, :])
            else:
                rev = _dot(flip, f_ref[src:src + 2 * BLK, :])
            frev_ref[blk * BLK:(blk + 1) * BLK, :] = rev.astype(BF16)
        first_row = lax.broadcasted_iota(jnp.int32, (V7X_BF16_SUBLANES, FG_DIM), 0) == 0
        for g in range(N_FGROUPS):
            lo = g * FG_DIM
            both = jnp.concatenate([f_ref[0:HALF_SEQ, lo:lo + FG_DIM],
                                    frev_ref[:, lo:lo + FG_DIM]], axis=1)
            r = _dot(both, dftc_ref[...])
            uw_ref[0:HALF_SEQ, lo:lo + FG_DIM] = r[:, :FG_DIM].astype(BF16)
            uw_ref[HALF_SEQ:, lo:lo + FG_DIM] = r[:, FG_DIM:].astype(BF16)
            mid = _dot(f_ref[HALF_SEQ:HALF_SEQ + V7X_BF16_SUBLANES, lo:lo + FG_DIM],
                       dftc_ref[0:FG_DIM, 0:FG_DIM])
            head = uw_ref[HALF_SEQ:HALF_SEQ + V7X_BF16_SUBLANES, lo:lo + FG_DIM]
            uw_ref[HALF_SEQ:HALF_SEQ + V7X_BF16_SUBLANES, lo:lo + FG_DIM] = jnp.where(
                first_row, mid, head.astype(F32)).astype(BF16)

    def block_keys(i):
        n = t * BLOCKS_PER_TILE + i
        variant = jnp.where(n == 0, 1, jnp.where(n == SEQ // BLK - 1, 2, 0))
        return pl.multiple_of(n * BLK, BLK), variant

    def scores(i):
        key0, variant = block_keys(i)
        q_t = q_ref[i * BLK:(i + 1) * BLK, :].T
        zero = jnp.zeros((HD, BLK), BF16)
        cols = []
        for h in range(N_HEADS):
            q_h = q_t[h * HD:(h + 1) * HD]
            cols.append(jnp.concatenate([q_h, zero] if h < GQ else [zero, q_h], axis=0))
        s_t = _dot(kpad_ref[pl.ds(key0, 3 * BLK), :], jnp.concatenate(cols, axis=1))
        tops = []
        for h in range(N_HEADS):
            biased = s_t[:, h * BLK:(h + 1) * BLK] + bias_ref[variant, h]
            s_ref[i % 2, h] = biased
            top = biased[0:KEY_CHUNK]
            for k0 in range(KEY_CHUNK, 3 * BLK, KEY_CHUNK):
                top = jnp.maximum(top, biased[k0:k0 + KEY_CHUNK])
            tops.append(jnp.max(top, axis=0, keepdims=True))
        return tops

    def softmax(i, tops):
        buf = i % 2
        sink_terms = []
        for h in range(N_HEADS):
            sink = sink_ref[h] * LOG2E
            m = jnp.maximum(tops[h], sink)
            for k0 in range(0, 3 * BLK, KEY_CHUNK):
                z = s_ref[buf, h, k0:k0 + KEY_CHUNK, :] - m
                p_ref[buf, h, k0:k0 + KEY_CHUNK, :] = jnp.exp2(z).astype(BF16)
            sink_terms.append(jnp.exp2(sink - m))
        return sink_terms

    def weighted_values(i, sink_terms):
        key0, _ = block_keys(i)
        for kh in range(N_KV):
            p_group = jnp.concatenate([p_ref[i % 2, kh * GQ + g] for g in range(GQ)], axis=1)
            o_t = _dot(vt_ref[kh * VT_ROWS:(kh + 1) * VT_ROWS, pl.ds(key0, 3 * BLK)], p_group)
            den = o_t[HD:HD + 1] + jnp.concatenate(sink_terms[kh * GQ:(kh + 1) * GQ], axis=1)
            o_n = o_t[:HD] / den
            for pr in range(GQ // 2):
                pair_t = jnp.concatenate([o_n[:, 2 * pr * BLK:(2 * pr + 1) * BLK],
                                          o_n[:, (2 * pr + 1) * BLK:(2 * pr + 2) * BLK]], axis=0)
                lane0 = (kh * GQ + 2 * pr) * HD
                o_ref[i * BLK:(i + 1) * BLK, lane0:lane0 + 2 * HD] = pair_t.T.astype(BF16)

    half_tile = ROW_TILE // 2
    tops = [scores(0)] + [None] * (BLOCKS_PER_TILE - 1)
    for i in range(BLOCKS_PER_TILE):
        if i + 1 < BLOCKS_PER_TILE:
            tops[i + 1] = scores(i + 1)
        if i < 2:
            rows = slice(i * half_tile, (i + 1) * half_tile)
            freq0 = pl.multiple_of(t * ROW_TILE + i * half_tile, half_tile)
            fmix_ref[rows, :] = _dot(dfts_ref[pl.ds(freq0, half_tile), :], uw_ref[...]).astype(BF16)
        elif i == 2:
            yb_ref[...] = _dot(fmix_ref[...], wb_ref[...])
        weighted_values(i, softmax(i, tops[i]))

    y_a = _dot(o_ref[...], wa_ref[...])
    mix = (gate_ref[:, :D_MODEL].astype(F32) * y_a + gate_ref[:, D_MODEL:].astype(F32) * yb_ref[...])
    out_ref[...] = x1_ref[...] + _dot(mix.astype(BF16), wout_ref[...])


def _resident(shape):
    return pl.BlockSpec(shape, lambda *_: (0,) * len(shape), pipeline_mode=pl.Buffered(1))


def _rows(width, tile=ROW_TILE):
    return pl.BlockSpec((tile, width), lambda i: (i, 0))


def _slab_spec(shape):
    rows, cols = shape
    steps = TOKENS // ROW_TILE
    slab = next(r for r in range(V7X_BF16_SUBLANES, rows + 1, V7X_BF16_SUBLANES)
                if rows % r == 0 and r * steps >= rows)
    return pl.BlockSpec((slab, cols), lambda i: (jnp.minimum(i, rows // slab - 1), 0))


def _ffn_proj(x, g1, wup, wdn, gmix, win, bg, later_weights):
    assert len(later_weights) == N_LATER_WEIGHTS
    slabs = [_slab_spec(w.shape) for w in later_weights]
    hbm = pl.BlockSpec(memory_space=pl.ANY)
    return pl.pallas_call(
        _ffn_proj_kernel,
        grid=(TOKENS // ROW_TILE,),
        in_specs=[_rows(D_MODEL), _resident((1, D_MODEL)), hbm, hbm, _resident((1, D_MODEL)),
                  hbm, _resident((1, 2 * D_MODEL))] + slabs,
        out_specs=[_rows(D_MODEL), _rows(Q_WIDTH), _rows(2 * KV_WIDTH), _rows(F_WIDTH),
                   _rows(2 * D_MODEL)] + slabs,
        out_shape=[jax.ShapeDtypeStruct((TOKENS, D_MODEL), F32),
                   jax.ShapeDtypeStruct((TOKENS, Q_WIDTH), BF16),
                   jax.ShapeDtypeStruct((TOKENS, 2 * KV_WIDTH), BF16),
                   jax.ShapeDtypeStruct((TOKENS, F_WIDTH), BF16),
                   jax.ShapeDtypeStruct((TOKENS, 2 * D_MODEL), BF16)]
        + [jax.ShapeDtypeStruct(w.shape, BF16) for w in later_weights],
        scratch_shapes=[pltpu.VMEM((ROW_TILE, D_FF), BF16),
                        pltpu.VMEM((ROW_TILE, D_MODEL), BF16),
                        pltpu.VMEM((D_MODEL, 2 * D_FF), BF16),
                        pltpu.VMEM((D_FF, D_MODEL), BF16),
                        pltpu.VMEM((D_MODEL, IN_WIDTH), BF16)],
        compiler_params=pltpu.CompilerParams(dimension_semantics=("arbitrary",),
                                             vmem_limit_bytes=V7X_VMEM_LIMIT_BYTES),
        name="ffn_proj",
    )(x, g1, wup, wdn, gmix, win, bg, *later_weights)


def _ffn_final(x, g2, wup, wdn, gfin):
    return pl.pallas_call(
        _ffn_final_kernel,
        grid=(TOKENS // FINAL_ROW_TILE,),
        in_specs=[_rows(D_MODEL, FINAL_ROW_TILE), _resident((1, D_MODEL)),
                  _resident((D_MODEL, 2 * D_FF)), _resident((D_FF, D_MODEL)),
                  _resident((1, D_MODEL))],
        out_specs=_rows(D_MODEL, FINAL_ROW_TILE),
        out_shape=jax.ShapeDtypeStruct((TOKENS, D_MODEL), F32),
        scratch_shapes=[pltpu.VMEM((FINAL_ROW_TILE, D_FF), BF16)],
        compiler_params=pltpu.CompilerParams(dimension_semantics=("arbitrary",),
                                             vmem_limit_bytes=V7X_VMEM_LIMIT_BYTES),
        name="ffn_final",
    )(x, g2, wup, wdn, gfin)


def _mixer(rel_bias, sink, x1, q, kv, f, gates, bucket, dftc, dfts, wa, wb, wout):
    def tile_rows(width):
        return pl.BlockSpec((ROW_TILE, width), lambda b, t: (b * TILES_PER_SEQ + t, 0))

    def seq_rows(width):
        return pl.BlockSpec((SEQ, width), lambda b, t: (b, 0))

    smem = pl.BlockSpec(memory_space=pltpu.SMEM)
    return pl.pallas_call(
        _mixer_kernel,
        grid=(BATCH, TILES_PER_SEQ),
        in_specs=[smem, smem, tile_rows(D_MODEL), tile_rows(Q_WIDTH), seq_rows(2 * KV_WIDTH),
                  seq_rows(F_WIDTH), tile_rows(2 * D_MODEL), _resident((3 * BLK, BLK)),
                  _resident((2 * FG_DIM, 2 * FG_DIM)),
                  _resident((SEQ, SEQ)),
                  _resident((Q_WIDTH, D_MODEL)), _resident((F_WIDTH, D_MODEL)),
                  _resident((D_MODEL, D_MODEL))],
        out_specs=tile_rows(D_MODEL),
        out_shape=jax.ShapeDtypeStruct((TOKENS, D_MODEL), F32),
        scratch_shapes=[pltpu.VMEM((3, N_HEADS, 3 * BLK, BLK), F32),
                        pltpu.VMEM((SEQ + 2 * BLK, KV_WIDTH), BF16),
                        pltpu.VMEM((N_KV * VT_ROWS, SEQ + 2 * BLK), BF16),
                        pltpu.VMEM((HALF_SEQ, F_WIDTH), BF16),
                        pltpu.VMEM((SEQ, F_WIDTH), BF16),
                        pltpu.VMEM((ROW_TILE, Q_WIDTH), BF16),
                        pltpu.VMEM((2, N_HEADS, 3 * BLK, BLK), F32),
                        pltpu.VMEM((2, N_HEADS, 3 * BLK, BLK), BF16),
                        pltpu.VMEM((ROW_TILE, F_WIDTH), BF16),
                        pltpu.VMEM((ROW_TILE, D_MODEL), F32)],
        compiler_params=pltpu.CompilerParams(dimension_semantics=("arbitrary", "arbitrary"),
                                             vmem_limit_bytes=V7X_VMEM_LIMIT_BYTES),
        name="mixer",
    )(rel_bias, sink, x1, q, kv, f, gates, bucket, dftc, dfts, wa, wb, wout)


def kernel(x, g_ffn1, w_up1, w_down1, g_mix, w_in, b_gate, sink, rel_bias, w_branch_a, w_branch_b, w_out, g_ffn2, w_up2, w_down2, g_final):
    assert x.shape == (BATCH, SEQ, D_MODEL) and w_up1.shape[0] == DEPTH == 1
    chan, pos = _dft_tables()
    bucket = jnp.asarray(_t5_bucket_table().T)
    dftc = jnp.asarray(chan).astype(BF16)
    dfts = jnp.asarray(pos).astype(BF16)
    row = lambda v: v.reshape(1, -1)
    x0 = x.reshape(TOKENS, D_MODEL)
    x1, q, kv, f, gates, wa, wb, wout, wup2, wdn2 = _ffn_proj(
        x0, row(g_ffn1[0]), w_up1[0], w_down1[0], row(g_mix[0]), w_in[0], row(b_gate[0]),
        [w_branch_a[0], w_branch_b[0], w_out[0], w_up2[0], w_down2[0]])
    x2 = _mixer(rel_bias, sink[0], x1, q, kv, f, gates, bucket, dftc, dfts, wa, wb, wout)
    out = _ffn_final(x2, row(g_ffn2[0]), wup2, wdn2, row(g_final))
    return out.reshape(BATCH, SEQ, D_MODEL)
```

```python
import functools

import numpy as np
import jax
import jax.numpy as jnp
from jax import lax
from jax.experimental import pallas as pl
from jax.experimental.pallas import tpu as pltpu

D_MODEL = 1024
BATCH = 8
SEQ = 2048
DEPTH = 1
N_HEADS = 8
N_KV = 2
GQ = N_HEADS // N_KV
HD = 64
Q_WIDTH = N_HEADS * HD
KV_WIDTH = N_KV * HD
WINDOW = 128
BLK = 128
N_FGROUPS = 4
FG_DIM = 128
F_WIDTH = N_FGROUPS * FG_DIM
N_BUCKETS = 32
MAX_DIST = 128
D_FF = 2816
EPS = 1e-6
O_K = Q_WIDTH
O_V = O_K + KV_WIDTH
O_F = O_V + KV_WIDTH
O_G = O_F + F_WIDTH
IN_WIDTH = O_G + 2 * D_MODEL

F32 = jnp.float32
BF16 = jnp.bfloat16

V7X_MXU_DIM = 256
V7X_BF16_SUBLANES = 16
V7X_VMEM_LIMIT_BYTES = 56 * 1024 * 1024

TOKENS = BATCH * SEQ
HALF_SEQ = SEQ // 2
ROW_TILE = 512
FINAL_ROW_TILE = 1024
TILES_PER_SEQ = SEQ // ROW_TILE
BLOCKS_PER_TILE = ROW_TILE // BLK
FF_CHUNK = V7X_MXU_DIM
FFN_SUB_ROWS = 256
VT_ROWS = HD + V7X_BF16_SUBLANES
N_LATER_WEIGHTS = 5
STAGE_SLOTS = 4
STAGE_BYTES = 3 * 512 * 1024
KEY_CHUNK = 64
LOG2E = float(np.log2(np.e))


def _t5_bucket_table():
    rel = (np.arange(3 * BLK)[None, :] - BLK) - np.arange(BLK)[:, None]
    half = N_BUCKETS // 2
    max_exact = half // 2
    ret = (rel > 0).astype(np.int32) * half
    n = np.abs(rel)
    n_safe = np.maximum(n, 1).astype(np.float32)
    large = max_exact + (np.log(n_safe / max_exact) / np.log(MAX_DIST / max_exact)
                         * (half - max_exact)).astype(np.int32)
    large = np.minimum(large, half - 1)
    return (ret + np.where(n < max_exact, n, large)).astype(np.int32)


@functools.lru_cache(maxsize=None)
def _dft_tables():
    def cos_sin(n):
        idx = np.arange(n)
        ang = 2.0 * np.pi * ((idx[:, None] * idx[None, :]) % n).astype(np.float64) / n
        return np.cos(ang) / np.sqrt(n), np.sin(ang) / np.sqrt(n)
    cc, sc = cos_sin(FG_DIM)
    cs, ss = cos_sin(SEQ)
    chan = np.block([[cc, sc], [cc, -sc]]).astype(np.float32)
    neg_sin = -ss[:, :HALF_SEQ]
    neg_sin[:, 0] = np.where(np.arange(SEQ) % 2 == 0, 1.0, -1.0) / np.sqrt(SEQ)
    pos = np.concatenate([cs[:, :HALF_SEQ], neg_sin], axis=1).astype(np.float32)
    return chan, pos


def _rms(x, g):
    return x * lax.rsqrt(jnp.mean(x * x, axis=-1, keepdims=True) + EPS) * g


def _sigmoid(z):
    return 1.0 / (1.0 + jnp.exp2(z * -LOG2E))


def _dot(a, b):
    return jnp.dot(a, b, preferred_element_type=F32)


def _skewed_ffn(x_ref, g_ref, wup_ref, wdn_ref, act_ref, finish, finish_late=None):
    n_sub = x_ref.shape[0] // FFN_SUB_ROWS
    rows = [slice(k * FFN_SUB_ROWS, (k + 1) * FFN_SUB_ROWS) for k in range(n_sub)]
    normed = [None] * n_sub

    def prologue(k):
        normed[k] = _rms(x_ref[rows[k], :], g_ref[...]).astype(BF16)

    def up_chunk(k, c):
        lo = c * FF_CHUNK
        gate = _dot(normed[k], wup_ref[:, lo:lo + FF_CHUNK])
        up = _dot(normed[k], wup_ref[:, D_FF + lo:D_FF + lo + FF_CHUNK])
        act_ref[rows[k], lo:lo + FF_CHUNK] = (gate * _sigmoid(gate) * up).astype(BF16)

    def epilogue(k):
        finish(k, rows[k], x_ref[rows[k], :] + 0.5 * _dot(act_ref[rows[k], :], wdn_ref[...]))

    prologue(0)
    for k in range(n_sub):
        for c in range(D_FF // FF_CHUNK):
            up_chunk(k, c)
            if c == 0 and k + 1 < n_sub:
                prologue(k + 1)
            if c == 1 and k > 0:
                epilogue(k - 1)
            if c == 3 and 0 < k < n_sub - 1 and finish_late is not None:
                finish_late(k - 1, rows[k - 1])
    epilogue(n_sub - 1)
    if finish_late is not None:
        for k in range(max(n_sub - 2, 0), n_sub):
            finish_late(k, rows[k])


def _stage_weight(src_hbm, dst_ref):
    rows, cols = dst_ref.shape
    slab = max(r for r in range(V7X_BF16_SUBLANES, rows + 1, V7X_BF16_SUBLANES)
               if rows % r == 0 and r * cols * 4 <= STAGE_BYTES)
    n_slabs = rows // slab

    def staged(stage_ref, sem):
        def slab_copy(c, slot):
            return pltpu.make_async_copy(src_hbm.at[pl.ds(c * slab, slab), :],
                                         stage_ref.at[slot], sem.at[slot])

        for c in range(min(STAGE_SLOTS - 1, n_slabs)):
            slab_copy(c, c).start()

        def step(c, carry):
            ahead = c + STAGE_SLOTS - 1

            @pl.when(ahead < n_slabs)
            def _start_ahead():
                slab_copy(ahead, ahead % STAGE_SLOTS).start()

            slot = c % STAGE_SLOTS
            slab_copy(c, slot).wait()
            dst_ref[pl.ds(pl.multiple_of(c * slab, slab), slab), :] = stage_ref[slot].astype(BF16)
            return carry

        lax.fori_loop(0, n_slabs, step, 0)

    pl.run_scoped(staged, pltpu.VMEM((STAGE_SLOTS, slab, cols), F32),
                  pltpu.SemaphoreType.DMA((STAGE_SLOTS,)))


def _ffn_proj_kernel(x_ref, g1_ref, wup_hbm, wdn_hbm, gmix_ref, win_hbm, bg_ref, *refs):
    later_f32 = refs[:N_LATER_WEIGHTS]
    x1_ref, q_ref, kv_ref, f_ref, gate_ref = refs[N_LATER_WEIGHTS:N_LATER_WEIGHTS + 5]
    later_bf16 = refs[N_LATER_WEIGHTS + 5:2 * N_LATER_WEIGHTS + 5]
    act_ref, h2_ref, wup_ref, wdn_ref, win_ref = refs[2 * N_LATER_WEIGHTS + 5:]

    @pl.when(pl.program_id(0) == 0)
    def _stage_own_weights():
        _stage_weight(wup_hbm, wup_ref)
        _stage_weight(wdn_hbm, wdn_ref)
        _stage_weight(win_hbm, win_ref)

    for src, dst in zip(later_f32, later_bf16):
        dst[...] = src[...].astype(BF16)

    def second_norm(k, rows, x1):
        x1_ref[rows, :] = x1
        h2_ref[rows, :] = _rms(x1, gmix_ref[...]).astype(BF16)

    def project(k, rows):
        h2 = h2_ref[rows, :]
        for c in range(2):
            lo = c * D_MODEL
            z = _dot(h2, win_ref[:, O_G + lo:O_G + lo + D_MODEL]) + bg_ref[:, lo:lo + D_MODEL]
            gate_ref[rows, lo:lo + D_MODEL] = _sigmoid(z)
        q_ref[rows, :] = (_dot(h2, win_ref[:, 0:O_K]) * (HD ** -0.5 * LOG2E)).astype(BF16)
        kv_ref[rows, :] = _dot(h2, win_ref[:, O_K:O_F]).astype(BF16)
        f_ref[rows, :] = _dot(h2, win_ref[:, O_F:O_G]).astype(BF16)

    _skewed_ffn(x_ref, g1_ref, wup_ref, wdn_ref, act_ref, second_norm, finish_late=project)


def _ffn_final_kernel(x_ref, g2_ref, wup_ref, wdn_ref, gfin_ref, out_ref, act_ref):
    def final_norm(k, rows, x3):
        out_ref[rows, :] = _rms(x3, gfin_ref[...])

    _skewed_ffn(x_ref, g2_ref, wup_ref, wdn_ref, act_ref, final_norm)


def _mixer_kernel(relb_ref, sink_ref, x1_ref, q_ref, kv_ref, f_ref, gate_ref, bucket_ref,
                  dftc_ref, dfts_ref, wa_ref, wb_ref, wout_ref, out_ref,
                  bias_ref, kpad_ref, vt_ref, frev_ref, uw_ref, o_ref, s_ref, p_ref,
                  fmix_ref, yb_ref):
    b = pl.program_id(0)
    t = pl.program_id(1)

    @pl.when((b == 0) & (t == 0))
    def _build_bias():
        bucket = bucket_ref[...]
        kj = lax.broadcasted_iota(jnp.int32, (3 * BLK, BLK), 0)
        qi = lax.broadcasted_iota(jnp.int32, (3 * BLK, BLK), 1)
        in_window = jnp.abs(kj - BLK - qi) <= WINDOW
        valid = (in_window, in_window & (kj >= BLK), in_window & (kj < 2 * BLK))
        for h in range(N_HEADS):
            def pick(bk, acc, h=h):
                return jnp.where(bucket == bk, relb_ref[bk, h] * LOG2E, acc)
            base = lax.fori_loop(0, N_BUCKETS, pick, jnp.zeros((3 * BLK, BLK), F32))
            for v in range(3):
                bias_ref[v, h] = jnp.where(valid[v], base, -jnp.inf)

    @pl.when(t == 0)
    def _per_sequence():
        kpad_ref[0:BLK, :] = jnp.zeros((BLK, KV_WIDTH), BF16)
        kpad_ref[BLK:BLK + SEQ, :] = kv_ref[:, :KV_WIDTH]
        kpad_ref[BLK + SEQ:, :] = jnp.zeros((BLK, KV_WIDTH), BF16)
        vt_ref[:, 0:BLK] = jnp.zeros((N_KV * VT_ROWS, BLK), BF16)
        vt_ref[:, BLK + SEQ:] = jnp.zeros((N_KV * VT_ROWS, BLK), BF16)
        for c in range(SEQ // BLK):
            v_t = kv_ref[c * BLK:(c + 1) * BLK, KV_WIDTH:].T
            for kh in range(N_KV):
                vt_ref[kh * VT_ROWS:kh * VT_ROWS + HD, (c + 1) * BLK:(c + 2) * BLK] = (
                    v_t[kh * HD:(kh + 1) * HD])
        for kh in range(N_KV):
            vt_ref[kh * VT_ROWS + HD:(kh + 1) * VT_ROWS, :] = jnp.ones(
                (VT_ROWS - HD, SEQ + 2 * BLK), BF16)
        jj = lax.broadcasted_iota(jnp.int32, (BLK, 2 * BLK), 0)
        cc = lax.broadcasted_iota(jnp.int32, (BLK, 2 * BLK), 1)
        flip = jnp.where(jj + cc == BLK, 1.0, 0.0).astype(BF16)
        for blk in range(HALF_SEQ // BLK):
            src = SEQ - (blk + 1) * BLK
            if blk == 0:
                rev = _dot(flip[:, :BLK], f_ref[src:src + BLK, :])
            else:
                rev = _dot(flip, f_ref[src:src + 2 * BLK, :])
            frev_ref[blk * BLK:(blk + 1) * BLK, :] = rev.astype(BF16)
        first_row = lax.broadcasted_iota(jnp.int32, (V7X_BF16_SUBLANES, FG_DIM), 0) == 0
        for g in range(N_FGROUPS):
            lo = g * FG_DIM
            both = jnp.concatenate([f_ref[0:HALF_SEQ, lo:lo + FG_DIM],
                                    frev_ref[:, lo:lo + FG_DIM]], axis=1)
            r = _dot(both, dftc_ref[...])
            uw_ref[0:HALF_SEQ, lo:lo + FG_DIM] = r[:, :FG_DIM].astype(BF16)
            uw_ref[HALF_SEQ:, lo:lo + FG_DIM] = r[:, FG_DIM:].astype(BF16)
            mid = _dot(f_ref[HALF_SEQ:HALF_SEQ + V7X_BF16_SUBLANES, lo:lo + FG_DIM],
                       dftc_ref[0:FG_DIM, 0:FG_DIM])
            head = uw_ref[HALF_SEQ:HALF_SEQ + V7X_BF16_SUBLANES, lo:lo + FG_DIM]
            uw_ref[HALF_SEQ:HALF_SEQ + V7X_BF16_SUBLANES, lo:lo + FG_DIM] = jnp.where(
                first_row, mid, head.astype(F32)).astype(BF16)

    def block_keys(i):
        n = t * BLOCKS_PER_TILE + i
        variant = jnp.where(n == 0, 1, jnp.where(n == SEQ // BLK - 1, 2, 0))
        return pl.multiple_of(n * BLK, BLK), variant

    def scores(i):
        key0, variant = block_keys(i)
        q_t = q_ref[i * BLK:(i + 1) * BLK, :].T
        zero = jnp.zeros((HD, BLK), BF16)
        cols = []
        for h in range(N_HEADS):
            q_h = q_t[h * HD:(h + 1) * HD]
            cols.append(jnp.concatenate([q_h, zero] if h < GQ else [zero, q_h], axis=0))
        s_t = _dot(kpad_ref[pl.ds(key0, 3 * BLK), :], jnp.concatenate(cols, axis=1))
        tops = []
        for h in range(N_HEADS):
            biased = s_t[:, h * BLK:(h + 1) * BLK] + bias_ref[variant, h]
            s_ref[i % 2, h] = biased
            top = biased[0:KEY_CHUNK]
            for k0 in range(KEY_CHUNK, 3 * BLK, KEY_CHUNK):
                top = jnp.maximum(top, biased[k0:k0 + KEY_CHUNK])
            tops.append(jnp.max(top, axis=0, keepdims=True))
        return tops

    def softmax(i, tops):
        buf = i % 2
        sink_terms = []
        for h in range(N_HEADS):
            sink = sink_ref[h] * LOG2E
            m = jnp.maximum(tops[h], sink)
            for k0 in range(0, 3 * BLK, KEY_CHUNK):
                z = s_ref[buf, h, k0:k0 + KEY_CHUNK, :] - m
                p_ref[buf, h, k0:k0 + KEY_CHUNK, :] = jnp.exp2(z).astype(BF16)
            sink_terms.append(jnp.exp2(sink - m))
        return sink_terms

    def weighted_values(i, sink_terms):
        key0, _ = block_keys(i)
        for kh in range(N_KV):
            p_group = jnp.concatenate([p_ref[i % 2, kh * GQ + g] for g in range(GQ)], axis=1)
            o_t = _dot(vt_ref[kh * VT_ROWS:(kh + 1) * VT_ROWS, pl.ds(key0, 3 * BLK)], p_group)
            den = o_t[HD:HD + 1] + jnp.concatenate(sink_terms[kh * GQ:(kh + 1) * GQ], axis=1)
            o_n = o_t[:HD] / den
            for pr in range(GQ // 2):
                pair_t = jnp.concatenate([o_n[:, 2 * pr * BLK:(2 * pr + 1) * BLK],
                                          o_n[:, (2 * pr + 1) * BLK:(2 * pr + 2) * BLK]], axis=0)
                lane0 = (kh * GQ + 2 * pr) * HD
                o_ref[i * BLK:(i + 1) * BLK, lane0:lane0 + 2 * HD] = pair_t.T.astype(BF16)

    half_tile = ROW_TILE // 2
    tops = [scores(0)] + [None] * (BLOCKS_PER_TILE - 1)
    for i in range(BLOCKS_PER_TILE):
        if i + 1 < BLOCKS_PER_TILE:
            tops[i + 1] = scores(i + 1)
        if i < 2:
            rows = slice(i * half_tile, (i + 1) * half_tile)
            freq0 = pl.multiple_of(t * ROW_TILE + i * half_tile, half_tile)
            fmix_ref[rows, :] = _dot(dfts_ref[pl.ds(freq0, half_tile), :], uw_ref[...]).astype(BF16)
        elif i == 2:
            yb_ref[...] = _dot(fmix_ref[...], wb_ref[...])
        weighted_values(i, softmax(i, tops[i]))

    y_a = _dot(o_ref[...], wa_ref[...])
    mix = gate_ref[:, :D_MODEL] * y_a + gate_ref[:, D_MODEL:] * yb_ref[...]
    out_ref[...] = x1_ref[...] + _dot(mix.astype(BF16), wout_ref[...])


def _resident(shape):
    return pl.BlockSpec(shape, lambda *_: (0,) * len(shape), pipeline_mode=pl.Buffered(1))


def _rows(width, tile=ROW_TILE):
    return pl.BlockSpec((tile, width), lambda i: (i, 0))


def _slab_spec(shape):
    rows, cols = shape
    steps = TOKENS // ROW_TILE
    slab = next(r for r in range(V7X_BF16_SUBLANES, rows + 1, V7X_BF16_SUBLANES)
                if rows % r == 0 and r * steps >= rows)
    return pl.BlockSpec((slab, cols), lambda i: (jnp.minimum(i, rows // slab - 1), 0))


def _ffn_proj(x, g1, wup, wdn, gmix, win, bg, later_weights):
    assert len(later_weights) == N_LATER_WEIGHTS
    slabs = [_slab_spec(w.shape) for w in later_weights]
    hbm = pl.BlockSpec(memory_space=pl.ANY)
    return pl.pallas_call(
        _ffn_proj_kernel,
        grid=(TOKENS // ROW_TILE,),
        in_specs=[_rows(D_MODEL), _resident((1, D_MODEL)), hbm, hbm, _resident((1, D_MODEL)),
                  hbm, _resident((1, 2 * D_MODEL))] + slabs,
        out_specs=[_rows(D_MODEL), _rows(Q_WIDTH), _rows(2 * KV_WIDTH), _rows(F_WIDTH),
                   _rows(2 * D_MODEL)] + slabs,
        out_shape=[jax.ShapeDtypeStruct((TOKENS, D_MODEL), F32),
                   jax.ShapeDtypeStruct((TOKENS, Q_WIDTH), BF16),
                   jax.ShapeDtypeStruct((TOKENS, 2 * KV_WIDTH), BF16),
                   jax.ShapeDtypeStruct((TOKENS, F_WIDTH), BF16),
                   jax.ShapeDtypeStruct((TOKENS, 2 * D_MODEL), F32)]
        + [jax.ShapeDtypeStruct(w.shape, BF16) for w in later_weights],
        scratch_shapes=[pltpu.VMEM((ROW_TILE, D_FF), BF16),
                        pltpu.VMEM((ROW_TILE, D_MODEL), BF16),
                        pltpu.VMEM((D_MODEL, 2 * D_FF), BF16),
                        pltpu.VMEM((D_FF, D_MODEL), BF16),
                        pltpu.VMEM((D_MODEL, IN_WIDTH), BF16)],
        compiler_params=pltpu.CompilerParams(dimension_semantics=("arbitrary",),
                                             vmem_limit_bytes=V7X_VMEM_LIMIT_BYTES),
        name="ffn_proj",
    )(x, g1, wup, wdn, gmix, win, bg, *later_weights)


def _ffn_final(x, g2, wup, wdn, gfin):
    return pl.pallas_call(
        _ffn_final_kernel,
        grid=(TOKENS // FINAL_ROW_TILE,),
        in_specs=[_rows(D_MODEL, FINAL_ROW_TILE), _resident((1, D_MODEL)),
                  _resident((D_MODEL, 2 * D_FF)), _resident((D_FF, D_MODEL)),
                  _resident((1, D_MODEL))],
        out_specs=_rows(D_MODEL, FINAL_ROW_TILE),
        out_shape=jax.ShapeDtypeStruct((TOKENS, D_MODEL), F32),
        scratch_shapes=[pltpu.VMEM((FINAL_ROW_TILE, D_FF), BF16)],
        compiler_params=pltpu.CompilerParams(dimension_semantics=("arbitrary",),
                                             vmem_limit_bytes=V7X_VMEM_LIMIT_BYTES),
        name="ffn_final",
    )(x, g2, wup, wdn, gfin)


def _mixer(rel_bias, sink, x1, q, kv, f, gates, bucket, dftc, dfts, wa, wb, wout):
    def tile_rows(width):
        return pl.BlockSpec((ROW_TILE, width), lambda b, t: (b * TILES_PER_SEQ + t, 0))

    def seq_rows(width):
        return pl.BlockSpec((SEQ, width), lambda b, t: (b, 0))

    smem = pl.BlockSpec(memory_space=pltpu.SMEM)
    return pl.pallas_call(
        _mixer_kernel,
        grid=(BATCH, TILES_PER_SEQ),
        in_specs=[smem, smem, tile_rows(D_MODEL), tile_rows(Q_WIDTH), seq_rows(2 * KV_WIDTH),
                  seq_rows(F_WIDTH), tile_rows(2 * D_MODEL), _resident((3 * BLK, BLK)),
                  _resident((2 * FG_DIM, 2 * FG_DIM)),
                  _resident((SEQ, SEQ)),
                  _resident((Q_WIDTH, D_MODEL)), _resident((F_WIDTH, D_MODEL)),
                  _resident((D_MODEL, D_MODEL))],
        out_specs=tile_rows(D_MODEL),
        out_shape=jax.ShapeDtypeStruct((TOKENS, D_MODEL), F32),
        scratch_shapes=[pltpu.VMEM((3, N_HEADS, 3 * BLK, BLK), F32),
                        pltpu.VMEM((SEQ + 2 * BLK, KV_WIDTH), BF16),
                        pltpu.VMEM((N_KV * VT_ROWS, SEQ + 2 * BLK), BF16),
                        pltpu.VMEM((HALF_SEQ, F_WIDTH), BF16),
                        pltpu.VMEM((SEQ, F_WIDTH), BF16),
                        pltpu.VMEM((ROW_TILE, Q_WIDTH), BF16),
                        pltpu.VMEM((2, N_HEADS, 3 * BLK, BLK), F32),
                        pltpu.VMEM((2, N_HEADS, 3 * BLK, BLK), BF16),
                        pltpu.VMEM((ROW_TILE, F_WIDTH), BF16),
                        pltpu.VMEM((ROW_TILE, D_MODEL), F32)],
        compiler_params=pltpu.CompilerParams(dimension_semantics=("arbitrary", "arbitrary"),
                                             vmem_limit_bytes=V7X_VMEM_LIMIT_BYTES),
        name="mixer",
    )(rel_bias, sink, x1, q, kv, f, gates, bucket, dftc, dfts, wa, wb, wout)


def kernel(x, g_ffn1, w_up1, w_down1, g_mix, w_in, b_gate, sink, rel_bias, w_branch_a, w_branch_b, w_out, g_ffn2, w_up2, w_down2, g_final):
    assert x.shape == (BATCH, SEQ, D_MODEL) and w_up1.shape[0] == DEPTH == 1
    chan, pos = _dft_tables()
    bucket = jnp.asarray(_t5_bucket_table().T)
    dftc = jnp.asarray(chan).astype(BF16)
    dfts = jnp.asarray(pos).astype(BF16)
    row = lambda v: v.reshape(1, -1)
    x0 = x.reshape(TOKENS, D_MODEL)
    x1, q, kv, f, gates, wa, wb, wout, wup2, wdn2 = _ffn_proj(
        x0, row(g_ffn1[0]), w_up1[0], w_down1[0], row(g_mix[0]), w_in[0], row(b_gate[0]),
        [w_branch_a[0], w_branch_b[0], w_out[0], w_up2[0], w_down2[0]])
    x2 = _mixer(rel_bias, sink[0], x1, q, kv, f, gates, bucket, dftc, dfts, wa, wb, wout)
    out = _ffn_final(x2, row(g_ffn2[0]), wup2, wdn2, row(g_final))
    return out.reshape(BATCH, SEQ, D_MODEL)
```
